```python
import math
import functools
import jax, jax.numpy as jnp
from jax import lax
import numpy as np

D_MODEL = 4096
BATCH = 4
SEQ = 2048
DEPTH = 1
DEC_BATCH = 128
DEC_SEQ = 4
PAST_LEN = 2048
PAGE_SIZE = 128

D_RNN = D_MODEL // 2
RNN_BLOCKS = 16
RNN_BW = D_RNN // RNN_BLOCKS
CONV_W = 4
LRU_C = 8.0
N_HEADS = 16
HEAD_DIM = 128
N_KV = 4
GROUP = N_HEADS // N_KV
CMP_LEN = 32
CMP_STRIDE = 16
CMP_HID = 2 * HEAD_DIM
SEL_BLK = 64
N_SEL = 8
WINDOW = 512
Q_BLOCK = 128
N_BUCKETS = 32
MAX_EXACT = 16
MAX_DIST = 128
D_FF = ((8 * D_MODEL // 3 + 255) // 256) * 256
N_PAGED = 4
N_WIN = 2
IN_SIZES = (D_RNN, D_RNN, N_HEADS * HEAD_DIM, N_PAGED * N_KV * HEAD_DIM, N_WIN * N_KV * HEAD_DIM, 3 * N_HEADS, 2 * D_MODEL)
D_IN = sum(IN_SIZES)
EPS = 1e-6
NEG = -1e30
BIG = 1e30

kernel_name = "hybrid_rglru_nsa_macaron_step"


def rmsnorm(x, g):
    xf = x.astype(jnp.float32)
    y = xf * lax.rsqrt(jnp.mean(xf * xf, axis=-1, keepdims=True) + EPS)
    return (y * g.astype(jnp.float32)).astype(x.dtype)


def swiglu(x, w_gate, w_up, w_down):
    return (jax.nn.silu(x @ w_gate) * (x @ w_up)) @ w_down


def split_cols(z, sizes):
    outs, start = [], 0
    for s in sizes:
        outs.append(z[..., start:start + s])
        start += s
    return outs


def t5_bucket(dist):
    d = jnp.maximum(dist, 0)
    df = jnp.maximum(d, 1).astype(jnp.float32)
    large = MAX_EXACT + (jnp.log(df / MAX_EXACT) / math.log(MAX_DIST / MAX_EXACT)
                         * (N_BUCKETS - MAX_EXACT)).astype(jnp.int32)
    large = jnp.minimum(large, N_BUCKETS - 1)
    return jnp.where(d < MAX_EXACT, d, large)


def head_bias(dist, rel_bias):
    b = rel_bias[t5_bucket(dist)].astype(jnp.float32)
    return jnp.transpose(b.reshape(dist.shape + (N_KV, GROUP)), (2, 3, 0, 1))


def masked_softmax(logits, mask):
    p = jax.nn.softmax(jnp.where(mask, logits, NEG), axis=-1)
    return jnp.where(mask, p, 0.0)


def overlap_matrix(nc, nb):
    cs = np.arange(nc)[:, None] * CMP_STRIDE
    js = np.arange(nb)[None, :] * SEL_BLK
    ov = np.clip(np.minimum(cs + CMP_LEN, js + SEL_BLK) - np.maximum(cs, js), 0, None) / CMP_LEN
    return jnp.asarray(ov, jnp.float32)


def compress(rows, pos, w1, b1, w2):
    n, length = rows.shape[:2]
    nc = (length - CMP_LEN) // CMP_STRIDE + 1
    r_seg = CMP_LEN // CMP_STRIDE
    nseg = nc + r_seg - 1
    seg = rows[:, :nseg * CMP_STRIDE].reshape(n, nseg, CMP_STRIDE, N_KV, HEAD_DIM)
    w1b = w1.reshape(r_seg, CMP_STRIDE, HEAD_DIM, CMP_HID)
    hid = b1 + jnp.einsum("lc,lch->h", pos, w1.reshape(CMP_LEN, HEAD_DIM, CMP_HID))
    for r in range(r_seg):
        hid = hid + jnp.einsum("nslkc,lch->nskh", seg[:, r:r + nc], w1b[r])
    return jnp.einsum("nskh,hc->nskc", jax.nn.gelu(hid), w2)


def nsa_attend(q, qpos, gates, ck, cv, cend, sk_t, sv_t, wk, wv, wpos, ov, rel_bias):
    n, tq = q.shape[:2]
    qs = q * (HEAD_DIM ** -0.5)
    dist_c = qpos[:, None] - cend[None, :]
    valid_c = dist_c >= 0
    lc = jnp.einsum("ntkgd,nckd->nkgtc", qs, ck).astype(jnp.float32) + head_bias(dist_c, rel_bias)
    pc = masked_softmax(lc, valid_c)
    o_c = jnp.einsum("nkgtc,nckd->ntkgd", pc.astype(cv.dtype), cv)
    nb = sk_t.shape[2]
    blk = jnp.arange(nb)[None, :]
    cur = (qpos // SEL_BLK)[:, None]
    valid_b = blk <= cur
    forced = (blk == 0) | (blk == cur) | (blk == cur - 1)
    score = jnp.einsum("nkgtc,cj->nktj", pc, ov)
    score = jnp.where(forced, BIG, jnp.where(valid_b, score, NEG))
    _, idx = lax.top_k(score, min(N_SEL, nb))
    sel_ok = idx <= cur[None, None]
    n_i = jnp.arange(n)[:, None, None, None]
    k_i = jnp.arange(N_KV)[None, :, None, None]
    gk = sk_t[n_i, k_i, idx]
    gv = sv_t[n_i, k_i, idx]
    spos = idx[..., None] * SEL_BLK + jnp.arange(SEL_BLK)
    dist_s = qpos[None, None, :, None, None] - spos
    m_s = (dist_s >= 0) & sel_ok[..., None]
    tab = rel_bias.reshape(N_BUCKETS, N_KV, GROUP)
    b_s = tab[t5_bucket(dist_s), k_i[..., None]]
    ls = jnp.einsum("ntkgd,nktsbd->nkgtsb", qs, gk).astype(jnp.float32) + jnp.moveaxis(b_s, -1, 2).astype(jnp.float32)
    shp = ls.shape
    ps = masked_softmax(ls.reshape(shp[:4] + (-1,)), m_s.reshape(n, N_KV, 1, tq, -1)).reshape(shp)
    o_s = jnp.einsum("nkgtsb,nktsbd->ntkgd", ps.astype(gv.dtype), gv)
    dist_w = qpos[:, None] - wpos[None, :]
    m_w = (dist_w >= 0) & (dist_w < WINDOW) & (wpos[None, :] >= 0)
    lw = jnp.einsum("ntkgd,nskd->nkgts", qs, wk).astype(jnp.float32) + head_bias(dist_w, rel_bias)
    pw = masked_softmax(lw, m_w)
    o_w = jnp.einsum("nkgts,nskd->ntkgd", pw.astype(wv.dtype), wv)
    return gates[..., 0:1] * o_c + gates[..., 1:2] * o_s + gates[..., 2:3] * o_w


def attend_prompt(q, gates, kv_pag, kv_win, cp, rel_bias):
    n, t = q.shape[:2]
    ck = compress(kv_pag[:, :, 0], cp[0], cp[1], cp[2], cp[3])
    cv = compress(kv_pag[:, :, 1], cp[0], cp[4], cp[5], cp[6])
    nc = ck.shape[1]
    cend = jnp.arange(nc) * CMP_STRIDE + CMP_LEN - 1
    nb = t // SEL_BLK
    sel = kv_pag[:, :, 2:4].reshape(n, nb, SEL_BLK, 2, N_KV, HEAD_DIM)
    sk_t = jnp.transpose(sel[:, :, :, 0], (0, 3, 1, 2, 4))
    sv_t = jnp.transpose(sel[:, :, :, 1], (0, 3, 1, 2, 4))
    ov = overlap_matrix(nc, nb)
    wpad = jnp.pad(kv_win, ((0, 0), (WINDOW, 0), (0, 0), (0, 0), (0, 0)))

    def one_block(s):
        qpos = s + jnp.arange(Q_BLOCK)
        wpos = s - WINDOW + jnp.arange(WINDOW + Q_BLOCK)
        qb = lax.dynamic_slice_in_dim(q, s, Q_BLOCK, axis=1)
        gb = lax.dynamic_slice_in_dim(gates, s, Q_BLOCK, axis=1)
        wb = lax.dynamic_slice_in_dim(wpad, s, WINDOW + Q_BLOCK, axis=1)
        return nsa_attend(qb, qpos, gb, ck, cv, cend, sk_t, sv_t, wb[:, :, 0], wb[:, :, 1], wpos, ov, rel_bias)

    out = lax.map(one_block, jnp.arange(t // Q_BLOCK) * Q_BLOCK)
    out = jnp.moveaxis(out, 0, 1).reshape(n, t, N_HEADS * HEAD_DIM)
    new_win = kv_win[:, t - min(WINDOW, t):]
    return out, new_win


def attend_sample(q, gates, kv_pag, kv_win, past_kv, win_buf, cp, rel_bias):
    n, t = q.shape[:2]
    past_len = past_kv.shape[1]
    full = jnp.concatenate([past_kv, kv_pag], axis=1)
    length = past_len + t
    ck = compress(full[:, :, 0], cp[0], cp[1], cp[2], cp[3])
    cv = compress(full[:, :, 1], cp[0], cp[4], cp[5], cp[6])
    nc = ck.shape[1]
    cend = jnp.arange(nc) * CMP_STRIDE + CMP_LEN - 1
    nb = -(-length // SEL_BLK)
    sel = jnp.pad(full[:, :, 2:4], ((0, 0), (0, nb * SEL_BLK - length), (0, 0), (0, 0), (0, 0)))
    sel = sel.reshape(n, nb, SEL_BLK, 2, N_KV, HEAD_DIM)
    sk_t = jnp.transpose(sel[:, :, :, 0], (0, 3, 1, 2, 4))
    sv_t = jnp.transpose(sel[:, :, :, 1], (0, 3, 1, 2, 4))
    ov = overlap_matrix(nc, nb)
    wall = jnp.concatenate([win_buf.astype(kv_win.dtype), kv_win], axis=1)
    w_buf = win_buf.shape[1]
    wpos = past_len - w_buf + jnp.arange(w_buf + t)
    qpos = past_len + jnp.arange(t)
    out = nsa_attend(q, qpos, gates, ck, cv, cend, sk_t, sv_t, wall[:, :, 0], wall[:, :, 1], wpos, ov, rel_bias)
    keep = min(WINDOW, wall.shape[1])
    return out.reshape(n, t, N_HEADS * HEAD_DIM), wall[:, wall.shape[1] - keep:]


def causal_conv(u, buf, w, b):
    t = u.shape[1]
    full = jnp.concatenate([buf.astype(u.dtype), u], axis=1)
    y = b
    for k in range(CONV_W):
        y = y + full[:, k:k + t] * w[k]
    return y, full[:, t:]


def rg_lru(xc, h0, w_a, b_a, w_i, b_i, lam):
    n, t, _ = xc.shape
    xb = xc.reshape(n, t, RNN_BLOCKS, RNN_BW)
    r = jax.nn.sigmoid((jnp.einsum("ntkc,kcd->ntkd", xb, w_a).reshape(n, t, D_RNN) + b_a).astype(jnp.float32))
    i = jax.nn.sigmoid((jnp.einsum("ntkc,kcd->ntkd", xb, w_i).reshape(n, t, D_RNN) + b_i).astype(jnp.float32))
    log_a = -LRU_C * r * jax.nn.softplus(-lam.astype(jnp.float32))
    a = jnp.exp(log_a)
    bt = jnp.sqrt(-jnp.expm1(2.0 * log_a)) * (i * xc.astype(jnp.float32))
    bt = bt.at[:, 0].add(a[:, 0] * h0.astype(jnp.float32))

    def combine(left, right):
        a_l, b_l = left
        a_r, b_r = right
        return a_l * a_r, a_r * b_l + b_r

    _, h = lax.associative_scan(combine, (a, bt), axis=1)
    return h.astype(xc.dtype), h[:, -1].astype(xc.dtype)


def layer_forward(x, conv_buf, h0, attend, lp):
    n, t, _ = x.shape
    x = x + 0.5 * swiglu(rmsnorm(x, lp["ln_ffn1"]), lp["w_ffn1_gate"], lp["w_ffn1_up"], lp["w_ffn1_down"])
    hn = rmsnorm(x, lp["ln_mix"])
    u_gate, u_x, q, kv_pag, kv_win, g_nsa, g_merge = split_cols(hn @ lp["w_in"], IN_SIZES)
    xc, new_conv = causal_conv(u_x, conv_buf, lp["conv_w"], lp["conv_b"])
    hs, h_last = rg_lru(xc, h0, lp["rg_wa"], lp["rg_ba"], lp["rg_wi"], lp["rg_bi"], lp["rg_lambda"])
    y_rnn = (hs * jax.nn.gelu(u_gate)) @ lp["w_br_rnn"]
    q = q.reshape(n, t, N_KV, GROUP, HEAD_DIM)
    kv_pag = kv_pag.reshape(n, t, N_PAGED, N_KV, HEAD_DIM)
    kv_win = kv_win.reshape(n, t, N_WIN, N_KV, HEAD_DIM)
    gates = jax.nn.sigmoid(g_nsa.reshape(n, t, N_KV, GROUP, 3))
    o_attn, new_win = attend(q, gates, kv_pag, kv_win)
    y_attn = o_attn @ lp["w_br_attn"]
    g_rnn, g_attn = split_cols(g_merge, (D_MODEL, D_MODEL))
    x = x + (jax.nn.sigmoid(g_rnn) * y_rnn + jax.nn.sigmoid(g_attn) * y_attn) @ lp["w_out"]
    x = x + 0.5 * swiglu(rmsnorm(x, lp["ln_ffn2"]), lp["w_ffn2_gate"], lp["w_ffn2_up"], lp["w_ffn2_down"])
    return x, kv_pag, new_win, new_conv, h_last


def setup_inputs(seed: int = 0) -> dict:
    key = jax.random.key(seed)
    ks = iter(jax.random.split(key, 48))

    def nrm(shape, scale):
        return jax.random.normal(next(ks), shape, jnp.float32) * scale

    def gain(shape):
        return 1.0 + nrm(shape, 0.02)

    n_pages = PAST_LEN // PAGE_SIZE
    n_pool = (DEC_BATCH * n_pages * 5) // 4
    w_buf = min(WINDOW, PAST_LEN)
    perm = jax.random.permutation(next(ks), n_pool)
    page_table = perm[:DEC_BATCH * n_pages].reshape(DEC_BATCH, n_pages).astype(jnp.int32)
    a0 = jax.random.uniform(next(ks), (DEPTH, D_RNN), jnp.float32, 0.9, 0.999)
    rg_lambda = jnp.log(a0) - jnp.log1p(-a0)
    return {
        "x_prompt": nrm((BATCH, SEQ, D_MODEL), 1.0),
        "x_sample": nrm((DEC_BATCH, DEC_SEQ, D_MODEL), 1.0),
        "cache_kv": nrm((DEPTH, n_pool, PAGE_SIZE, N_PAGED, N_KV, HEAD_DIM), 1.0),
        "page_table": page_table,
        "state_win": nrm((DEPTH, DEC_BATCH, w_buf, N_WIN, N_KV, HEAD_DIM), 1.0),
        "state_conv": nrm((DEPTH, DEC_BATCH, CONV_W - 1, D_RNN), 1.0),
        "state_h": nrm((DEPTH, DEC_BATCH, D_RNN), 0.5),
        "rel_bias": nrm((N_BUCKETS, N_HEADS), 0.1),
        "ln_final": gain((D_MODEL,)),
        "ln_ffn1": gain((DEPTH, D_MODEL)),
        "w_ffn1_gate": nrm((DEPTH, D_MODEL, D_FF), D_MODEL ** -0.5),
        "w_ffn1_up": nrm((DEPTH, D_MODEL, D_FF), D_MODEL ** -0.5),
        "w_ffn1_down": nrm((DEPTH, D_FF, D_MODEL), D_FF ** -0.5),
        "ln_mix": gain((DEPTH, D_MODEL)),
        "w_in": nrm((DEPTH, D_MODEL, D_IN), D_MODEL ** -0.5),
        "conv_w": nrm((DEPTH, CONV_W, D_RNN), CONV_W ** -0.5),
        "conv_b": nrm((DEPTH, D_RNN), 0.01),
        "rg_wa": nrm((DEPTH, RNN_BLOCKS, RNN_BW, RNN_BW), RNN_BW ** -0.5),
        "rg_ba": nrm((DEPTH, D_RNN), 0.01),
        "rg_wi": nrm((DEPTH, RNN_BLOCKS, RNN_BW, RNN_BW), RNN_BW ** -0.5),
        "rg_bi": nrm((DEPTH, D_RNN), 0.01),
        "rg_lambda": rg_lambda,
        "cmp_pos": nrm((DEPTH, CMP_LEN, HEAD_DIM), 0.1),
        "cmp_k_w1": nrm((DEPTH, CMP_LEN * HEAD_DIM, CMP_HID), (CMP_LEN * HEAD_DIM) ** -0.5),
        "cmp_k_b1": nrm((DEPTH, CMP_HID), 0.01),
        "cmp_k_w2": nrm((DEPTH, CMP_HID, HEAD_DIM), CMP_HID ** -0.5),
        "cmp_v_w1": nrm((DEPTH, CMP_LEN * HEAD_DIM, CMP_HID), (CMP_LEN * HEAD_DIM) ** -0.5),
        "cmp_v_b1": nrm((DEPTH, CMP_HID), 0.01),
        "cmp_v_w2": nrm((DEPTH, CMP_HID, HEAD_DIM), CMP_HID ** -0.5),
        "w_br_rnn": nrm((DEPTH, D_RNN, D_MODEL), D_RNN ** -0.5),
        "w_br_attn": nrm((DEPTH, N_HEADS * HEAD_DIM, D_MODEL), (N_HEADS * HEAD_DIM) ** -0.5),
        "w_out": nrm((DEPTH, D_MODEL, D_MODEL), D_MODEL ** -0.5),
        "ln_ffn2": gain((DEPTH, D_MODEL)),
        "w_ffn2_gate": nrm((DEPTH, D_MODEL, D_FF), D_MODEL ** -0.5),
        "w_ffn2_up": nrm((DEPTH, D_MODEL, D_FF), D_MODEL ** -0.5),
        "w_ffn2_down": nrm((DEPTH, D_FF, D_MODEL), D_FF ** -0.5),
    }


def reference(x_prompt, x_sample, cache_kv, page_table, state_win, state_conv, state_h, rel_bias, ln_final,
              ln_ffn1, w_ffn1_gate, w_ffn1_up, w_ffn1_down, ln_mix, w_in, conv_w, conv_b, rg_wa, rg_ba, rg_wi,
              rg_bi, rg_lambda, cmp_pos, cmp_k_w1, cmp_k_b1, cmp_k_w2, cmp_v_w1, cmp_v_b1, cmp_v_w2, w_br_rnn,
              w_br_attn, w_out, ln_ffn2, w_ffn2_gate, w_ffn2_up, w_ffn2_down):
    n_dec, n_pages = page_table.shape
    page = cache_kv.shape[2]
    xp, xs = x_prompt, x_sample
    kv_p, kv_s, win_p, win_s, conv_p, conv_s, h_p, h_s = [], [], [], [], [], [], [], []
    for l in range(DEPTH):
        lp = {
            "ln_ffn1": ln_ffn1[l], "w_ffn1_gate": w_ffn1_gate[l], "w_ffn1_up": w_ffn1_up[l],
            "w_ffn1_down": w_ffn1_down[l], "ln_mix": ln_mix[l], "w_in": w_in[l], "conv_w": conv_w[l],
            "conv_b": conv_b[l], "rg_wa": rg_wa[l], "rg_ba": rg_ba[l], "rg_wi": rg_wi[l], "rg_bi": rg_bi[l],
            "rg_lambda": rg_lambda[l], "w_br_rnn": w_br_rnn[l], "w_br_attn": w_br_attn[l], "w_out": w_out[l],
            "ln_ffn2": ln_ffn2[l], "w_ffn2_gate": w_ffn2_gate[l], "w_ffn2_up": w_ffn2_up[l],
            "w_ffn2_down": w_ffn2_down[l],
        }
        cp = (cmp_pos[l], cmp_k_w1[l], cmp_k_b1[l], cmp_k_w2[l], cmp_v_w1[l], cmp_v_b1[l], cmp_v_w2[l])
        zero_conv = jnp.zeros((xp.shape[0], CONV_W - 1, D_RNN), xp.dtype)
        zero_h = jnp.zeros((xp.shape[0], D_RNN), xp.dtype)
        xp, kv_new, win_new, conv_new, h_new = layer_forward(
            xp, zero_conv, zero_h, functools.partial(attend_prompt, cp=cp, rel_bias=rel_bias), lp)
        kv_p.append(kv_new); win_p.append(win_new); conv_p.append(conv_new); h_p.append(h_new)
        past = cache_kv[l, page_table].reshape(n_dec, n_pages * page, N_PAGED, N_KV, HEAD_DIM)
        xs, kv_new, win_new, conv_new, h_new = layer_forward(
            xs, state_conv[l], state_h[l],
            functools.partial(attend_sample, past_kv=past, win_buf=state_win[l], cp=cp, rel_bias=rel_bias), lp)
        kv_s.append(kv_new); win_s.append(win_new); conv_s.append(conv_new); h_s.append(h_new)
    y_prompt = rmsnorm(xp, ln_final)
    y_sample = rmsnorm(xs, ln_final)
    return (y_prompt, y_sample, jnp.stack(kv_p), jnp.stack(kv_s), jnp.stack(win_p), jnp.stack(win_s),
            jnp.stack(conv_p), jnp.stack(conv_s), jnp.stack(h_p), jnp.stack(h_s))
```

```python
import functools
import math

import numpy as np
import jax
import jax.numpy as jnp
from jax import lax
from jax.experimental import pallas as pl
from jax.experimental.pallas import tpu as pltpu

F32 = jnp.float32
BF16 = jnp.bfloat16

D_MODEL = 4096
BATCH = 4
SEQ = 2048
DEC_BATCH = 128
DEC_SEQ = 4
PAST_LEN = 2048
PAGE_SIZE = 128
N_PAGES = PAST_LEN // PAGE_SIZE
D_RNN = D_MODEL // 2
RNN_BLOCKS = 16
RNN_BW = D_RNN // RNN_BLOCKS
CONV_W = 4
LRU_C = 8.0
N_HEADS = 16
HEAD_DIM = 128
N_KV = 4
GROUP = N_HEADS // N_KV
CMP_LEN = 32
CMP_STRIDE = 16
CMP_HID = 2 * HEAD_DIM
SEL_BLK = 64
N_SEL = 8
WINDOW = 512
N_BUCKETS = 32
MAX_EXACT = 16
MAX_DIST = 128
D_FF = ((8 * D_MODEL // 3 + 255) // 256) * 256
EPS = 1e-6
NEG = -1e30
BIG = 1e30
M_FLOOR = -1e29
Q_SCALE = HEAD_DIM ** -0.5

M_PROMPT = BATCH * SEQ
M_SAMPLE = DEC_BATCH * DEC_SEQ
M_TOK = M_PROMPT + M_SAMPLE

C_UGATE = 0
C_UX = C_UGATE + D_RNN
C_Q = C_UX + D_RNN
C_PAG = C_Q + N_HEADS * HEAD_DIM
C_WIN = C_PAG + 4 * N_KV * HEAD_DIM
C_GRNN = C_WIN + 2 * N_KV * HEAD_DIM
C_GATTN = C_GRNN + D_MODEL
C_GNSA = C_GATTN + D_MODEL
D_Z = C_GNSA + N_KV * 128

TM = 512
TF = 256
TN_IN = 1280
TN_MM = 512
VMEM_LIMIT = 56 * 2 ** 20


def _cparams(n_axes, vmem=VMEM_LIMIT):
    return pltpu.CompilerParams(dimension_semantics=("arbitrary",) * n_axes, vmem_limit_bytes=vmem)


def _dot(a, b):
    return jnp.dot(a, b, preferred_element_type=F32)


def _dot_nt(a, b):
    return lax.dot_general(a, b, (((1,), (1,)), ((), ())), preferred_element_type=F32)


def _dot_split3(a, b):
    a1 = a.astype(BF16)
    r1 = a - a1.astype(F32)
    a2 = r1.astype(BF16)
    a3 = (r1 - a2.astype(F32)).astype(BF16)
    return _dot(a1, b) + _dot(a2, b) + _dot(a3, b)


def _rms(x, g):
    return x * lax.rsqrt(jnp.mean(x * x, axis=-1, keepdims=True) + EPS) * g


def _ffn_body(x_ref, ln_ref, wg_ref, wu_ref, wd_ref, lnf_ref, o_ref, xn_ref, *, n_f, final_norm):
    f = pl.program_id(1)

    @pl.when(f == 0)
    def _():
        x = x_ref[...]
        xn_ref[...] = _rms(x, ln_ref[...]).astype(BF16)
        o_ref[...] = 2.0 * x

    xn = xn_ref[...]
    g = _dot(xn, wg_ref[...])
    u = _dot(xn, wu_ref[...])
    h = (g * jax.nn.sigmoid(g) * u).astype(BF16)
    o_ref[...] += _dot(h, wd_ref[...])

    @pl.when(f == n_f - 1)
    def _():
        y = 0.5 * o_ref[...]
        if final_norm:
            y = _rms(y, lnf_ref[...])
        o_ref[...] = y


def _ffn(x, ln, wg, wu, wd, lnf, final_norm):
    m = x.shape[0]
    n_f = D_FF // TF
    return pl.pallas_call(
        functools.partial(_ffn_body, n_f=n_f, final_norm=final_norm),
        grid=(m // TM, n_f),
        in_specs=[
            pl.BlockSpec((TM, D_MODEL), lambda i, f: (i, 0), pipeline_mode=pl.Buffered(1)),
            pl.BlockSpec((1, D_MODEL), lambda i, f: (0, 0)),
            pl.BlockSpec((D_MODEL, TF), lambda i, f: (0, f)),
            pl.BlockSpec((D_MODEL, TF), lambda i, f: (0, f)),
            pl.BlockSpec((TF, D_MODEL), lambda i, f: (f, 0)),
            pl.BlockSpec((1, D_MODEL), lambda i, f: (0, 0)),
        ],
        out_specs=pl.BlockSpec((TM, D_MODEL), lambda i, f: (i, 0)),
        out_shape=jax.ShapeDtypeStruct((m, D_MODEL), F32),
        scratch_shapes=[pltpu.VMEM((TM, D_MODEL), BF16)],
        compiler_params=_cparams(2),
        name="ffn",
    )(x, ln.reshape(1, D_MODEL), wg, wu, wd, lnf.reshape(1, D_MODEL))


def _in_proj_body(x_ref, ln_ref, w_ref, o_ref, xn_ref):
    @pl.when(pl.program_id(1) == 0)
    def _():
        xn_ref[...] = _rms(x_ref[...], ln_ref[...]).astype(BF16)

    o_ref[...] = _dot(xn_ref[...], w_ref[...])


def _in_proj(x, ln, w):
    m = x.shape[0]
    return pl.pallas_call(
        _in_proj_body,
        grid=(m // TM, D_Z // TN_IN),
        in_specs=[
            pl.BlockSpec((TM, D_MODEL), lambda i, j: (i, 0), pipeline_mode=pl.Buffered(1)),
            pl.BlockSpec((1, D_MODEL), lambda i, j: (0, 0)),
            pl.BlockSpec((D_MODEL, TN_IN), lambda i, j: (0, j)),
        ],
        out_specs=pl.BlockSpec((TM, TN_IN), lambda i, j: (i, j)),
        out_shape=jax.ShapeDtypeStruct((m, D_Z), F32),
        scratch_shapes=[pltpu.VMEM((TM, D_MODEL), BF16)],
        compiler_params=_cparams(2),
        name="in_proj",
    )(x, ln.reshape(1, D_MODEL), w)


def _merge_body(gr_ref, oa_ref, wr_ref, wa_ref, zr_ref, za_ref, o_ref):
    y_rnn = _dot(gr_ref[...], wr_ref[...])
    y_attn = _dot(oa_ref[...], wa_ref[...])
    o_ref[...] = (jax.nn.sigmoid(zr_ref[...]) * y_rnn + jax.nn.sigmoid(za_ref[...]) * y_attn).astype(BF16)


def _merge(z, grnn, oattn, w_rnn, w_attn):
    m = z.shape[0]
    cr, ca = C_GRNN // TN_MM, C_GATTN // TN_MM
    return pl.pallas_call(
        _merge_body,
        grid=(m // TM, D_MODEL // TN_MM),
        in_specs=[
            pl.BlockSpec((TM, D_RNN), lambda i, j: (i, 0)),
            pl.BlockSpec((TM, N_HEADS * HEAD_DIM), lambda i, j: (i, 0)),
            pl.BlockSpec((D_RNN, TN_MM), lambda i, j: (0, j)),
            pl.BlockSpec((N_HEADS * HEAD_DIM, TN_MM), lambda i, j: (0, j)),
            pl.BlockSpec((TM, TN_MM), lambda i, j: (i, cr + j)),
            pl.BlockSpec((TM, TN_MM), lambda i, j: (i, ca + j)),
        ],
        out_specs=pl.BlockSpec((TM, TN_MM), lambda i, j: (i, j)),
        out_shape=jax.ShapeDtypeStruct((m, D_MODEL), BF16),
        compiler_params=_cparams(2),
        name="merge",
    )(grnn, oattn, w_rnn, w_attn, z, z)


def _out_proj_body(a_ref, w_ref, x_ref, o_ref):
    o_ref[...] = x_ref[...] + _dot(a_ref[...], w_ref[...])


def _out_proj(x, a, w):
    m = x.shape[0]
    return pl.pallas_call(
        _out_proj_body,
        grid=(m // TM, D_MODEL // TN_MM),
        in_specs=[
            pl.BlockSpec((TM, D_MODEL), lambda i, j: (i, 0)),
            pl.BlockSpec((D_MODEL, TN_MM), lambda i, j: (0, j)),
            pl.BlockSpec((TM, TN_MM), lambda i, j: (i, j)),
        ],
        out_specs=pl.BlockSpec((TM, TN_MM), lambda i, j: (i, j)),
        out_shape=jax.ShapeDtypeStruct((m, D_MODEL), F32),
        compiler_params=_cparams(2),
        name="out_proj",
    )(a, w, x)


def _softplus(v):
    return jnp.maximum(v, 0.0) + jnp.log1p(jnp.exp(-jnp.abs(v)))


def _lru_coeffs(xc, wa_ref, ba, wi_ref, bi, sp, n_blk):
    xb = xc.astype(BF16)
    ra = jnp.concatenate([_dot(xb[:, b * RNN_BW:(b + 1) * RNN_BW], wa_ref[b]) for b in range(n_blk)], axis=1)
    ia = jnp.concatenate([_dot(xb[:, b * RNN_BW:(b + 1) * RNN_BW], wi_ref[b]) for b in range(n_blk)], axis=1)
    r = jax.nn.sigmoid(ra + ba)
    i = jax.nn.sigmoid(ia + bi)
    log_a = -LRU_C * r * sp
    a = jnp.exp(log_a)
    bt = jnp.sqrt(-jnp.tanh(log_a) * (a * a + 1.0)) * (i * xc)
    return a, bt


RNN_TC = 256


def _rnn_prompt_body(ug_ref, ux_ref, cw_ref, cb_ref, wa_ref, ba_ref, wi_ref, bi_ref, lam_ref,
                     g_ref, h_ref, tail_ref, hc_ref):
    c = pl.program_id(1)
    tc = RNN_TC

    @pl.when(c == 0)
    def _():
        tail_ref[...] = jnp.zeros_like(tail_ref)
        hc_ref[...] = jnp.zeros_like(hc_ref)

    u = ux_ref[...]
    tail = tail_ref[...]
    row8 = lax.broadcasted_iota(jnp.int32, (8, D_RNN), 0)
    xc = cb_ref[...] + cw_ref[CONV_W - 1:CONV_W, :] * u
    for j in range(1, CONV_W):
        r = pltpu.roll(u, j, axis=0)
        first = jnp.where(row8 >= j, r[0:8], pltpu.roll(tail, j, axis=0))
        shifted = jnp.concatenate([first, r[8:]], axis=0)
        xc = xc + cw_ref[CONV_W - 1 - j:CONV_W - j, :] * shifted
    tail_ref[...] = u[tc - 8:tc]

    a, bt = _lru_coeffs(xc, wa_ref, ba_ref[...], wi_ref, bi_ref[...], _softplus(-lam_ref[...]), RNN_BLOCKS)

    row = lax.broadcasted_iota(jnp.int32, (tc, D_RNN), 0)
    s = 1
    while s < tc:
        keep = row >= s
        a_sh = jnp.where(keep, pltpu.roll(a, s, axis=0), 1.0)
        b_sh = jnp.where(keep, pltpu.roll(bt, s, axis=0), 0.0)
        bt = a * b_sh + bt
        a = a * a_sh
        s *= 2
    h = bt + a * hc_ref[7:8, :]
    hc_ref[...] = h[tc - 8:tc]
    g_ref[...] = (h * jax.nn.gelu(ug_ref[...])).astype(BF16)

    @pl.when(c == pl.num_programs(1) - 1)
    def _():
        h_ref[0] = h[tc - 8:tc]


def _rnn_prompt(z, cw, cb, wa, ba, wi, bi, lam):
    nc = SEQ // RNN_TC
    vec = lambda: pl.BlockSpec((1, D_RNN), lambda n, c: (0, 0))
    blk = lambda: pl.BlockSpec((RNN_BLOCKS, RNN_BW, RNN_BW), lambda n, c: (0, 0, 0))
    return pl.pallas_call(
        _rnn_prompt_body,
        grid=(BATCH, nc),
        in_specs=[
            pl.BlockSpec((RNN_TC, D_RNN), lambda n, c: (n * nc + c, C_UGATE // D_RNN)),
            pl.BlockSpec((RNN_TC, D_RNN), lambda n, c: (n * nc + c, C_UX // D_RNN)),
            pl.BlockSpec((CONV_W, D_RNN), lambda n, c: (0, 0)),
            vec(), blk(), vec(), blk(), vec(), vec(),
        ],
        out_specs=[
            pl.BlockSpec((RNN_TC, D_RNN), lambda n, c: (n * nc + c, 0)),
            pl.BlockSpec((1, 8, D_RNN), lambda n, c: (n, 0, 0)),
        ],
        out_shape=[jax.ShapeDtypeStruct((M_PROMPT, D_RNN), BF16),
                   jax.ShapeDtypeStruct((BATCH, 8, D_RNN), F32)],
        scratch_shapes=[pltpu.VMEM((8, D_RNN), F32), pltpu.VMEM((8, D_RNN), F32)],
        compiler_params=_cparams(2),
        name="rnn_prompt",
    )(z, z, cw, cb, wa, ba, wi, bi, lam)


RNN_SC = 512


def _rnn_sample_body(ug_ref, ux_ref, buf_ref, h0_ref, cw_ref, cb_ref, wa_ref, ba_ref, wi_ref, bi_ref, lam_ref,
                     g_ref, h_ref):
    full = [buf_ref[j] for j in range(CONV_W - 1)] + [ux_ref[t] for t in range(DEC_SEQ)]
    sp = _softplus(-lam_ref[...])
    h = h0_ref[...]
    for t in range(DEC_SEQ):
        xc = cb_ref[...]
        for k in range(CONV_W):
            xc = xc + full[t + k] * cw_ref[k:k + 1, :]
        a, bt = _lru_coeffs(xc, wa_ref, ba_ref[...], wi_ref, bi_ref[...], sp, RNN_SC // RNN_BW)
        h = a * h + bt
        g_ref[t] = (h * jax.nn.gelu(ug_ref[t])).astype(BF16)
    h_ref[...] = h


def _rnn_sample(ug_t, ux_t, buf_t, h0, cw, cb, wa, ba, wi, bi, lam):
    nb = RNN_SC // RNN_BW
    vec = lambda: pl.BlockSpec((1, RNN_SC), lambda c: (0, c))
    blk = lambda: pl.BlockSpec((nb, RNN_BW, RNN_BW), lambda c: (c, 0, 0))
    return pl.pallas_call(
        _rnn_sample_body,
        grid=(D_RNN // RNN_SC,),
        in_specs=[
            pl.BlockSpec((DEC_SEQ, DEC_BATCH, RNN_SC), lambda c: (0, 0, c)),
            pl.BlockSpec((DEC_SEQ, DEC_BATCH, RNN_SC), lambda c: (0, 0, c)),
            pl.BlockSpec((CONV_W - 1, DEC_BATCH, RNN_SC), lambda c: (0, 0, c)),
            pl.BlockSpec((DEC_BATCH, RNN_SC), lambda c: (0, c)),
            pl.BlockSpec((CONV_W, RNN_SC), lambda c: (0, c)),
            vec(), blk(), vec(), blk(), vec(), vec(),
        ],
        out_specs=[
            pl.BlockSpec((DEC_SEQ, DEC_BATCH, RNN_SC), lambda c: (0, 0, c)),
            pl.BlockSpec((DEC_BATCH, RNN_SC), lambda c: (0, c)),
        ],
        out_shape=[jax.ShapeDtypeStruct((DEC_SEQ, DEC_BATCH, D_RNN), BF16),
                   jax.ShapeDtypeStruct((DEC_BATCH, D_RNN), F32)],
        compiler_params=_cparams(1),
        name="rnn_sample",
    )(ug_t, ux_t, buf_t, h0, cw, cb, wa, ba, wi, bi, lam)


N_SEG = PAGE_SIZE // CMP_STRIDE
SEG_ROWS = N_PAGES * N_SEG
N_PAIR = CMP_STRIDE // 2


def _compress_body(pt_ref, *refs):
    page_refs = refs[:2 * N_KV]
    w1_ref, pos_ref, b1_ref, w2_ref, ck_ref, cv_ref, stage_ref, pterm_ref = refs[2 * N_KV:]
    n = pl.program_id(0)
    p = pl.program_id(1)

    @pl.when((n == 0) & (p == 0))
    def _():
        for kind in range(2):
            acc = jnp.zeros((8, 2 * CMP_HID), F32)
            for pr in range(N_PAIR):
                acc = acc + _dot(pos_ref[:, pr * 256:(pr + 1) * 256].astype(BF16), w1_ref[kind, pr])
            pterm_ref[kind] = acc

    for kind in range(2):
        for l in range(CMP_STRIDE):
            for k in range(N_KV):
                piece = page_refs[kind * N_KV + k][pl.ds(l, N_SEG, stride=CMP_STRIDE), :]
                stage_ref[kind, l // 2, pl.ds(k * SEG_ROWS + p * N_SEG, N_SEG),
                          pl.ds((l % 2) * HEAD_DIM, HEAD_DIM)] = piece

    @pl.when(p == N_PAGES - 1)
    def _():
        for kind, out_ref in ((0, ck_ref), (1, cv_ref)):
            acc = jnp.zeros((N_KV * SEG_ROWS, 2 * CMP_HID), F32)
            for pr in range(N_PAIR):
                acc = acc + _dot(stage_ref[kind, pr].astype(BF16), w1_ref[kind, pr])
            nxt = pltpu.roll(acc[:, CMP_HID:], N_KV * SEG_ROWS - 1, axis=0)
            pt = pterm_ref[kind]
            posterm = pt[0:1, :CMP_HID] + pt[1:2, CMP_HID:] + b1_ref[kind]
            hid = acc[:, :CMP_HID] + nxt + posterm
            out_ref[0] = _dot(jax.nn.gelu(hid).astype(BF16), w2_ref[kind]).astype(BF16)


def _compress(src3, pt_flat, col_blk, n_seq, w1, pos, b1, w2, name):
    page_spec = lambda j: pl.BlockSpec((None, PAGE_SIZE, HEAD_DIM),
                                       lambda n, p, pt: (pt[n * N_PAGES + p], 0, col_blk + j))
    grid_spec = pltpu.PrefetchScalarGridSpec(
        num_scalar_prefetch=1,
        grid=(n_seq, N_PAGES),
        in_specs=[page_spec(j) for j in range(2 * N_KV)] + [
            pl.BlockSpec((2, N_PAIR, 256, 2 * CMP_HID), lambda n, p, pt: (0, 0, 0, 0)),
            pl.BlockSpec((8, CMP_STRIDE * HEAD_DIM), lambda n, p, pt: (0, 0)),
            pl.BlockSpec((2, 1, CMP_HID), lambda n, p, pt: (0, 0, 0)),
            pl.BlockSpec((2, CMP_HID, HEAD_DIM), lambda n, p, pt: (0, 0, 0)),
        ],
        out_specs=[
            pl.BlockSpec((1, N_KV * SEG_ROWS, HEAD_DIM), lambda n, p, pt: (n, 0, 0)),
            pl.BlockSpec((1, N_KV * SEG_ROWS, HEAD_DIM), lambda n, p, pt: (n, 0, 0)),
        ],
        scratch_shapes=[pltpu.VMEM((2, N_PAIR, N_KV * SEG_ROWS, 256), F32),
                        pltpu.VMEM((2, 8, 2 * CMP_HID), F32)],
    )
    shp = jax.ShapeDtypeStruct((n_seq, N_KV * SEG_ROWS, HEAD_DIM), BF16)
    return pl.pallas_call(
        _compress_body,
        grid_spec=grid_spec,
        out_shape=[shp, shp],
        compiler_params=_cparams(2),
        name=name,
    )(pt_flat, *([src3] * (2 * N_KV)), w1, pos, b1, w2)


def _select_blocks(score, cur, n_blk):
    jj = lax.broadcasted_iota(jnp.int32, score.shape, 1)
    forced = (jj == 0) | (jj == cur) | (jj == cur - 1)
    sc = jnp.where(forced, BIG, jnp.where(jj <= cur, score, NEG))
    rank = jnp.zeros(score.shape, F32)
    for i in range(n_blk):
        si = sc[:, i:i + 1]
        beats = (si > sc) | ((si == sc) & (jj > i))
        rank = rank + jnp.where(beats, 1.0, 0.0)
    sel = (rank < float(min(N_SEL, n_blk))) & (jj <= cur) & (jj < n_blk)
    return jnp.where(sel, 1.0, 0.0).astype(BF16)


def _softmax_rows(logits):
    m = jnp.maximum(jnp.max(logits, axis=-1, keepdims=True), M_FLOOR)
    e = jnp.exp(logits - m)
    s = jnp.sum(e, axis=-1, keepdims=True)
    return e / jnp.where(s > 0.0, s, 1.0)


def _attn_prompt_body(zq_ref, zg_ref, ck_ref, cv_ref, ks_ref, vs_ref, kw_ref, vw_ref,
                      bc_ref, tz_ref, ov_ref, e_ref, o_ref,
                      ksb, vsb, kwb, vwb, selm_ref):
    qt = pl.program_id(2)
    rows = GROUP * 128

    @pl.when(qt == 0)
    def _():
        ksb[...] = ks_ref[...].astype(BF16)
        vsb[...] = vs_ref[...].astype(BF16)
        kwb[...] = kw_ref[...].astype(BF16)
        vwb[...] = vw_ref[...].astype(BF16)

    q = zq_ref[...] * Q_SCALE
    qq = jnp.concatenate([q[:, g * HEAD_DIM:(g + 1) * HEAD_DIM] for g in range(GROUP)], axis=0).astype(BF16)

    pc = _softmax_rows(_dot_nt(qq, ck_ref[0]) + bc_ref[0].reshape(rows, 128))
    o_c = _dot(pc.astype(BF16), cv_ref[0])

    ps = pc[0:128] + pc[128:256] + pc[256:384] + pc[384:512]
    score = _dot_split3(ps, ov_ref[...])
    tt = lax.broadcasted_iota(jnp.int32, (128, 128), 0)
    cur = jnp.right_shift(qt * 128 + tt, 6)
    sel = _select_blocks(score, cur, SEQ // SEL_BLK)
    for kt in range(SEQ // 128):
        selm_ref[kt] = (_dot(sel, e_ref[kt]) - 1.0) * BIG

    def flash(k_ref, v_ref, lo, use_sel, clamp):
        def body(kt, carry):
            m, l, acc = carry
            off = pl.multiple_of(kt * 128, 128)
            s = _dot_nt(qq, k_ref[pl.ds(off, 128), :]).reshape(GROUP, 128, 128)
            s = s + tz_ref[jnp.minimum(qt - kt, clamp)]
            if use_sel:
                s = s + selm_ref[kt][None]
            s = s.reshape(rows, 128)
            m_new = jnp.maximum(m, jnp.max(s, axis=-1, keepdims=True))
            alpha = jnp.exp(m - m_new)
            pe = jnp.exp(s - m_new)
            l = alpha * l + jnp.sum(pe, axis=-1, keepdims=True)
            acc = alpha * acc + _dot(pe.astype(BF16), v_ref[pl.ds(off, 128), :])
            return m_new, l, acc

        init = (jnp.full((rows, 1), M_FLOOR, F32), jnp.zeros((rows, 1), F32), jnp.zeros((rows, HEAD_DIM), F32))
        _, l, acc = lax.fori_loop(lo, qt + 1, body, init)
        return acc / jnp.where(l > 0.0, l, 1.0)

    o_s = flash(ksb, vsb, 0, True, 2)
    o_w = flash(kwb, vwb, jnp.maximum(qt - WINDOW // 128, 0), False, WINDOW // 128)

    gates = jax.nn.sigmoid(zg_ref[...])
    outs = []
    for g in range(GROUP):
        r = slice(g * 128, (g + 1) * 128)
        outs.append(gates[:, g:g + 1] * o_c[r] + gates[:, GROUP + g:GROUP + g + 1] * o_s[r]
                    + gates[:, 2 * GROUP + g:2 * GROUP + g + 1] * o_w[r])
    o_ref[...] = jnp.concatenate(outs, axis=1).astype(BF16)


def _attn_prompt(z, ck, cv, bias_cmp, tz, ov, emat):
    nq = SEQ // 128
    kv_col = lambda base, kind: (lambda n, k, t: (n, (base + kind * N_KV * HEAD_DIM) // HEAD_DIM + k))
    kvspec = lambda base, kind: pl.BlockSpec((SEQ, HEAD_DIM), kv_col(base, kind))
    return pl.pallas_call(
        _attn_prompt_body,
        grid=(BATCH, N_KV, nq),
        in_specs=[
            pl.BlockSpec((128, GROUP * HEAD_DIM), lambda n, k, t: (n * nq + t, C_Q // (GROUP * HEAD_DIM) + k)),
            pl.BlockSpec((128, 128), lambda n, k, t: (n * nq + t, C_GNSA // 128 + k)),
            pl.BlockSpec((1, SEG_ROWS, HEAD_DIM), lambda n, k, t: (n * N_KV + k, 0, 0)),
            pl.BlockSpec((1, SEG_ROWS, HEAD_DIM), lambda n, k, t: (n * N_KV + k, 0, 0)),
            kvspec(C_PAG, 2), kvspec(C_PAG, 3), kvspec(C_WIN, 0), kvspec(C_WIN, 1),
            pl.BlockSpec((1, GROUP, 128, 128), lambda n, k, t: (t, k, 0, 0)),
            pl.BlockSpec((WINDOW // 128 + 1, GROUP, 128, 128), lambda n, k, t: (0, k, 0, 0)),
            pl.BlockSpec((128, 128), lambda n, k, t: (0, 0)),
            pl.BlockSpec((SEQ // 128, 128, 128), lambda n, k, t: (0, 0, 0)),
        ],
        out_specs=pl.BlockSpec((128, GROUP * HEAD_DIM), lambda n, k, t: (n * nq + t, k)),
        out_shape=jax.ShapeDtypeStruct((M_PROMPT, N_HEADS * HEAD_DIM), BF16),
        scratch_shapes=[pltpu.VMEM((SEQ, HEAD_DIM), BF16)] * 4 + [pltpu.VMEM((SEQ // 128, 128, 128), F32)],
        compiler_params=_cparams(3),
        name="attn_prompt",
    )(z, z, ck, cv, z, z, z, z, bias_cmp, tz, ov, emat)


S_ROWS = GROUP * N_KV * DEC_SEQ


def _attn_sample_body(pt_ref, q_ref, gs_ref, ck_ref, cv_ref, page_ref, nkv_ref, win_ref, nwin_ref,
                      bcmp_ref, bsel_ref, bnew_ref, bwin_ref, ov_ref, e_ref, o_ref,
                      m_ref, l_ref, acc_ref, selm_ref, oc_ref, ow_ref, nk_ref, nw_ref):
    n = pl.program_id(0)
    p = pl.program_id(1)
    kv_of_row = jnp.bitwise_and(jnp.right_shift(lax.broadcasted_iota(jnp.int32, (S_ROWS, 1), 0), 2), N_KV - 1)
    qq = (q_ref[0] * Q_SCALE).astype(BF16)

    def logits(get_k):
        out = None
        for k in range(N_KV):
            s = jnp.where(kv_of_row == k, _dot_nt(qq, get_k(k).astype(BF16)), 0.0)
            out = s if out is None else out + s
        return out

    def weighted(pe, get_v):
        out = None
        for k in range(N_KV):
            o = _dot(jnp.where(kv_of_row == k, pe, 0.0).astype(BF16), get_v(k).astype(BF16))
            out = o if out is None else out + o
        return out

    col = lambda k, half: pl.ds(half * N_KV * HEAD_DIM + k * HEAD_DIM, HEAD_DIM)

    @pl.when((n == 0) & (p == 0))
    def _():
        nk_ref[...] = jnp.zeros_like(nk_ref)
        nw_ref[...] = jnp.zeros_like(nw_ref)

    @pl.when(p == 0)
    def _():
        nk_ref[0:8, :] = nkv_ref[0]
        nw_ref[0:8, :] = nwin_ref[0]

        pc = _softmax_rows(logits(lambda k: ck_ref[0, pl.ds(k * SEG_ROWS, SEG_ROWS), :]) + bcmp_ref[...])
        oc_ref[...] = weighted(pc, lambda k: cv_ref[0, pl.ds(k * SEG_ROWS, SEG_ROWS), :])
        ps = pc + pltpu.roll(pc, 16, axis=0) + pltpu.roll(pc, 32, axis=0) + pltpu.roll(pc, 48, axis=0)
        score = _dot_split3(ps, ov_ref[...])
        n_blk = -(-(PAST_LEN + DEC_SEQ) // SEL_BLK)
        cur = jnp.full((S_ROWS, 128), PAST_LEN // SEL_BLK, jnp.int32)
        sel = _select_blocks(score, cur, n_blk)
        big = (_dot(sel, e_ref[...]) - 1.0) * BIG
        for t in range(N_PAGES):
            selm_ref[t] = big[:, t * 128:(t + 1) * 128]

        sw = logits(lambda k: win_ref[0, :, col(k, 0)]) + bwin_ref[...]
        sn = logits(lambda k: nw_ref[:, col(k, 0)]) + bnew_ref[...]
        m = jnp.maximum(jnp.maximum(jnp.max(sw, axis=-1, keepdims=True), jnp.max(sn, axis=-1, keepdims=True)),
                        M_FLOOR)
        pw = jnp.exp(sw - m)
        pn = jnp.exp(sn - m)
        l = jnp.sum(pw, axis=-1, keepdims=True) + jnp.sum(pn, axis=-1, keepdims=True)
        ow = weighted(pw, lambda k: win_ref[0, :, col(k, 1)]) + weighted(pn, lambda k: nw_ref[:, col(k, 1)])
        ow_ref[...] = ow / jnp.where(l > 0.0, l, 1.0)

        m_ref[...] = jnp.full(m_ref.shape, M_FLOOR, F32)
        l_ref[...] = jnp.zeros_like(l_ref)
        acc_ref[...] = jnp.zeros_like(acc_ref)

    def update(s, get_v):
        m = m_ref[...]
        m_new = jnp.maximum(m, jnp.max(s, axis=-1, keepdims=True))
        alpha = jnp.exp(m - m_new)
        pe = jnp.exp(s - m_new)
        l_ref[...] = alpha * l_ref[...] + jnp.sum(pe, axis=-1, keepdims=True)
        acc_ref[...] = alpha * acc_ref[...] + weighted(pe, get_v)
        m_ref[...] = m_new

    update(logits(lambda k: page_ref[:, col(k, 0)]) + bsel_ref[p] + selm_ref[p],
           lambda k: page_ref[:, col(k, 1)])

    @pl.when(p == N_PAGES - 1)
    def _():
        update(logits(lambda k: nk_ref[:, col(k, 0)]) + bnew_ref[...], lambda k: nk_ref[:, col(k, 1)])
        l = l_ref[...]
        o_s = acc_ref[...] / jnp.where(l > 0.0, l, 1.0)
        gates = jax.nn.sigmoid(gs_ref[0])
        o_ref[0] = gates[:, 0:1] * oc_ref[...] + gates[:, 1:2] * o_s + gates[:, 2:3] * ow_ref[...]


def _attn_sample(pt_flat, q_s, g_s, ck, cv, cache3, nkv, state_win3, nwin, bcmp, bsel, bnew, bwin, ov, emat):
    const2 = lambda shape: pl.BlockSpec(shape, lambda n, p, pt: (0, 0))
    grid_spec = pltpu.PrefetchScalarGridSpec(
        num_scalar_prefetch=1,
        grid=(DEC_BATCH, N_PAGES),
        in_specs=[
            pl.BlockSpec((1, S_ROWS, HEAD_DIM), lambda n, p, pt: (n, 0, 0)),
            pl.BlockSpec((1, S_ROWS, 128), lambda n, p, pt: (n, 0, 0)),
            pl.BlockSpec((1, N_KV * SEG_ROWS, HEAD_DIM), lambda n, p, pt: (n, 0, 0)),
            pl.BlockSpec((1, N_KV * SEG_ROWS, HEAD_DIM), lambda n, p, pt: (n, 0, 0)),
            pl.BlockSpec((None, PAGE_SIZE, 1024), lambda n, p, pt: (pt[n * N_PAGES + p], 0, 1)),
            pl.BlockSpec((1, 8, 1024), lambda n, p, pt: (n, 0, 0)),
            pl.BlockSpec((1, WINDOW, 1024), lambda n, p, pt: (n, 0, 0)),
            pl.BlockSpec((1, 8, 1024), lambda n, p, pt: (n, 0, 0)),
            const2((S_ROWS, 128)),
            pl.BlockSpec((N_PAGES, S_ROWS, 128), lambda n, p, pt: (0, 0, 0)),
            const2((S_ROWS, 128)),
            const2((S_ROWS, WINDOW)),
            const2((128, 128)),
            const2((128, PAST_LEN)),
        ],
        out_specs=pl.BlockSpec((1, S_ROWS, HEAD_DIM), lambda n, p, pt: (n, 0, 0)),
        scratch_shapes=[
            pltpu.VMEM((S_ROWS, 1), F32), pltpu.VMEM((S_ROWS, 1), F32), pltpu.VMEM((S_ROWS, HEAD_DIM), F32),
            pltpu.VMEM((N_PAGES, S_ROWS, 128), F32),
            pltpu.VMEM((S_ROWS, HEAD_DIM), F32), pltpu.VMEM((S_ROWS, HEAD_DIM), F32),
            pltpu.VMEM((128, 1024), F32), pltpu.VMEM((128, 1024), F32),
        ],
    )
    return pl.pallas_call(
        _attn_sample_body,
        grid_spec=grid_spec,
        out_shape=jax.ShapeDtypeStruct((DEC_BATCH, S_ROWS, HEAD_DIM), F32),
        compiler_params=_cparams(2),
        name="attn_sample",
    )(pt_flat, q_s, g_s, ck, cv, cache3, nkv, state_win3, nwin, bcmp, bsel, bnew, bwin, ov, emat)


def _t5_bucket(dist):
    d = jnp.maximum(dist, 0)
    df = jnp.maximum(d, 1).astype(F32)
    large = MAX_EXACT + (jnp.log(df / MAX_EXACT) / math.log(MAX_DIST / MAX_EXACT)
                         * (N_BUCKETS - MAX_EXACT)).astype(jnp.int32)
    large = jnp.minimum(large, N_BUCKETS - 1)
    return jnp.where(d < MAX_EXACT, d, large)


def _bias_table(rel_bias, dist, valid, head):
    d_max = PAST_LEN + SEQ
    by_dist = rel_bias.astype(F32)[_t5_bucket(jnp.arange(d_max))]
    vals = by_dist[np.clip(dist, 0, d_max - 1), head]
    return jnp.where(valid, vals, NEG)


def _overlap(nc, nb):
    cs = np.arange(nc)[:, None] * CMP_STRIDE
    js = np.arange(nb)[None, :] * SEL_BLK
    ov = np.clip(np.minimum(cs + CMP_LEN, js + SEL_BLK) - np.maximum(cs, js), 0, None) / CMP_LEN
    out = np.zeros((128, 128), np.float32)
    out[:nc, :nb] = ov
    return jnp.asarray(out, BF16)


def _position_tables(rel_bias):
    nc = SEG_ROWS - 1
    t = np.arange(128)[:, None]
    c = np.arange(128)[None, :]
    h = np.arange(N_HEADS)[:, None, None]
    deltas = []
    for delta in range(WINDOW // 128 + 1):
        d = delta * 128 + t - c
        valid = (d >= 0) & (d < WINDOW)
        deltas.append(_bias_table(rel_bias, d[None], valid[None], h))
    tz = jnp.stack(deltas)
    qt = np.arange(SEQ // 128)[:, None, None, None]
    d = qt * 128 + t[None, None] - (c[None, None] * CMP_STRIDE + CMP_LEN - 1)
    bias_cmp = _bias_table(rel_bias, d, (d >= 0) & (c[None, None] < nc), h[None])
    r = np.arange(S_ROWS)
    rg, rk, rt = r // (N_KV * DEC_SEQ), (r // DEC_SEQ) % N_KV, r % DEC_SEQ
    rh = (rk * GROUP + rg)[:, None]
    qpos = (PAST_LEN + rt)[:, None]
    d = qpos - (c * CMP_STRIDE + CMP_LEN - 1)
    bcmp = _bias_table(rel_bias, d, (d >= 0) & (c < nc), rh)
    pg = np.arange(N_PAGES)[:, None, None]
    d = qpos[None] - (pg * PAGE_SIZE + c[None])
    bsel = _bias_table(rel_bias, d, d >= 0, rh[None])
    d = rt[:, None] - c
    bnew = _bias_table(rel_bias, d, (d >= 0) & (c < DEC_SEQ), rh)
    cw = np.arange(WINDOW)[None, :]
    d = qpos - (PAST_LEN - WINDOW + cw)
    bwin = _bias_table(rel_bias, d, (d >= 0) & (d < WINDOW), rh)
    keys = np.arange(SEQ)
    e_all = (np.arange(128)[:, None] == (keys // SEL_BLK)[None, :]).astype(np.float32)
    e_tiles = jnp.asarray(e_all.reshape(128, SEQ // 128, 128).transpose(1, 0, 2), BF16)
    return dict(tz=tz, bias_cmp=bias_cmp, bcmp=bcmp, bsel=bsel, bnew=bnew, bwin=bwin,
                ov_p=_overlap(nc, SEQ // SEL_BLK), ov_s=_overlap(nc, -(-(PAST_LEN + DEC_SEQ) // SEL_BLK)),
                e_tiles=e_tiles, e_all=jnp.asarray(e_all, BF16))


def _permute_w_in(w_in):
    c0 = 2 * D_RNN + N_HEADS * HEAD_DIM + 6 * N_KV * HEAD_DIM
    g_nsa = w_in[:, c0:c0 + 3 * N_HEADS].reshape(D_MODEL, N_KV, GROUP, 3)
    g_nsa = jnp.transpose(g_nsa, (0, 1, 3, 2)).reshape(D_MODEL, N_KV, 3 * GROUP)
    g_nsa = jnp.pad(g_nsa, ((0, 0), (0, 0), (0, 128 - 3 * GROUP))).reshape(D_MODEL, N_KV * 128)
    return jnp.concatenate([w_in[:, :c0], w_in[:, c0 + 3 * N_HEADS:], g_nsa], axis=1).astype(BF16)


def _cmp_weights(w1_k, w1_v, b1_k, b1_v, w2_k, w2_v, pos):
    def cat(w1):
        w = w1.reshape(2, CMP_STRIDE * HEAD_DIM, CMP_HID)
        return jnp.concatenate([w[0], w[1]], axis=1).reshape(N_PAIR, 256, 2 * CMP_HID)
    w1 = jnp.stack([cat(w1_k), cat(w1_v)]).astype(BF16)
    posm = jnp.pad(pos.reshape(2, CMP_STRIDE * HEAD_DIM), ((0, 6), (0, 0)))
    b1 = jnp.stack([b1_k, b1_v]).reshape(2, 1, CMP_HID)
    w2 = jnp.stack([w2_k, w2_v]).astype(BF16)
    return w1, posm, b1, w2


def kernel(x_prompt, x_sample, cache_kv, page_table, state_win, state_conv, state_h, rel_bias, ln_final, ln_ffn1, w_ffn1_gate, w_ffn1_up, w_ffn1_down, ln_mix, w_in, conv_w, conv_b, rg_wa, rg_ba, rg_wi, rg_bi, rg_lambda, cmp_pos, cmp_k_w1, cmp_k_b1, cmp_k_w2, cmp_v_w1, cmp_v_b1, cmp_v_w2, w_br_rnn, w_br_attn, w_out, ln_ffn2, w_ffn2_gate, w_ffn2_up, w_ffn2_down):
    tabs = _position_tables(rel_bias)
    x = jnp.concatenate([x_prompt.reshape(M_PROMPT, D_MODEL), x_sample.reshape(M_SAMPLE, D_MODEL)], axis=0)

    x = _ffn(x, ln_ffn1[0], w_ffn1_gate[0].astype(BF16), w_ffn1_up[0].astype(BF16), w_ffn1_down[0].astype(BF16),
             ln_final, False)
    z = _in_proj(x, ln_mix[0], _permute_w_in(w_in[0]))
    z_s = z[M_PROMPT:]

    vec = lambda v: v.reshape(1, D_RNN)
    rnn_w = (conv_w[0], vec(conv_b[0]), rg_wa[0].astype(BF16), vec(rg_ba[0]), rg_wi[0].astype(BF16),
             vec(rg_bi[0]), vec(rg_lambda[0]))
    g_p, h_p = _rnn_prompt(z, *rnn_w)
    tmajor = lambda a: jnp.transpose(a.reshape(DEC_BATCH, -1, D_RNN), (1, 0, 2))
    g_s, h_s = _rnn_sample(tmajor(z_s[:, C_UGATE:C_UGATE + D_RNN]), tmajor(z_s[:, C_UX:C_UX + D_RNN]),
                           tmajor(state_conv[0]), state_h[0], *rnn_w)
    grnn = jnp.concatenate([g_p, jnp.transpose(g_s, (1, 0, 2)).reshape(M_SAMPLE, D_RNN)], axis=0)

    cw = _cmp_weights(cmp_k_w1[0], cmp_v_w1[0], cmp_k_b1[0], cmp_v_b1[0], cmp_k_w2[0], cmp_v_w2[0], cmp_pos[0])
    pt_prompt = jnp.arange(BATCH * N_PAGES, dtype=jnp.int32)
    pt_sample = page_table.reshape(-1).astype(jnp.int32)
    cache3 = cache_kv[0].reshape(-1, PAGE_SIZE, 4 * N_KV * HEAD_DIM)
    ck_p, cv_p = _compress(z.reshape(M_TOK // PAGE_SIZE, PAGE_SIZE, D_Z), pt_prompt, C_PAG // HEAD_DIM, BATCH, *cw,
                           name="compress_prompt")
    ck_s, cv_s = _compress(cache3, pt_sample, 0, DEC_BATCH, *cw, name="compress_sample")
    o_p = _attn_prompt(z, ck_p.reshape(BATCH * N_KV, SEG_ROWS, HEAD_DIM), cv_p.reshape(BATCH * N_KV, SEG_ROWS, HEAD_DIM),
                       tabs["bias_cmp"], tabs["tz"], tabs["ov_p"], tabs["e_tiles"])

    def rows_gkt(a, width):
        a = a.reshape(DEC_BATCH, DEC_SEQ, N_KV, GROUP, width)
        return jnp.transpose(a, (0, 3, 2, 1, 4)).reshape(DEC_BATCH, S_ROWS, width)

    q_s = rows_gkt(z_s[:, C_Q:C_Q + N_HEADS * HEAD_DIM], HEAD_DIM)
    gn = z_s[:, C_GNSA:].reshape(M_SAMPLE, N_KV, 128)[:, :, :3 * GROUP].reshape(M_SAMPLE, N_KV, 3, GROUP)
    g_s3 = jnp.pad(rows_gkt(jnp.transpose(gn, (0, 1, 3, 2)), 3), ((0, 0), (0, 0), (0, 125)))
    pad8 = lambda a: jnp.pad(a.reshape(DEC_BATCH, DEC_SEQ, -1), ((0, 0), (0, 8 - DEC_SEQ), (0, 0)))
    nkv = pad8(z_s[:, C_PAG + 2 * N_KV * HEAD_DIM:C_PAG + 4 * N_KV * HEAD_DIM])
    nwin = pad8(z_s[:, C_WIN:C_WIN + 2 * N_KV * HEAD_DIM])
    win3 = state_win[0].reshape(DEC_BATCH, WINDOW, 2 * N_KV * HEAD_DIM)
    o_s = _attn_sample(pt_sample, q_s, g_s3, ck_s, cv_s, cache3, nkv, win3, nwin,
                       tabs["bcmp"], tabs["bsel"], tabs["bnew"], tabs["bwin"], tabs["ov_s"], tabs["e_all"])
    o_s = jnp.transpose(o_s.reshape(DEC_BATCH, GROUP, N_KV, DEC_SEQ, HEAD_DIM), (0, 3, 2, 1, 4))
    oattn = jnp.concatenate([o_p, o_s.reshape(M_SAMPLE, N_HEADS * HEAD_DIM).astype(BF16)], axis=0)

    merged = _merge(z, grnn, oattn, w_br_rnn[0].astype(BF16), w_br_attn[0].astype(BF16))
    x = _out_proj(x, merged, w_out[0].astype(BF16))
    y = _ffn(x, ln_ffn2[0], w_ffn2_gate[0].astype(BF16), w_ffn2_up[0].astype(BF16), w_ffn2_down[0].astype(BF16),
             ln_final, True)

    kv = z[:, C_PAG:C_PAG + 4 * N_KV * HEAD_DIM]
    wn = z[:, C_WIN:C_WIN + 2 * N_KV * HEAD_DIM]
    ux = z[:, C_UX:C_UX + D_RNN]
    win_p = wn[:M_PROMPT].reshape(BATCH, SEQ, 2, N_KV, HEAD_DIM)[:, SEQ - WINDOW:]
    win_s = jnp.concatenate([state_win[0], wn[M_PROMPT:].reshape(DEC_BATCH, DEC_SEQ, 2, N_KV, HEAD_DIM)], axis=1)
    return (
        y[:M_PROMPT].reshape(BATCH, SEQ, D_MODEL),
        y[M_PROMPT:].reshape(DEC_BATCH, DEC_SEQ, D_MODEL),
        kv[:M_PROMPT].reshape(1, BATCH, SEQ, 4, N_KV, HEAD_DIM),
        kv[M_PROMPT:].reshape(1, DEC_BATCH, DEC_SEQ, 4, N_KV, HEAD_DIM),
        win_p[None],
        win_s[None, :, DEC_SEQ:],
        ux[:M_PROMPT].reshape(BATCH, SEQ, D_RNN)[None, :, SEQ - (CONV_W - 1):],
        ux[M_PROMPT:].reshape(DEC_BATCH, DEC_SEQ, D_RNN)[None, :, DEC_SEQ - (CONV_W - 1):],
        h_p[None, :, 7],
        h_s[None],
    )
```

```python
import functools
import math

import numpy as np
import jax
import jax.numpy as jnp
from jax import lax
from jax.experimental import pallas as pl
from jax.experimental.pallas import tpu as pltpu

F32 = jnp.float32
BF16 = jnp.bfloat16

D_MODEL = 4096
BATCH = 4
SEQ = 2048
DEC_BATCH = 128
DEC_SEQ = 4
PAST_LEN = 2048
PAGE_SIZE = 128
N_PAGES = PAST_LEN // PAGE_SIZE
D_RNN = D_MODEL // 2
RNN_BLOCKS = 16
RNN_BW = D_RNN // RNN_BLOCKS
CONV_W = 4
LRU_C = 8.0
N_HEADS = 16
HEAD_DIM = 128
N_KV = 4
GROUP = N_HEADS // N_KV
CMP_LEN = 32
CMP_STRIDE = 16
CMP_HID = 2 * HEAD_DIM
SEL_BLK = 64
N_SEL = 8
WINDOW = 512
N_BUCKETS = 32
MAX_EXACT = 16
MAX_DIST = 128
D_FF = ((8 * D_MODEL // 3 + 255) // 256) * 256
EPS = 1e-6
NEG = -1e30
BIG = 1e30
M_FLOOR = -1e29
Q_SCALE = HEAD_DIM ** -0.5

M_PROMPT = BATCH * SEQ
M_SAMPLE = DEC_BATCH * DEC_SEQ
M_TOK = M_PROMPT + M_SAMPLE

C_UGATE = 0
C_UX = C_UGATE + D_RNN
C_Q = C_UX + D_RNN
C_PAG = C_Q + N_HEADS * HEAD_DIM
C_WIN = C_PAG + 4 * N_KV * HEAD_DIM
C_GRNN = C_WIN + 2 * N_KV * HEAD_DIM
C_GATTN = C_GRNN + D_MODEL
C_GNSA = C_GATTN + D_MODEL
D_Z = C_GNSA + N_KV * 128

TM = 512
TF = 256
TN_IN = 1280
TN_MM = 512
VMEM_LIMIT = 56 * 2 ** 20


def _cparams(n_axes, vmem=VMEM_LIMIT):
    return pltpu.CompilerParams(dimension_semantics=("arbitrary",) * n_axes, vmem_limit_bytes=vmem)


def _dot(a, b):
    return jnp.dot(a, b, preferred_element_type=F32)


def _dot_nt(a, b):
    return lax.dot_general(a, b, (((1,), (1,)), ((), ())), preferred_element_type=F32)


def _dot_split3(a, b):
    a1 = a.astype(BF16)
    r1 = a - a1.astype(F32)
    a2 = r1.astype(BF16)
    a3 = (r1 - a2.astype(F32)).astype(BF16)
    return _dot(a1, b) + _dot(a2, b) + _dot(a3, b)


def _rms(x, g):
    return x * lax.rsqrt(jnp.mean(x * x, axis=-1, keepdims=True) + EPS) * g


def _ffn_body(x_ref, ln_ref, wg_ref, wu_ref, wd_ref, lnf_ref, o_ref, xn_ref, *, n_f, final_norm):
    f = pl.program_id(1)

    @pl.when(f == 0)
    def _():
        x = x_ref[...]
        xn_ref[...] = _rms(x, ln_ref[...]).astype(BF16)
        o_ref[...] = 2.0 * x

    xn = xn_ref[...]
    g = _dot(xn, wg_ref[...])
    u = _dot(xn, wu_ref[...])
    h = (g * jax.nn.sigmoid(g) * u).astype(BF16)
    o_ref[...] += _dot(h, wd_ref[...])

    @pl.when(f == n_f - 1)
    def _():
        y = 0.5 * o_ref[...]
        if final_norm:
            y = _rms(y, lnf_ref[...])
        o_ref[...] = y


def _ffn(x, ln, wg, wu, wd, lnf, final_norm):
    m = x.shape[0]
    n_f = D_FF // TF
    return pl.pallas_call(
        functools.partial(_ffn_body, n_f=n_f, final_norm=final_norm),
        grid=(m // TM, n_f),
        in_specs=[
            pl.BlockSpec((TM, D_MODEL), lambda i, f: (i, 0), pipeline_mode=pl.Buffered(1)),
            pl.BlockSpec((1, D_MODEL), lambda i, f: (0, 0)),
            pl.BlockSpec((D_MODEL, TF), lambda i, f: (0, f)),
            pl.BlockSpec((D_MODEL, TF), lambda i, f: (0, f)),
            pl.BlockSpec((TF, D_MODEL), lambda i, f: (f, 0)),
            pl.BlockSpec((1, D_MODEL), lambda i, f: (0, 0)),
        ],
        out_specs=pl.BlockSpec((TM, D_MODEL), lambda i, f: (i, 0)),
        out_shape=jax.ShapeDtypeStruct((m, D_MODEL), F32),
        scratch_shapes=[pltpu.VMEM((TM, D_MODEL), BF16)],
        compiler_params=_cparams(2),
        name="ffn",
    )(x, ln.reshape(1, D_MODEL), wg, wu, wd, lnf.reshape(1, D_MODEL))


def _in_proj_body(x_ref, ln_ref, w_ref, o_ref, xn_ref):
    @pl.when(pl.program_id(1) == 0)
    def _():
        xn_ref[...] = _rms(x_ref[...], ln_ref[...]).astype(BF16)

    o_ref[...] = _dot(xn_ref[...], w_ref[...])


def _in_proj(x, ln, w):
    m = x.shape[0]
    return pl.pallas_call(
        _in_proj_body,
        grid=(m // TM, D_Z // TN_IN),
        in_specs=[
            pl.BlockSpec((TM, D_MODEL), lambda i, j: (i, 0), pipeline_mode=pl.Buffered(1)),
            pl.BlockSpec((1, D_MODEL), lambda i, j: (0, 0)),
            pl.BlockSpec((D_MODEL, TN_IN), lambda i, j: (0, j)),
        ],
        out_specs=pl.BlockSpec((TM, TN_IN), lambda i, j: (i, j)),
        out_shape=jax.ShapeDtypeStruct((m, D_Z), F32),
        scratch_shapes=[pltpu.VMEM((TM, D_MODEL), BF16)],
        compiler_params=_cparams(2),
        name="in_proj",
    )(x, ln.reshape(1, D_MODEL), w)


def _merge_body(gr_ref, oa_ref, wr_ref, wa_ref, zr_ref, za_ref, o_ref):
    y_rnn = _dot(gr_ref[...], wr_ref[...])
    y_attn = _dot(oa_ref[...], wa_ref[...])
    o_ref[...] = (jax.nn.sigmoid(zr_ref[...]) * y_rnn + jax.nn.sigmoid(za_ref[...]) * y_attn).astype(BF16)


def _merge(z, grnn, oattn, w_rnn, w_attn):
    m = z.shape[0]
    cr, ca = C_GRNN // TN_MM, C_GATTN // TN_MM
    return pl.pallas_call(
        _merge_body,
        grid=(m // TM, D_MODEL // TN_MM),
        in_specs=[
            pl.BlockSpec((TM, D_RNN), lambda i, j: (i, 0)),
            pl.BlockSpec((TM, N_HEADS * HEAD_DIM), lambda i, j: (i, 0)),
            pl.BlockSpec((D_RNN, TN_MM), lambda i, j: (0, j)),
            pl.BlockSpec((N_HEADS * HEAD_DIM, TN_MM), lambda i, j: (0, j)),
            pl.BlockSpec((TM, TN_MM), lambda i, j: (i, cr + j)),
            pl.BlockSpec((TM, TN_MM), lambda i, j: (i, ca + j)),
        ],
        out_specs=pl.BlockSpec((TM, TN_MM), lambda i, j: (i, j)),
        out_shape=jax.ShapeDtypeStruct((m, D_MODEL), BF16),
        compiler_params=_cparams(2),
        name="merge",
    )(grnn, oattn, w_rnn, w_attn, z, z)


def _out_proj_body(a_ref, w_ref, x_ref, o_ref):
    o_ref[...] = x_ref[...] + _dot(a_ref[...], w_ref[...])


def _out_proj(x, a, w):
    m = x.shape[0]
    return pl.pallas_call(
        _out_proj_body,
        grid=(m // TM, D_MODEL // TN_MM),
        in_specs=[
            pl.BlockSpec((TM, D_MODEL), lambda i, j: (i, 0)),
            pl.BlockSpec((D_MODEL, TN_MM), lambda i, j: (0, j)),
            pl.BlockSpec((TM, TN_MM), lambda i, j: (i, j)),
        ],
        out_specs=pl.BlockSpec((TM, TN_MM), lambda i, j: (i, j)),
        out_shape=jax.ShapeDtypeStruct((m, D_MODEL), F32),
        compiler_params=_cparams(2),
        name="out_proj",
    )(a, w, x)


def _softplus(v):
    return jnp.maximum(v, 0.0) + jnp.log1p(jnp.exp(-jnp.abs(v)))


def _lru_coeffs(xc, wa_ref, ba, wi_ref, bi, sp, n_blk):
    xb = xc.astype(BF16)
    ra = jnp.concatenate([_dot(xb[:, b * RNN_BW:(b + 1) * RNN_BW], wa_ref[b]) for b in range(n_blk)], axis=1)
    ia = jnp.concatenate([_dot(xb[:, b * RNN_BW:(b + 1) * RNN_BW], wi_ref[b]) for b in range(n_blk)], axis=1)
    r = jax.nn.sigmoid(ra + ba)
    i = jax.nn.sigmoid(ia + bi)
    log_a = -LRU_C * r * sp
    a = jnp.exp(log_a)
    bt = jnp.sqrt(-jnp.tanh(log_a) * (a * a + 1.0)) * (i * xc)
    return a, bt


RNN_TC = 256


def _rnn_prompt_body(ug_ref, ux_ref, cw_ref, cb_ref, wa_ref, ba_ref, wi_ref, bi_ref, lam_ref,
                     g_ref, h_ref, tail_ref, hc_ref):
    c = pl.program_id(1)
    tc = RNN_TC

    @pl.when(c == 0)
    def _():
        tail_ref[...] = jnp.zeros_like(tail_ref)
        hc_ref[...] = jnp.zeros_like(hc_ref)

    u = ux_ref[...]
    tail = tail_ref[...]
    row8 = lax.broadcasted_iota(jnp.int32, (8, D_RNN), 0)
    xc = cb_ref[...] + cw_ref[CONV_W - 1:CONV_W, :] * u
    for j in range(1, CONV_W):
        r = pltpu.roll(u, j, axis=0)
        first = jnp.where(row8 >= j, r[0:8], pltpu.roll(tail, j, axis=0))
        shifted = jnp.concatenate([first, r[8:]], axis=0)
        xc = xc + cw_ref[CONV_W - 1 - j:CONV_W - j, :] * shifted
    tail_ref[...] = u[tc - 8:tc]

    a, bt = _lru_coeffs(xc, wa_ref, ba_ref[...], wi_ref, bi_ref[...], _softplus(-lam_ref[...]), RNN_BLOCKS)

    row = lax.broadcasted_iota(jnp.int32, (tc, D_RNN), 0)
    s = 1
    while s < tc:
        keep = row >= s
        a_sh = jnp.where(keep, pltpu.roll(a, s, axis=0), 1.0)
        b_sh = jnp.where(keep, pltpu.roll(bt, s, axis=0), 0.0)
        bt = a * b_sh + bt
        a = a * a_sh
        s *= 2
    h = bt + a * hc_ref[7:8, :]
    hc_ref[...] = h[tc - 8:tc]
    g_ref[...] = (h * jax.nn.gelu(ug_ref[...])).astype(BF16)

    @pl.when(c == pl.num_programs(1) - 1)
    def _():
        h_ref[0] = h[tc - 8:tc]


def _rnn_prompt(z, cw, cb, wa, ba, wi, bi, lam):
    nc = SEQ // RNN_TC
    vec = lambda: pl.BlockSpec((1, D_RNN), lambda n, c: (0, 0))
    blk = lambda: pl.BlockSpec((RNN_BLOCKS, RNN_BW, RNN_BW), lambda n, c: (0, 0, 0))
    return pl.pallas_call(
        _rnn_prompt_body,
        grid=(BATCH, nc),
        in_specs=[
            pl.BlockSpec((RNN_TC, D_RNN), lambda n, c: (n * nc + c, C_UGATE // D_RNN)),
            pl.BlockSpec((RNN_TC, D_RNN), lambda n, c: (n * nc + c, C_UX // D_RNN)),
            pl.BlockSpec((CONV_W, D_RNN), lambda n, c: (0, 0)),
            vec(), blk(), vec(), blk(), vec(), vec(),
        ],
        out_specs=[
            pl.BlockSpec((RNN_TC, D_RNN), lambda n, c: (n * nc + c, 0)),
            pl.BlockSpec((1, 8, D_RNN), lambda n, c: (n, 0, 0)),
        ],
        out_shape=[jax.ShapeDtypeStruct((M_PROMPT, D_RNN), BF16),
                   jax.ShapeDtypeStruct((BATCH, 8, D_RNN), F32)],
        scratch_shapes=[pltpu.VMEM((8, D_RNN), F32), pltpu.VMEM((8, D_RNN), F32)],
        compiler_params=_cparams(2),
        name="rnn_prompt",
    )(z, z, cw, cb, wa, ba, wi, bi, lam)


RNN_SC = 512


def _rnn_sample_body(ug_ref, ux_ref, buf_ref, h0_ref, cw_ref, cb_ref, wa_ref, ba_ref, wi_ref, bi_ref, lam_ref,
                     g_ref, h_ref):
    full = [buf_ref[j] for j in range(CONV_W - 1)] + [ux_ref[t] for t in range(DEC_SEQ)]
    sp = _softplus(-lam_ref[...])
    h = h0_ref[...]
    for t in range(DEC_SEQ):
        xc = cb_ref[...]
        for k in range(CONV_W):
            xc = xc + full[t + k] * cw_ref[k:k + 1, :]
        a, bt = _lru_coeffs(xc, wa_ref, ba_ref[...], wi_ref, bi_ref[...], sp, RNN_SC // RNN_BW)
        h = a * h + bt
        g_ref[t] = (h * jax.nn.gelu(ug_ref[t])).astype(BF16)
    h_ref[...] = h


def _rnn_sample(ug_t, ux_t, buf_t, h0, cw, cb, wa, ba, wi, bi, lam):
    nb = RNN_SC // RNN_BW
    vec = lambda: pl.BlockSpec((1, RNN_SC), lambda c: (0, c))
    blk = lambda: pl.BlockSpec((nb, RNN_BW, RNN_BW), lambda c: (c, 0, 0))
    return pl.pallas_call(
        _rnn_sample_body,
        grid=(D_RNN // RNN_SC,),
        in_specs=[
            pl.BlockSpec((DEC_SEQ, DEC_BATCH, RNN_SC), lambda c: (0, 0, c)),
            pl.BlockSpec((DEC_SEQ, DEC_BATCH, RNN_SC), lambda c: (0, 0, c)),
            pl.BlockSpec((CONV_W - 1, DEC_BATCH, RNN_SC), lambda c: (0, 0, c)),
            pl.BlockSpec((DEC_BATCH, RNN_SC), lambda c: (0, c)),
            pl.BlockSpec((CONV_W, RNN_SC), lambda c: (0, c)),
            vec(), blk(), vec(), blk(), vec(), vec(),
        ],
        out_specs=[
            pl.BlockSpec((DEC_SEQ, DEC_BATCH, RNN_SC), lambda c: (0, 0, c)),
            pl.BlockSpec((DEC_BATCH, RNN_SC), lambda c: (0, c)),
        ],
        out_shape=[jax.ShapeDtypeStruct((DEC_SEQ, DEC_BATCH, D_RNN), BF16),
                   jax.ShapeDtypeStruct((DEC_BATCH, D_RNN), F32)],
        compiler_params=_cparams(1),
        name="rnn_sample",
    )(ug_t, ux_t, buf_t, h0, cw, cb, wa, ba, wi, bi, lam)


N_SEG = PAGE_SIZE // CMP_STRIDE
SEG_ROWS = N_PAGES * N_SEG
N_PAIR = CMP_STRIDE // 2


CMP_PG = 4


def _compress_body(pt_ref, *refs):
    page_refs = refs[:CMP_PG]
    w1_ref, pos_ref, b1_ref, w2_ref, ck_ref, cv_ref, stage_ref, pterm_ref, slab_ref = refs[CMP_PG:]
    n = pl.program_id(0)
    q = pl.program_id(1)

    @pl.when((n == 0) & (q == 0))
    def _():
        for kind in range(2):
            acc = jnp.zeros((8, 2 * CMP_HID), F32)
            for pr in range(N_PAIR):
                acc = acc + _dot(pos_ref[:, pr * 256:(pr + 1) * 256].astype(BF16), w1_ref[kind, pr])
            pterm_ref[kind] = acc

    for j in range(CMP_PG):
        for kk in range(2 * N_KV):
            slab_ref[j * 2 * N_KV + kk] = page_refs[j][:, kk * HEAD_DIM:(kk + 1) * HEAD_DIM]

    for j in range(CMP_PG):
        seg0 = (q * CMP_PG + j) * N_SEG
        for kind in range(2):
            for k in range(N_KV):
                for l in range(CMP_STRIDE):
                    piece = slab_ref[(j * 2 + kind) * N_KV + k, pl.ds(l, N_SEG, stride=CMP_STRIDE), :]
                    stage_ref[kind, l // 2, pl.ds(k * SEG_ROWS + seg0, N_SEG),
                              pl.ds((l % 2) * HEAD_DIM, HEAD_DIM)] = piece

    @pl.when(q == N_PAGES // CMP_PG - 1)
    def _():
        for kind, out_ref in ((0, ck_ref), (1, cv_ref)):
            acc = jnp.zeros((N_KV * SEG_ROWS, 2 * CMP_HID), F32)
            for pr in range(N_PAIR):
                acc = acc + _dot(stage_ref[kind, pr].astype(BF16), w1_ref[kind, pr])
            nxt = pltpu.roll(acc[:, CMP_HID:], N_KV * SEG_ROWS - 1, axis=0)
            pt = pterm_ref[kind]
            posterm = pt[0:1, :CMP_HID] + pt[1:2, CMP_HID:] + b1_ref[kind]
            hid = acc[:, :CMP_HID] + nxt + posterm
            out_ref[0] = _dot(jax.nn.gelu(hid).astype(BF16), w2_ref[kind]).astype(BF16)


def _compress(src3, pt_flat, col_blk, n_seq, w1, pos, b1, w2, name):
    page_spec = lambda j: pl.BlockSpec((None, PAGE_SIZE, 2 * N_KV * HEAD_DIM),
                                       lambda n, p, pt: (pt[n * N_PAGES + p * CMP_PG + j], 0, col_blk))
    grid_spec = pltpu.PrefetchScalarGridSpec(
        num_scalar_prefetch=1,
        grid=(n_seq, N_PAGES // CMP_PG),
        in_specs=[page_spec(j) for j in range(CMP_PG)] + [
            pl.BlockSpec((2, N_PAIR, 256, 2 * CMP_HID), lambda n, p, pt: (0, 0, 0, 0)),
            pl.BlockSpec((8, CMP_STRIDE * HEAD_DIM), lambda n, p, pt: (0, 0)),
            pl.BlockSpec((2, 1, CMP_HID), lambda n, p, pt: (0, 0, 0)),
            pl.BlockSpec((2, CMP_HID, HEAD_DIM), lambda n, p, pt: (0, 0, 0)),
        ],
        out_specs=[
            pl.BlockSpec((1, N_KV * SEG_ROWS, HEAD_DIM), lambda n, p, pt: (n, 0, 0)),
            pl.BlockSpec((1, N_KV * SEG_ROWS, HEAD_DIM), lambda n, p, pt: (n, 0, 0)),
        ],
        scratch_shapes=[pltpu.VMEM((2, N_PAIR, N_KV * SEG_ROWS, 256), F32),
                        pltpu.VMEM((2, 8, 2 * CMP_HID), F32),
                        pltpu.VMEM((CMP_PG * 2 * N_KV, PAGE_SIZE, HEAD_DIM), F32)],
    )
    shp = jax.ShapeDtypeStruct((n_seq, N_KV * SEG_ROWS, HEAD_DIM), BF16)
    return pl.pallas_call(
        _compress_body,
        grid_spec=grid_spec,
        out_shape=[shp, shp],
        compiler_params=_cparams(2),
        name=name,
    )(pt_flat, *([src3] * CMP_PG), w1, pos, b1, w2)


def _select_blocks(score, cur, n_blk):
    jj = lax.broadcasted_iota(jnp.int32, score.shape, 1)
    forced = (jj == 0) | (jj == cur) | (jj == cur - 1)
    sc = jnp.where(forced, BIG, jnp.where(jj <= cur, score, NEG))
    rank = jnp.zeros(score.shape, F32)
    for i in range(n_blk):
        si = sc[:, i:i + 1]
        beats = (si > sc) | ((si == sc) & (jj > i))
        rank = rank + jnp.where(beats, 1.0, 0.0)
    sel = (rank < float(min(N_SEL, n_blk))) & (jj <= cur) & (jj < n_blk)
    return jnp.where(sel, 1.0, 0.0).astype(BF16)


def _softmax_rows(logits):
    m = jnp.maximum(jnp.max(logits, axis=-1, keepdims=True), M_FLOOR)
    e = jnp.exp(logits - m)
    s = jnp.sum(e, axis=-1, keepdims=True)
    return e / jnp.where(s > 0.0, s, 1.0)


def _attn_prompt_body(zq_ref, zg_ref, ck_ref, cv_ref, ks_ref, vs_ref, kw_ref, vw_ref,
                      bc_ref, tz_ref, ov_ref, e_ref, o_ref,
                      ksb, vsb, kwb, vwb, selm_ref):
    qt = pl.program_id(2)
    rows = GROUP * 128

    @pl.when(qt == 0)
    def _():
        ksb[...] = ks_ref[...].astype(BF16)
        vsb[...] = vs_ref[...].astype(BF16)
        kwb[...] = kw_ref[...].astype(BF16)
        vwb[...] = vw_ref[...].astype(BF16)

    q = zq_ref[...] * Q_SCALE
    qq = jnp.concatenate([q[:, g * HEAD_DIM:(g + 1) * HEAD_DIM] for g in range(GROUP)], axis=0).astype(BF16)

    pc = _softmax_rows(_dot_nt(qq, ck_ref[0]) + bc_ref[0].reshape(rows, 128))
    o_c = _dot(pc.astype(BF16), cv_ref[0])

    ps = pc[0:128] + pc[128:256] + pc[256:384] + pc[384:512]
    score = _dot_split3(ps, ov_ref[...])
    tt = lax.broadcasted_iota(jnp.int32, (128, 128), 0)
    cur = jnp.right_shift(qt * 128 + tt, 6)
    sel = _select_blocks(score, cur, SEQ // SEL_BLK)
    for kt in range(SEQ // 128):
        selm_ref[kt] = (_dot(sel, e_ref[kt]) - 1.0) * BIG

    def flash(k_ref, v_ref, lo, use_sel, clamp):
        def body(kt, carry):
            m, l, acc = carry
            off = pl.multiple_of(kt * 128, 128)
            s = _dot_nt(qq, k_ref[pl.ds(off, 128), :]).reshape(GROUP, 128, 128)
            s = s + tz_ref[jnp.minimum(qt - kt, clamp)]
            if use_sel:
                s = s + selm_ref[kt][None]
            s = s.reshape(rows, 128)
            m_new = jnp.maximum(m, jnp.max(s, axis=-1, keepdims=True))
            alpha = jnp.exp(m - m_new)
            pe = jnp.exp(s - m_new)
            l = alpha * l + jnp.sum(pe, axis=-1, keepdims=True)
            acc = alpha * acc + _dot(pe.astype(BF16), v_ref[pl.ds(off, 128), :])
            return m_new, l, acc

        init = (jnp.full((rows, 1), M_FLOOR, F32), jnp.zeros((rows, 1), F32), jnp.zeros((rows, HEAD_DIM), F32))
        _, l, acc = lax.fori_loop(lo, qt + 1, body, init)
        return acc / jnp.where(l > 0.0, l, 1.0)

    o_s = flash(ksb, vsb, 0, True, 2)
    o_w = flash(kwb, vwb, jnp.maximum(qt - WINDOW // 128, 0), False, WINDOW // 128)

    gates = jax.nn.sigmoid(zg_ref[...])
    outs = []
    for g in range(GROUP):
        r = slice(g * 128, (g + 1) * 128)
        outs.append(gates[:, g:g + 1] * o_c[r] + gates[:, GROUP + g:GROUP + g + 1] * o_s[r]
                    + gates[:, 2 * GROUP + g:2 * GROUP + g + 1] * o_w[r])
    o_ref[...] = jnp.concatenate(outs, axis=1).astype(BF16)


def _attn_prompt(z, ck, cv, bias_cmp, tz, ov, emat):
    nq = SEQ // 128
    kv_col = lambda base, kind: (lambda n, k, t: (n, (base + kind * N_KV * HEAD_DIM) // HEAD_DIM + k))
    kvspec = lambda base, kind: pl.BlockSpec((SEQ, HEAD_DIM), kv_col(base, kind))
    return pl.pallas_call(
        _attn_prompt_body,
        grid=(BATCH, N_KV, nq),
        in_specs=[
            pl.BlockSpec((128, GROUP * HEAD_DIM), lambda n, k, t: (n * nq + t, C_Q // (GROUP * HEAD_DIM) + k)),
            pl.BlockSpec((128, 128), lambda n, k, t: (n * nq + t, C_GNSA // 128 + k)),
            pl.BlockSpec((1, SEG_ROWS, HEAD_DIM), lambda n, k, t: (n * N_KV + k, 0, 0)),
            pl.BlockSpec((1, SEG_ROWS, HEAD_DIM), lambda n, k, t: (n * N_KV + k, 0, 0)),
            kvspec(C_PAG, 2), kvspec(C_PAG, 3), kvspec(C_WIN, 0), kvspec(C_WIN, 1),
            pl.BlockSpec((1, GROUP, 128, 128), lambda n, k, t: (t, k, 0, 0)),
            pl.BlockSpec((WINDOW // 128 + 1, GROUP, 128, 128), lambda n, k, t: (0, k, 0, 0)),
            pl.BlockSpec((128, 128), lambda n, k, t: (0, 0)),
            pl.BlockSpec((SEQ // 128, 128, 128), lambda n, k, t: (0, 0, 0)),
        ],
        out_specs=pl.BlockSpec((128, GROUP * HEAD_DIM), lambda n, k, t: (n * nq + t, k)),
        out_shape=jax.ShapeDtypeStruct((M_PROMPT, N_HEADS * HEAD_DIM), BF16),
        scratch_shapes=[pltpu.VMEM((SEQ, HEAD_DIM), BF16)] * 4 + [pltpu.VMEM((SEQ // 128, 128, 128), F32)],
        compiler_params=_cparams(3),
        name="attn_prompt",
    )(z, z, ck, cv, z, z, z, z, bias_cmp, tz, ov, emat)


S_ROWS = GROUP * N_KV * DEC_SEQ


def _attn_sample_body(pt_ref, q_ref, gs_ref, ck_ref, cv_ref, *rest):
    page_refs = rest[:N_PAGES]
    (nkv_ref, win_ref, nwin_ref, bcmp_ref, bsel_ref, bnew_ref, bwin_ref, ov_ref, e_ref,
     o_ref, s_ref, nk_ref, nw_ref) = rest[N_PAGES:]
    kv_of_row = jnp.bitwise_and(jnp.right_shift(lax.broadcasted_iota(jnp.int32, (S_ROWS, 1), 0), 2), N_KV - 1)
    qq = (q_ref[0] * Q_SCALE).astype(BF16)

    def logits(get_k):
        out = None
        for k in range(N_KV):
            s = jnp.where(kv_of_row == k, _dot_nt(qq, get_k(k).astype(BF16)), 0.0)
            out = s if out is None else out + s
        return out

    def weighted(pe, get_v):
        out = None
        for k in range(N_KV):
            o = _dot(jnp.where(kv_of_row == k, pe, 0.0).astype(BF16), get_v(k).astype(BF16))
            out = o if out is None else out + o
        return out

    col = lambda k, half: pl.ds(half * N_KV * HEAD_DIM + k * HEAD_DIM, HEAD_DIM)

    rowmax = lambda s: jnp.max(s, axis=-1, keepdims=True)
    rowsum = lambda s: jnp.sum(s, axis=-1, keepdims=True)

    @pl.when(pl.program_id(0) == 0)
    def _():
        nk_ref[...] = jnp.zeros_like(nk_ref)
        nw_ref[...] = jnp.zeros_like(nw_ref)

    nk_ref[0:8, :] = nkv_ref[0]
    nw_ref[0:8, :] = nwin_ref[0]

    pc = _softmax_rows(logits(lambda k: ck_ref[0, pl.ds(k * SEG_ROWS, SEG_ROWS), :]) + bcmp_ref[...])
    o_c = weighted(pc, lambda k: cv_ref[0, pl.ds(k * SEG_ROWS, SEG_ROWS), :])
    ps = pc + pltpu.roll(pc, 16, axis=0) + pltpu.roll(pc, 32, axis=0) + pltpu.roll(pc, 48, axis=0)
    score = _dot_split3(ps, ov_ref[...])
    n_blk = -(-(PAST_LEN + DEC_SEQ) // SEL_BLK)
    cur = jnp.full((S_ROWS, 128), PAST_LEN // SEL_BLK, jnp.int32)
    sel = _select_blocks(score, cur, n_blk)
    key_mask = (_dot(sel, e_ref[...]) - 1.0) * BIG

    m = jnp.full((S_ROWS, 1), M_FLOOR, F32)
    for p in range(N_PAGES):
        s = (logits(lambda k: page_refs[p][:, col(k, 0)]) + bsel_ref[p]
             + key_mask[:, p * PAGE_SIZE:(p + 1) * PAGE_SIZE])
        s_ref[p] = s
        m = jnp.maximum(m, rowmax(s))
    sn = logits(lambda k: nk_ref[:, col(k, 0)]) + bnew_ref[...]
    m = jnp.maximum(m, rowmax(sn))
    pn = jnp.exp(sn - m)
    l = rowsum(pn)
    acc = weighted(pn, lambda k: nk_ref[:, col(k, 1)])
    for p in range(N_PAGES):
        pe = jnp.exp(s_ref[p] - m)
        l = l + rowsum(pe)
        acc = acc + weighted(pe, lambda k: page_refs[p][:, col(k, 1)])
    o_s = acc / jnp.where(l > 0.0, l, 1.0)

    sw = logits(lambda k: win_ref[0, :, col(k, 0)]) + bwin_ref[...]
    sn = logits(lambda k: nw_ref[:, col(k, 0)]) + bnew_ref[...]
    m = jnp.maximum(jnp.maximum(rowmax(sw), rowmax(sn)), M_FLOOR)
    pw = jnp.exp(sw - m)
    pn = jnp.exp(sn - m)
    l = rowsum(pw) + rowsum(pn)
    o_w = weighted(pw, lambda k: win_ref[0, :, col(k, 1)]) + weighted(pn, lambda k: nw_ref[:, col(k, 1)])
    o_w = o_w / jnp.where(l > 0.0, l, 1.0)

    gates = jax.nn.sigmoid(gs_ref[0])
    o_ref[0] = gates[:, 0:1] * o_c + gates[:, 1:2] * o_s + gates[:, 2:3] * o_w


def _attn_sample(pt_flat, q_s, g_s, ck, cv, cache3, nkv, state_win3, nwin, bcmp, bsel, bnew, bwin, ov, emat):
    const2 = lambda shape: pl.BlockSpec(shape, lambda n, pt: (0, 0))
    page_spec = lambda p: pl.BlockSpec((None, PAGE_SIZE, 1024), lambda n, pt: (pt[n * N_PAGES + p], 0, 1))
    grid_spec = pltpu.PrefetchScalarGridSpec(
        num_scalar_prefetch=1,
        grid=(DEC_BATCH,),
        in_specs=[
            pl.BlockSpec((1, S_ROWS, HEAD_DIM), lambda n, pt: (n, 0, 0)),
            pl.BlockSpec((1, S_ROWS, 128), lambda n, pt: (n, 0, 0)),
            pl.BlockSpec((1, N_KV * SEG_ROWS, HEAD_DIM), lambda n, pt: (n, 0, 0)),
            pl.BlockSpec((1, N_KV * SEG_ROWS, HEAD_DIM), lambda n, pt: (n, 0, 0)),
        ] + [page_spec(p) for p in range(N_PAGES)] + [
            pl.BlockSpec((1, 8, 1024), lambda n, pt: (n, 0, 0)),
            pl.BlockSpec((1, WINDOW, 1024), lambda n, pt: (n, 0, 0)),
            pl.BlockSpec((1, 8, 1024), lambda n, pt: (n, 0, 0)),
            const2((S_ROWS, 128)),
            pl.BlockSpec((N_PAGES, S_ROWS, 128), lambda n, pt: (0, 0, 0)),
            const2((S_ROWS, 128)),
            const2((S_ROWS, WINDOW)),
            const2((128, 128)),
            const2((128, PAST_LEN)),
        ],
        out_specs=pl.BlockSpec((1, S_ROWS, HEAD_DIM), lambda n, pt: (n, 0, 0)),
        scratch_shapes=[
            pltpu.VMEM((N_PAGES, S_ROWS, 128), F32),
            pltpu.VMEM((128, 1024), F32), pltpu.VMEM((128, 1024), F32),
        ],
    )
    return pl.pallas_call(
        _attn_sample_body,
        grid_spec=grid_spec,
        out_shape=jax.ShapeDtypeStruct((DEC_BATCH, S_ROWS, HEAD_DIM), F32),
        compiler_params=_cparams(1),
        name="attn_sample",
    )(pt_flat, q_s, g_s, ck, cv, *([cache3] * N_PAGES), nkv, state_win3, nwin, bcmp, bsel, bnew, bwin, ov, emat)


def _t5_bucket(dist):
    d = jnp.maximum(dist, 0)
    df = jnp.maximum(d, 1).astype(F32)
    large = MAX_EXACT + (jnp.log(df / MAX_EXACT) / math.log(MAX_DIST / MAX_EXACT)
                         * (N_BUCKETS - MAX_EXACT)).astype(jnp.int32)
    large = jnp.minimum(large, N_BUCKETS - 1)
    return jnp.where(d < MAX_EXACT, d, large)


def _bias_lookup_body(rb_ref, idx_ref, o_ref):
    idx = idx_ref[0]
    for h in range(N_HEADS):
        acc = jnp.full(idx.shape, NEG, F32)
        for b in range(N_BUCKETS):
            acc = jnp.where(idx == b, rb_ref[b * N_HEADS + h], acc)
        o_ref[0, h] = acc


def _bias_table(rel_bias, dist, valid, name):
    p, r, _ = dist.shape
    idx = jnp.where(jnp.asarray(valid), _t5_bucket(jnp.asarray(dist, jnp.int32)), -1)
    return pl.pallas_call(
        _bias_lookup_body,
        grid=(p,),
        in_specs=[pl.BlockSpec(memory_space=pltpu.SMEM), pl.BlockSpec((1, r, 128), lambda i: (i, 0, 0))],
        out_specs=pl.BlockSpec((1, N_HEADS, r, 128), lambda i: (i, 0, 0, 0)),
        out_shape=jax.ShapeDtypeStruct((p, N_HEADS, r, 128), F32),
        compiler_params=_cparams(1),
        name=name,
    )(rel_bias.astype(F32).reshape(-1), idx)


def _overlap(nc, nb):
    cs = np.arange(nc)[:, None] * CMP_STRIDE
    js = np.arange(nb)[None, :] * SEL_BLK
    ov = np.clip(np.minimum(cs + CMP_LEN, js + SEL_BLK) - np.maximum(cs, js), 0, None) / CMP_LEN
    out = np.zeros((128, 128), np.float32)
    out[:nc, :nb] = ov
    return jnp.asarray(out, BF16)


def _position_tables(rel_bias):
    nc = SEG_ROWS - 1
    t = np.arange(128)[None, :, None]
    c = np.arange(128)[None, None, :]
    cend = c * CMP_STRIDE + CMP_LEN - 1
    d = np.arange(WINDOW // 128 + 1)[:, None, None] * 128 + t - c
    tz = _bias_table(rel_bias, d, (d >= 0) & (d < WINDOW), "bias_tiles")
    d = np.arange(SEQ // 128)[:, None, None] * 128 + t - cend
    bias_cmp = _bias_table(rel_bias, d, (d >= 0) & (c < nc), "bias_cmp")
    ts = np.arange(8)[None, :, None]
    qpos = PAST_LEN + ts
    live = ts < DEC_SEQ
    d_cmp = qpos - cend
    d_sel = qpos - (np.arange(N_PAGES)[:, None, None] * PAGE_SIZE + c)
    d_new = ts - c
    d_win = qpos - (PAST_LEN - WINDOW + np.arange(WINDOW // 128)[:, None, None] * 128 + c)
    d = np.concatenate([d_cmp, d_sel, d_new, d_win], axis=0)
    valid = np.concatenate([(d_cmp >= 0) & (c < nc), d_sel >= 0, (d_new >= 0) & (c < DEC_SEQ),
                            (d_win >= 0) & (d_win < WINDOW)], axis=0) & live
    o = _bias_table(rel_bias, d, valid, "bias_sample")[:, :, :DEC_SEQ]
    o = jnp.transpose(o.reshape(-1, N_KV, GROUP, DEC_SEQ, 128), (0, 2, 1, 3, 4)).reshape(-1, S_ROWS, 128)
    bcmp, bsel, bnew = o[0], o[1:1 + N_PAGES], o[1 + N_PAGES]
    bwin = jnp.transpose(o[2 + N_PAGES:], (1, 0, 2)).reshape(S_ROWS, WINDOW)
    keys = np.arange(SEQ)
    e_all = (np.arange(128)[:, None] == (keys // SEL_BLK)[None, :]).astype(np.float32)
    e_tiles = jnp.asarray(e_all.reshape(128, SEQ // 128, 128).transpose(1, 0, 2), BF16)
    return dict(tz=tz, bias_cmp=bias_cmp, bcmp=bcmp, bsel=bsel, bnew=bnew, bwin=bwin,
                ov_p=_overlap(nc, SEQ // SEL_BLK), ov_s=_overlap(nc, -(-(PAST_LEN + DEC_SEQ) // SEL_BLK)),
                e_tiles=e_tiles, e_all=jnp.asarray(e_all, BF16))


def _permute_w_in(w_in):
    c0 = 2 * D_RNN + N_HEADS * HEAD_DIM + 6 * N_KV * HEAD_DIM
    g_nsa = w_in[:, c0:c0 + 3 * N_HEADS].reshape(D_MODEL, N_KV, GROUP, 3)
    g_nsa = jnp.transpose(g_nsa, (0, 1, 3, 2)).reshape(D_MODEL, N_KV, 3 * GROUP)
    g_nsa = jnp.pad(g_nsa, ((0, 0), (0, 0), (0, 128 - 3 * GROUP))).reshape(D_MODEL, N_KV * 128)
    return jnp.concatenate([w_in[:, :c0], w_in[:, c0 + 3 * N_HEADS:], g_nsa], axis=1).astype(BF16)


def _cmp_weights(w1_k, w1_v, b1_k, b1_v, w2_k, w2_v, pos):
    def cat(w1):
        w = w1.reshape(2, CMP_STRIDE * HEAD_DIM, CMP_HID)
        return jnp.concatenate([w[0], w[1]], axis=1).reshape(N_PAIR, 256, 2 * CMP_HID)
    w1 = jnp.stack([cat(w1_k), cat(w1_v)]).astype(BF16)
    posm = jnp.pad(pos.reshape(2, CMP_STRIDE * HEAD_DIM), ((0, 6), (0, 0)))
    b1 = jnp.stack([b1_k, b1_v]).reshape(2, 1, CMP_HID)
    w2 = jnp.stack([w2_k, w2_v]).astype(BF16)
    return w1, posm, b1, w2


def kernel(x_prompt, x_sample, cache_kv, page_table, state_win, state_conv, state_h, rel_bias, ln_final, ln_ffn1, w_ffn1_gate, w_ffn1_up, w_ffn1_down, ln_mix, w_in, conv_w, conv_b, rg_wa, rg_ba, rg_wi, rg_bi, rg_lambda, cmp_pos, cmp_k_w1, cmp_k_b1, cmp_k_w2, cmp_v_w1, cmp_v_b1, cmp_v_w2, w_br_rnn, w_br_attn, w_out, ln_ffn2, w_ffn2_gate, w_ffn2_up, w_ffn2_down):
    tabs = _position_tables(rel_bias)
    x = jnp.concatenate([x_prompt.reshape(M_PROMPT, D_MODEL), x_sample.reshape(M_SAMPLE, D_MODEL)], axis=0)

    x = _ffn(x, ln_ffn1[0], w_ffn1_gate[0].astype(BF16), w_ffn1_up[0].astype(BF16), w_ffn1_down[0].astype(BF16),
             ln_final, False)
    z = _in_proj(x, ln_mix[0], _permute_w_in(w_in[0]))
    z_s = z[M_PROMPT:]

    vec = lambda v: v.reshape(1, D_RNN)
    rnn_w = (conv_w[0], vec(conv_b[0]), rg_wa[0].astype(BF16), vec(rg_ba[0]), rg_wi[0].astype(BF16),
             vec(rg_bi[0]), vec(rg_lambda[0]))
    g_p, h_p = _rnn_prompt(z, *rnn_w)
    tmajor = lambda a: jnp.transpose(a.reshape(DEC_BATCH, -1, D_RNN), (1, 0, 2))
    g_s, h_s = _rnn_sample(tmajor(z_s[:, C_UGATE:C_UGATE + D_RNN]), tmajor(z_s[:, C_UX:C_UX + D_RNN]),
                           tmajor(state_conv[0]), state_h[0], *rnn_w)
    grnn = jnp.concatenate([g_p, jnp.transpose(g_s, (1, 0, 2)).reshape(M_SAMPLE, D_RNN)], axis=0)

    cw = _cmp_weights(cmp_k_w1[0], cmp_v_w1[0], cmp_k_b1[0], cmp_v_b1[0], cmp_k_w2[0], cmp_v_w2[0], cmp_pos[0])
    pt_prompt = jnp.arange(BATCH * N_PAGES, dtype=jnp.int32)
    pt_sample = page_table.reshape(-1).astype(jnp.int32)
    cache3 = cache_kv.reshape(-1, PAGE_SIZE, 4 * N_KV * HEAD_DIM)
    ck_p, cv_p = _compress(z.reshape(M_TOK // PAGE_SIZE, PAGE_SIZE, D_Z), pt_prompt, C_PAG // 1024, BATCH, *cw,
                           name="compress_prompt")
    ck_s, cv_s = _compress(cache3, pt_sample, 0, DEC_BATCH, *cw, name="compress_sample")
    o_p = _attn_prompt(z, ck_p.reshape(BATCH * N_KV, SEG_ROWS, HEAD_DIM), cv_p.reshape(BATCH * N_KV, SEG_ROWS, HEAD_DIM),
                       tabs["bias_cmp"], tabs["tz"], tabs["ov_p"], tabs["e_tiles"])

    def rows_gkt(a, width):
        a = a.reshape(DEC_BATCH, DEC_SEQ, N_KV, GROUP, width)
        return jnp.transpose(a, (0, 3, 2, 1, 4)).reshape(DEC_BATCH, S_ROWS, width)

    q_s = rows_gkt(z_s[:, C_Q:C_Q + N_HEADS * HEAD_DIM], HEAD_DIM)
    gn = z_s[:, C_GNSA:].reshape(M_SAMPLE, N_KV, 128)[:, :, :3 * GROUP].reshape(M_SAMPLE, N_KV, 3, GROUP)
    g_s3 = jnp.pad(rows_gkt(jnp.transpose(gn, (0, 1, 3, 2)), 3), ((0, 0), (0, 0), (0, 125)))
    pad8 = lambda a: jnp.pad(a.reshape(DEC_BATCH, DEC_SEQ, -1), ((0, 0), (0, 8 - DEC_SEQ), (0, 0)))
    nkv = pad8(z_s[:, C_PAG + 2 * N_KV * HEAD_DIM:C_PAG + 4 * N_KV * HEAD_DIM])
    nwin = pad8(z_s[:, C_WIN:C_WIN + 2 * N_KV * HEAD_DIM])
    win3 = state_win.reshape(DEC_BATCH, WINDOW, 2 * N_KV * HEAD_DIM)
    o_s = _attn_sample(pt_sample, q_s, g_s3, ck_s, cv_s, cache3, nkv, win3, nwin,
                       tabs["bcmp"], tabs["bsel"], tabs["bnew"], tabs["bwin"], tabs["ov_s"], tabs["e_all"])
    o_s = jnp.transpose(o_s.reshape(DEC_BATCH, GROUP, N_KV, DEC_SEQ, HEAD_DIM), (0, 3, 2, 1, 4))
    oattn = jnp.concatenate([o_p, o_s.reshape(M_SAMPLE, N_HEADS * HEAD_DIM).astype(BF16)], axis=0)

    merged = _merge(z, grnn, oattn, w_br_rnn[0].astype(BF16), w_br_attn[0].astype(BF16))
    x = _out_proj(x, merged, w_out[0].astype(BF16))
    y = _ffn(x, ln_ffn2[0], w_ffn2_gate[0].astype(BF16), w_ffn2_up[0].astype(BF16), w_ffn2_down[0].astype(BF16),
             ln_final, True)

    kv = z[:, C_PAG:C_PAG + 4 * N_KV * HEAD_DIM]
    wn = z[:, C_WIN:C_WIN + 2 * N_KV * HEAD_DIM]
    ux = z[:, C_UX:C_UX + D_RNN]
    win_p = wn[:M_PROMPT].reshape(BATCH, SEQ, 2, N_KV, HEAD_DIM)[:, SEQ - WINDOW:]
    win_s = jnp.concatenate([state_win.reshape(DEC_BATCH, WINDOW, 2, N_KV, HEAD_DIM),
                             wn[M_PROMPT:].reshape(DEC_BATCH, DEC_SEQ, 2, N_KV, HEAD_DIM)], axis=1)
    return (
        y[:M_PROMPT].reshape(BATCH, SEQ, D_MODEL),
        y[M_PROMPT:].reshape(DEC_BATCH, DEC_SEQ, D_MODEL),
        kv[:M_PROMPT].reshape(1, BATCH, SEQ, 4, N_KV, HEAD_DIM),
        kv[M_PROMPT:].reshape(1, DEC_BATCH, DEC_SEQ, 4, N_KV, HEAD_DIM),
        win_p[None],
        win_s[None, :, DEC_SEQ:],
        ux[:M_PROMPT].reshape(BATCH, SEQ, D_RNN)[None, :, SEQ - (CONV_W - 1):],
        ux[M_PROMPT:].reshape(DEC_BATCH, DEC_SEQ, D_RNN)[None, :, DEC_SEQ - (CONV_W - 1):],
        h_p[None, :, 7],
        h_s[None],
    )
```

```python
import functools
import math

import numpy as np
import jax
import jax.numpy as jnp
from jax import lax
from jax.experimental import pallas as pl
from jax.experimental.pallas import tpu as pltpu

F32 = jnp.float32
BF16 = jnp.bfloat16

D_MODEL = 4096
BATCH = 4
SEQ = 2048
DEC_BATCH = 128
DEC_SEQ = 4
PAST_LEN = 2048
PAGE_SIZE = 128
N_PAGES = PAST_LEN // PAGE_SIZE
D_RNN = D_MODEL // 2
RNN_BLOCKS = 16
RNN_BW = D_RNN // RNN_BLOCKS
CONV_W = 4
LRU_C = 8.0
N_HEADS = 16
HEAD_DIM = 128
N_KV = 4
GROUP = N_HEADS // N_KV
CMP_LEN = 32
CMP_STRIDE = 16
CMP_HID = 2 * HEAD_DIM
SEL_BLK = 64
N_SEL = 8
WINDOW = 512
N_BUCKETS = 32
MAX_EXACT = 16
MAX_DIST = 128
D_FF = ((8 * D_MODEL // 3 + 255) // 256) * 256
EPS = 1e-6
NEG = -1e30
BIG = 1e30
M_FLOOR = -1e29
Q_SCALE = HEAD_DIM ** -0.5

M_PROMPT = BATCH * SEQ
M_SAMPLE = DEC_BATCH * DEC_SEQ
M_TOK = M_PROMPT + M_SAMPLE

C_UGATE = 0
C_UX = C_UGATE + D_RNN
C_Q = C_UX + D_RNN
C_PAG = C_Q + N_HEADS * HEAD_DIM
C_WIN = C_PAG + 4 * N_KV * HEAD_DIM
C_GRNN = C_WIN + 2 * N_KV * HEAD_DIM
C_GATTN = C_GRNN + D_MODEL
C_GNSA = C_GATTN + D_MODEL
D_Z = C_GNSA + N_KV * 128

TM = 512
TF = 256
TN_IN = 1280
TN_MM = 512
VMEM_LIMIT = 56 * 2 ** 20


def _cparams(n_axes, vmem=VMEM_LIMIT):
    return pltpu.CompilerParams(dimension_semantics=("arbitrary",) * n_axes, vmem_limit_bytes=vmem)


def _dot(a, b):
    return jnp.dot(a, b, preferred_element_type=F32)


def _dot_nt(a, b):
    return lax.dot_general(a, b, (((1,), (1,)), ((), ())), preferred_element_type=F32)


def _dot_split3(a, b):
    a1 = a.astype(BF16)
    r1 = a - a1.astype(F32)
    a2 = r1.astype(BF16)
    a3 = (r1 - a2.astype(F32)).astype(BF16)
    return _dot(a1, b) + _dot(a2, b) + _dot(a3, b)


def _dot_nt_split3(a, b):
    b1 = b.astype(BF16)
    r1 = b - b1.astype(F32)
    b2 = r1.astype(BF16)
    b3 = (r1 - b2.astype(F32)).astype(BF16)
    return _dot_nt(a, b1) + _dot_nt(a, b2) + _dot_nt(a, b3)


def _kv_head_rows(ref, k):
    n_rows = ref.shape[0]
    return ref.reshape(n_rows * N_KV, HEAD_DIM)[pl.ds(k, n_rows, stride=N_KV), :]


def _rms(x, g):
    return x * lax.rsqrt(jnp.mean(x * x, axis=-1, keepdims=True) + EPS) * g


def _ffn_body(x_ref, ln_ref, wg_ref, wu_ref, wd_ref, lnf_ref, o_ref, xn_ref, *, n_f, final_norm):
    f = pl.program_id(1)

    @pl.when(f == 0)
    def _():
        x = x_ref[...]
        xn_ref[...] = _rms(x, ln_ref[...]).astype(BF16)
        o_ref[...] = 2.0 * x

    xn = xn_ref[...]
    g = _dot(xn, wg_ref[...])
    u = _dot(xn, wu_ref[...])
    h = (g * jax.nn.sigmoid(g) * u).astype(BF16)
    o_ref[...] += _dot(h, wd_ref[...])

    @pl.when(f == n_f - 1)
    def _():
        y = 0.5 * o_ref[...]
        if final_norm:
            y = _rms(y, lnf_ref[...])
        o_ref[...] = y


def _ffn(x, ln, wg, wu, wd, lnf, final_norm):
    m = x.shape[0]
    n_f = D_FF // TF
    return pl.pallas_call(
        functools.partial(_ffn_body, n_f=n_f, final_norm=final_norm),
        grid=(m // TM, n_f),
        in_specs=[
            pl.BlockSpec((TM, D_MODEL), lambda i, f: (i, 0), pipeline_mode=pl.Buffered(1)),
            pl.BlockSpec((1, D_MODEL), lambda i, f: (0, 0)),
            pl.BlockSpec((D_MODEL, TF), lambda i, f: (0, f)),
            pl.BlockSpec((D_MODEL, TF), lambda i, f: (0, f)),
            pl.BlockSpec((TF, D_MODEL), lambda i, f: (f, 0)),
            pl.BlockSpec((1, D_MODEL), lambda i, f: (0, 0)),
        ],
        out_specs=pl.BlockSpec((TM, D_MODEL), lambda i, f: (i, 0)),
        out_shape=jax.ShapeDtypeStruct((m, D_MODEL), F32),
        scratch_shapes=[pltpu.VMEM((TM, D_MODEL), BF16)],
        compiler_params=_cparams(2),
        name="ffn",
    )(x, ln.reshape(1, D_MODEL), wg, wu, wd, lnf.reshape(1, D_MODEL))


def _in_proj_body(x_ref, ln_ref, w_ref, o_ref, xn_ref):
    @pl.when(pl.program_id(1) == 0)
    def _():
        xn_ref[...] = _rms(x_ref[...], ln_ref[...]).astype(BF16)

    o_ref[...] = _dot(xn_ref[...], w_ref[...])


def _in_proj(x, ln, w):
    m = x.shape[0]
    return pl.pallas_call(
        _in_proj_body,
        grid=(m // TM, D_Z // TN_IN),
        in_specs=[
            pl.BlockSpec((TM, D_MODEL), lambda i, j: (i, 0), pipeline_mode=pl.Buffered(1)),
            pl.BlockSpec((1, D_MODEL), lambda i, j: (0, 0)),
            pl.BlockSpec((D_MODEL, TN_IN), lambda i, j: (0, j)),
        ],
        out_specs=pl.BlockSpec((TM, TN_IN), lambda i, j: (i, j)),
        out_shape=jax.ShapeDtypeStruct((m, D_Z), F32),
        scratch_shapes=[pltpu.VMEM((TM, D_MODEL), BF16)],
        compiler_params=_cparams(2),
        name="in_proj",
    )(x, ln.reshape(1, D_MODEL), w)


def _merge_body(gr_ref, oa_ref, wr_ref, wa_ref, zr_ref, za_ref, o_ref):
    y_rnn = _dot(gr_ref[...], wr_ref[...])
    y_attn = _dot(oa_ref[...], wa_ref[...])
    o_ref[...] = (jax.nn.sigmoid(zr_ref[...]) * y_rnn + jax.nn.sigmoid(za_ref[...]) * y_attn).astype(BF16)


def _merge(z, grnn, oattn, w_rnn, w_attn):
    m = z.shape[0]
    cr, ca = C_GRNN // TN_MM, C_GATTN // TN_MM
    return pl.pallas_call(
        _merge_body,
        grid=(m // TM, D_MODEL // TN_MM),
        in_specs=[
            pl.BlockSpec((TM, D_RNN), lambda i, j: (i, 0)),
            pl.BlockSpec((TM, N_HEADS * HEAD_DIM), lambda i, j: (i, 0)),
            pl.BlockSpec((D_RNN, TN_MM), lambda i, j: (0, j)),
            pl.BlockSpec((N_HEADS * HEAD_DIM, TN_MM), lambda i, j: (0, j)),
            pl.BlockSpec((TM, TN_MM), lambda i, j: (i, cr + j)),
            pl.BlockSpec((TM, TN_MM), lambda i, j: (i, ca + j)),
        ],
        out_specs=pl.BlockSpec((TM, TN_MM), lambda i, j: (i, j)),
        out_shape=jax.ShapeDtypeStruct((m, D_MODEL), BF16),
        compiler_params=_cparams(2),
        name="merge",
    )(grnn, oattn, w_rnn, w_attn, z, z)


def _out_proj_body(a_ref, w_ref, x_ref, o_ref):
    o_ref[...] = x_ref[...] + _dot(a_ref[...], w_ref[...])


def _out_proj(x, a, w):
    m = x.shape[0]
    return pl.pallas_call(
        _out_proj_body,
        grid=(m // TM, D_MODEL // TN_MM),
        in_specs=[
            pl.BlockSpec((TM, D_MODEL), lambda i, j: (i, 0)),
            pl.BlockSpec((D_MODEL, TN_MM), lambda i, j: (0, j)),
            pl.BlockSpec((TM, TN_MM), lambda i, j: (i, j)),
        ],
        out_specs=pl.BlockSpec((TM, TN_MM), lambda i, j: (i, j)),
        out_shape=jax.ShapeDtypeStruct((m, D_MODEL), F32),
        compiler_params=_cparams(2),
        name="out_proj",
    )(a, w, x)


def _softplus(v):
    return jnp.maximum(v, 0.0) + jnp.log1p(jnp.exp(-jnp.abs(v)))


def _lru_coeffs(xc, wa_ref, ba, wi_ref, bi, sp, n_blk):
    xb = xc.astype(BF16)
    ra = jnp.concatenate([_dot(xb[:, b * RNN_BW:(b + 1) * RNN_BW], wa_ref[b]) for b in range(n_blk)], axis=1)
    ia = jnp.concatenate([_dot(xb[:, b * RNN_BW:(b + 1) * RNN_BW], wi_ref[b]) for b in range(n_blk)], axis=1)
    r = jax.nn.sigmoid(ra + ba)
    i = jax.nn.sigmoid(ia + bi)
    log_a = -LRU_C * r * sp
    a = jnp.exp(log_a)
    bt = jnp.sqrt(-jnp.tanh(log_a) * (a * a + 1.0)) * (i * xc)
    return a, bt


RNN_TC = 256


def _rnn_prompt_body(ug_ref, ux_ref, cw_ref, cb_ref, wa_ref, ba_ref, wi_ref, bi_ref, lam_ref,
                     g_ref, h_ref, tail_ref, hc_ref):
    c = pl.program_id(1)
    tc = RNN_TC

    @pl.when(c == 0)
    def _():
        tail_ref[...] = jnp.zeros_like(tail_ref)
        hc_ref[...] = jnp.zeros_like(hc_ref)

    u = ux_ref[...]
    tail = tail_ref[...]
    row8 = lax.broadcasted_iota(jnp.int32, (8, D_RNN), 0)
    xc = cb_ref[...] + cw_ref[CONV_W - 1:CONV_W, :] * u
    for j in range(1, CONV_W):
        r = pltpu.roll(u, j, axis=0)
        first = jnp.where(row8 >= j, r[0:8], pltpu.roll(tail, j, axis=0))
        shifted = jnp.concatenate([first, r[8:]], axis=0)
        xc = xc + cw_ref[CONV_W - 1 - j:CONV_W - j, :] * shifted
    tail_ref[...] = u[tc - 8:tc]

    a, bt = _lru_coeffs(xc, wa_ref, ba_ref[...], wi_ref, bi_ref[...], _softplus(-lam_ref[...]), RNN_BLOCKS)

    row = lax.broadcasted_iota(jnp.int32, (tc, D_RNN), 0)
    s = 1
    while s < tc:
        keep = row >= s
        a_sh = jnp.where(keep, pltpu.roll(a, s, axis=0), 1.0)
        b_sh = jnp.where(keep, pltpu.roll(bt, s, axis=0), 0.0)
        bt = a * b_sh + bt
        a = a * a_sh
        s *= 2
    h = bt + a * hc_ref[7:8, :]
    hc_ref[...] = h[tc - 8:tc]
    g_ref[...] = (h * jax.nn.gelu(ug_ref[...])).astype(BF16)

    @pl.when(c == pl.num_programs(1) - 1)
    def _():
        h_ref[0] = h[tc - 8:tc]


def _rnn_prompt(z, cw, cb, wa, ba, wi, bi, lam):
    nc = SEQ // RNN_TC
    vec = lambda: pl.BlockSpec((1, D_RNN), lambda n, c: (0, 0))
    blk = lambda: pl.BlockSpec((RNN_BLOCKS, RNN_BW, RNN_BW), lambda n, c: (0, 0, 0))
    return pl.pallas_call(
        _rnn_prompt_body,
        grid=(BATCH, nc),
        in_specs=[
            pl.BlockSpec((RNN_TC, D_RNN), lambda n, c: (n * nc + c, C_UGATE // D_RNN)),
            pl.BlockSpec((RNN_TC, D_RNN), lambda n, c: (n * nc + c, C_UX // D_RNN)),
            pl.BlockSpec((CONV_W, D_RNN), lambda n, c: (0, 0)),
            vec(), blk(), vec(), blk(), vec(), vec(),
        ],
        out_specs=[
            pl.BlockSpec((RNN_TC, D_RNN), lambda n, c: (n * nc + c, 0)),
            pl.BlockSpec((1, 8, D_RNN), lambda n, c: (n, 0, 0)),
        ],
        out_shape=[jax.ShapeDtypeStruct((M_PROMPT, D_RNN), BF16),
                   jax.ShapeDtypeStruct((BATCH, 8, D_RNN), F32)],
        scratch_shapes=[pltpu.VMEM((8, D_RNN), F32), pltpu.VMEM((8, D_RNN), F32)],
        compiler_params=_cparams(2),
        name="rnn_prompt",
    )(z, z, cw, cb, wa, ba, wi, bi, lam)


RNN_SC = 512


def _rnn_sample_body(ug_ref, ux_ref, buf_ref, h0_ref, cw_ref, cb_ref, wa_ref, ba_ref, wi_ref, bi_ref, lam_ref,
                     g_ref, h_ref):
    full = [buf_ref[j] for j in range(CONV_W - 1)] + [ux_ref[t] for t in range(DEC_SEQ)]
    sp = _softplus(-lam_ref[...])
    h = h0_ref[...]
    for t in range(DEC_SEQ):
        xc = cb_ref[...]
        for k in range(CONV_W):
            xc = xc + full[t + k] * cw_ref[k:k + 1, :]
        a, bt = _lru_coeffs(xc, wa_ref, ba_ref[...], wi_ref, bi_ref[...], sp, RNN_SC // RNN_BW)
        h = a * h + bt
        g_ref[t] = (h * jax.nn.gelu(ug_ref[t])).astype(BF16)
    h_ref[...] = h


def _rnn_sample(ug_t, ux_t, buf_t, h0, cw, cb, wa, ba, wi, bi, lam):
    nb = RNN_SC // RNN_BW
    vec = lambda: pl.BlockSpec((1, RNN_SC), lambda c: (0, c))
    blk = lambda: pl.BlockSpec((nb, RNN_BW, RNN_BW), lambda c: (c, 0, 0))
    return pl.pallas_call(
        _rnn_sample_body,
        grid=(D_RNN // RNN_SC,),
        in_specs=[
            pl.BlockSpec((DEC_SEQ, DEC_BATCH, RNN_SC), lambda c: (0, 0, c)),
            pl.BlockSpec((DEC_SEQ, DEC_BATCH, RNN_SC), lambda c: (0, 0, c)),
            pl.BlockSpec((CONV_W - 1, DEC_BATCH, RNN_SC), lambda c: (0, 0, c)),
            pl.BlockSpec((DEC_BATCH, RNN_SC), lambda c: (0, c)),
            pl.BlockSpec((CONV_W, RNN_SC), lambda c: (0, c)),
            vec(), blk(), vec(), blk(), vec(), vec(),
        ],
        out_specs=[
            pl.BlockSpec((DEC_SEQ, DEC_BATCH, RNN_SC), lambda c: (0, 0, c)),
            pl.BlockSpec((DEC_BATCH, RNN_SC), lambda c: (0, c)),
        ],
        out_shape=[jax.ShapeDtypeStruct((DEC_SEQ, DEC_BATCH, D_RNN), BF16),
                   jax.ShapeDtypeStruct((DEC_BATCH, D_RNN), F32)],
        compiler_params=_cparams(1),
        name="rnn_sample",
    )(ug_t, ux_t, buf_t, h0, cw, cb, wa, ba, wi, bi, lam)


N_SEG = PAGE_SIZE // CMP_STRIDE
SEG_ROWS = N_PAGES * N_SEG
N_PAIR = CMP_STRIDE // 2


CMP_PG = 4


def _compress_body(pt_ref, *refs, paged):
    n_in = CMP_PG * (2 if paged else 1)
    page_refs = refs[:n_in]
    w1_ref, pos_ref, b1_ref, w2_ref, ck_ref, cv_ref, stage_ref, pterm_ref, slab_ref = refs[n_in:]
    n = pl.program_id(0)
    q = pl.program_id(1)

    @pl.when((n == 0) & (q == 0))
    def _():
        for kind in range(2):
            acc = jnp.zeros((8, 2 * CMP_HID), F32)
            for pr in range(N_PAIR):
                acc = acc + _dot(pos_ref[:, pr * 256:(pr + 1) * 256].astype(BF16), w1_ref[kind, pr])
            pterm_ref[kind] = acc

    for j in range(CMP_PG):
        for kind in range(2):
            for k in range(N_KV):
                if paged:
                    slab = _kv_head_rows(page_refs[j * 2 + kind], k)
                else:
                    kk = kind * N_KV + k
                    slab = page_refs[j][:, kk * HEAD_DIM:(kk + 1) * HEAD_DIM]
                slab_ref[(j * 2 + kind) * N_KV + k] = slab

    for j in range(CMP_PG):
        seg0 = (q * CMP_PG + j) * N_SEG
        for kind in range(2):
            for k in range(N_KV):
                for l in range(CMP_STRIDE):
                    piece = slab_ref[(j * 2 + kind) * N_KV + k, pl.ds(l, N_SEG, stride=CMP_STRIDE), :]
                    stage_ref[kind, l // 2, pl.ds(k * SEG_ROWS + seg0, N_SEG),
                              pl.ds((l % 2) * HEAD_DIM, HEAD_DIM)] = piece

    @pl.when(q == N_PAGES // CMP_PG - 1)
    def _():
        for kind, out_ref in ((0, ck_ref), (1, cv_ref)):
            acc = jnp.zeros((N_KV * SEG_ROWS, 2 * CMP_HID), F32)
            for pr in range(N_PAIR):
                acc = acc + _dot(stage_ref[kind, pr].astype(BF16), w1_ref[kind, pr])
            nxt = pltpu.roll(acc[:, CMP_HID:], N_KV * SEG_ROWS - 1, axis=0)
            pt = pterm_ref[kind]
            posterm = pt[0:1, :CMP_HID] + pt[1:2, CMP_HID:] + b1_ref[kind]
            hid = acc[:, :CMP_HID] + nxt + posterm
            out_ref[0] = _dot(jax.nn.gelu(hid).astype(BF16), w2_ref[kind]).astype(BF16)


def _compress(src, pt_flat, col_blk, n_seq, w1, pos, b1, w2, name):
    paged = col_blk is None
    page_of = lambda n, p, pt, j: pt[n * N_PAGES + p * CMP_PG + j]
    if paged:
        page_specs = [pl.BlockSpec((None, None, PAGE_SIZE, None, N_KV, HEAD_DIM),
                                   lambda n, p, pt, j=j, kind=kind: (0, page_of(n, p, pt, j), 0, kind, 0, 0))
                      for j in range(CMP_PG) for kind in range(2)]
    else:
        page_specs = [pl.BlockSpec((None, PAGE_SIZE, 2 * N_KV * HEAD_DIM),
                                   lambda n, p, pt, j=j: (page_of(n, p, pt, j), 0, col_blk))
                      for j in range(CMP_PG)]
    grid_spec = pltpu.PrefetchScalarGridSpec(
        num_scalar_prefetch=1,
        grid=(n_seq, N_PAGES // CMP_PG),
        in_specs=page_specs + [
            pl.BlockSpec((2, N_PAIR, 256, 2 * CMP_HID), lambda n, p, pt: (0, 0, 0, 0)),
            pl.BlockSpec((8, CMP_STRIDE * HEAD_DIM), lambda n, p, pt: (0, 0)),
            pl.BlockSpec((2, 1, CMP_HID), lambda n, p, pt: (0, 0, 0)),
            pl.BlockSpec((2, CMP_HID, HEAD_DIM), lambda n, p, pt: (0, 0, 0)),
        ],
        out_specs=[
            pl.BlockSpec((1, N_KV * SEG_ROWS, HEAD_DIM), lambda n, p, pt: (n, 0, 0)),
            pl.BlockSpec((1, N_KV * SEG_ROWS, HEAD_DIM), lambda n, p, pt: (n, 0, 0)),
        ],
        scratch_shapes=[pltpu.VMEM((2, N_PAIR, N_KV * SEG_ROWS, 256), F32),
                        pltpu.VMEM((2, 8, 2 * CMP_HID), F32),
                        pltpu.VMEM((CMP_PG * 2 * N_KV, PAGE_SIZE, HEAD_DIM), F32)],
    )
    shp = jax.ShapeDtypeStruct((n_seq, N_KV * SEG_ROWS, HEAD_DIM), BF16)
    return pl.pallas_call(
        functools.partial(_compress_body, paged=paged),
        grid_spec=grid_spec,
        out_shape=[shp, shp],
        compiler_params=_cparams(2),
        name=name,
    )(pt_flat, *([src] * len(page_specs)), w1, pos, b1, w2)


def _select_blocks(score, cur, n_blk):
    jj = lax.broadcasted_iota(jnp.int32, score.shape, 1)
    forced = (jj == 0) | (jj == cur) | (jj == cur - 1)
    sc = jnp.where(forced, BIG, jnp.where(jj <= cur, score, NEG))
    rank = jnp.zeros(score.shape, F32)
    for i in range(n_blk):
        si = sc[:, i:i + 1]
        beats = (si > sc) | ((si == sc) & (jj > i))
        rank = rank + jnp.where(beats, 1.0, 0.0)
    sel = (rank < float(min(N_SEL, n_blk))) & (jj <= cur) & (jj < n_blk)
    return jnp.where(sel, 1.0, 0.0).astype(BF16)


def _select_blocks_t(score, cur):
    n_blk = score.shape[0]
    jj = lax.broadcasted_iota(jnp.int32, score.shape, 0)
    forced = (jj == 0) | (jj == cur) | (jj == cur - 1)
    sc = jnp.where(forced, BIG, jnp.where(jj <= cur, score, NEG))
    rank = jnp.zeros(score.shape, F32)
    for i in range(n_blk):
        si = sc[i:i + 1, :]
        beats = (si > sc) | ((si == sc) & (jj > i))
        rank = rank + jnp.where(beats, 1.0, 0.0)
    sel = (rank < float(min(N_SEL, n_blk))) & (jj <= cur)
    return jnp.where(sel, 1.0, 0.0)


def _softmax_rows(logits):
    m = jnp.maximum(jnp.max(logits, axis=-1, keepdims=True), M_FLOOR)
    e = jnp.exp(logits - m)
    s = jnp.sum(e, axis=-1, keepdims=True)
    return e / jnp.where(s > 0.0, s, 1.0)


MASKED_TILE = WINDOW // 128 + 1


def _attn_prompt_body(zq_ref, zg_ref, ck_ref, cv_ref, ks_ref, vs_ref, kw_ref, vw_ref,
                      bc_ref, tz_ref, ovt_ref, e_ref, o_ref,
                      ksb, vsb, kwb, vwb, selm_ref, s_ref, mel_ref, lel_ref, acc_ref):
    qt = pl.program_id(2)
    rows = GROUP * 128

    @pl.when(qt == 0)
    def _():
        ksb[...] = ks_ref[...].astype(BF16)
        vsb[...] = vs_ref[...].astype(BF16)
        kwb[...] = kw_ref[...].astype(BF16)
        vwb[...] = vw_ref[...].astype(BF16)

    q = zq_ref[...] * Q_SCALE
    qq = jnp.concatenate([q[:, g * HEAD_DIM:(g + 1) * HEAD_DIM] for g in range(GROUP)], axis=0).astype(BF16)

    pc = _softmax_rows(_dot_nt(qq, ck_ref[0]) + bc_ref[0].reshape(rows, 128))
    o_c = _dot(pc.astype(BF16), cv_ref[0])

    ps = pc[0:128] + pc[128:256] + pc[256:384] + pc[384:512]
    score_t = _dot_nt_split3(ovt_ref[...], ps)
    n_blk = SEQ // SEL_BLK
    cur = jnp.right_shift(qt * 128 + lax.broadcasted_iota(jnp.int32, (n_blk, 128), 1), 6)
    sel_t = _select_blocks_t(score_t, cur)
    sel = jnp.concatenate([sel_t, jnp.zeros((128 - n_blk, 128), F32)], axis=0).T.astype(BF16)
    for j in range(SEQ // 256):
        selm_ref[j] = (_dot(sel, e_ref[j]) - 1.0) * BIG

    def attend(k_ref, v_ref, lo, tile_of_delta, use_sel):
        hi = jnp.right_shift(qt, 1) + 1
        mel_ref[...] = jnp.full(mel_ref.shape, M_FLOOR, F32)

        def logits_pass(j, carry):
            off = pl.multiple_of(j * 256, 256)
            s = _dot_nt(qq, k_ref[pl.ds(off, 256), :]).reshape(GROUP, 128, 256)
            d0 = qt - 2 * j
            s = s + jnp.concatenate([tz_ref[tile_of_delta(d0)], tz_ref[tile_of_delta(d0 - 1)]], axis=-1)
            if use_sel:
                s = s + selm_ref[j][None]
            s = s.reshape(rows, 256)
            s_ref[j] = s
            mel_ref[...] = jnp.maximum(mel_ref[...], jnp.maximum(s[:, :128], s[:, 128:]))
            return carry

        lax.fori_loop(lo, hi, logits_pass, 0)
        m = jnp.max(mel_ref[...], axis=-1, keepdims=True)
        lel_ref[...] = jnp.zeros_like(lel_ref)
        acc_ref[...] = jnp.zeros_like(acc_ref)

        def value_pass(j, carry):
            off = pl.multiple_of(j * 256, 256)
            pe = jnp.exp(s_ref[j] - m)
            lel_ref[...] += pe[:, :128] + pe[:, 128:]
            acc_ref[...] += _dot(pe.astype(BF16), v_ref[pl.ds(off, 256), :])
            return carry

        lax.fori_loop(lo, hi, value_pass, 0)
        l = jnp.sum(lel_ref[...], axis=-1, keepdims=True)
        return acc_ref[...] / jnp.where(l > 0.0, l, 1.0)

    n_win = WINDOW // 128
    o_s = attend(ksb, vsb, 0, lambda d: jnp.where(d < 0, MASKED_TILE, jnp.minimum(d, 2)), True)
    o_w = attend(kwb, vwb, jnp.right_shift(jnp.maximum(qt - n_win, 0), 1),
                 lambda d: jnp.where((d < 0) | (d > n_win), MASKED_TILE, d), False)

    gates = jax.nn.sigmoid(zg_ref[...])
    outs = []
    for g in range(GROUP):
        r = slice(g * 128, (g + 1) * 128)
        outs.append(gates[:, g:g + 1] * o_c[r] + gates[:, GROUP + g:GROUP + g + 1] * o_s[r]
                    + gates[:, 2 * GROUP + g:2 * GROUP + g + 1] * o_w[r])
    o_ref[...] = jnp.concatenate(outs, axis=1).astype(BF16)


def _attn_prompt(z, ck, cv, bias_cmp, tz, ov, emat):
    nq = SEQ // 128
    kv_col = lambda base, kind: (lambda n, k, t: (n, (base + kind * N_KV * HEAD_DIM) // HEAD_DIM + k))
    kvspec = lambda base, kind: pl.BlockSpec((SEQ, HEAD_DIM), kv_col(base, kind))
    return pl.pallas_call(
        _attn_prompt_body,
        grid=(BATCH, N_KV, nq),
        in_specs=[
            pl.BlockSpec((128, GROUP * HEAD_DIM), lambda n, k, t: (n * nq + t, C_Q // (GROUP * HEAD_DIM) + k)),
            pl.BlockSpec((128, 128), lambda n, k, t: (n * nq + t, C_GNSA // 128 + k)),
            pl.BlockSpec((1, SEG_ROWS, HEAD_DIM), lambda n, k, t: (n * N_KV + k, 0, 0)),
            pl.BlockSpec((1, SEG_ROWS, HEAD_DIM), lambda n, k, t: (n * N_KV + k, 0, 0)),
            kvspec(C_PAG, 2), kvspec(C_PAG, 3), kvspec(C_WIN, 0), kvspec(C_WIN, 1),
            pl.BlockSpec((1, GROUP, 128, 128), lambda n, k, t: (t, k, 0, 0)),
            pl.BlockSpec((MASKED_TILE + 1, GROUP, 128, 128), lambda n, k, t: (0, k, 0, 0)),
            pl.BlockSpec((SEQ // SEL_BLK, 128), lambda n, k, t: (0, 0)),
            pl.BlockSpec((SEQ // 256, 128, 256), lambda n, k, t: (0, 0, 0)),
        ],
        out_specs=pl.BlockSpec((128, GROUP * HEAD_DIM), lambda n, k, t: (n * nq + t, k)),
        out_shape=jax.ShapeDtypeStruct((M_PROMPT, N_HEADS * HEAD_DIM), BF16),
        scratch_shapes=[pltpu.VMEM((SEQ, HEAD_DIM), BF16)] * 4 + [
            pltpu.VMEM((SEQ // 256, 128, 256), F32),
            pltpu.VMEM((SEQ // 256, GROUP * 128, 256), F32),
            pltpu.VMEM((GROUP * 128, 128), F32), pltpu.VMEM((GROUP * 128, 128), F32),
            pltpu.VMEM((GROUP * 128, HEAD_DIM), F32)],
        compiler_params=_cparams(3),
        name="attn_prompt",
    )(z, z, ck, cv, z, z, z, z, bias_cmp, tz, ov, emat)


S_ROWS = GROUP * N_KV * DEC_SEQ


def _attn_sample_body(pt_ref, q_ref, gs_ref, ck_ref, cv_ref, *rest):
    page_refs = rest[:2 * N_PAGES]
    (nkv_ref, kwin_ref, vwin_ref, nwin_ref, bcmp_ref, bsel_ref, bnew_ref, bwin_ref, ov_ref, e_ref,
     o_ref, s_ref, nk_ref, nw_ref) = rest[2 * N_PAGES:]
    kv_of_row = jnp.bitwise_and(jnp.right_shift(lax.broadcasted_iota(jnp.int32, (S_ROWS, 1), 0), 2), N_KV - 1)
    qq = (q_ref[0] * Q_SCALE).astype(BF16)

    def logits(get_k):
        out = None
        for k in range(N_KV):
            s = jnp.where(kv_of_row == k, _dot_nt(qq, get_k(k).astype(BF16)), 0.0)
            out = s if out is None else out + s
        return out

    def weighted(pe, get_v):
        out = None
        for k in range(N_KV):
            o = _dot(jnp.where(kv_of_row == k, pe, 0.0).astype(BF16), get_v(k).astype(BF16))
            out = o if out is None else out + o
        return out

    col = lambda k, half: pl.ds(half * N_KV * HEAD_DIM + k * HEAD_DIM, HEAD_DIM)

    head_rows = _kv_head_rows
    rowmax = lambda s: jnp.max(s, axis=-1, keepdims=True)
    rowsum = lambda s: jnp.sum(s, axis=-1, keepdims=True)

    @pl.when(pl.program_id(0) == 0)
    def _():
        nk_ref[...] = jnp.zeros_like(nk_ref)
        nw_ref[...] = jnp.zeros_like(nw_ref)

    nk_ref[0:8, :] = nkv_ref[0]
    nw_ref[0:8, :] = nwin_ref[0]

    pc = _softmax_rows(logits(lambda k: ck_ref[0, pl.ds(k * SEG_ROWS, SEG_ROWS), :]) + bcmp_ref[...])
    o_c = weighted(pc, lambda k: cv_ref[0, pl.ds(k * SEG_ROWS, SEG_ROWS), :])
    ps = pc + pltpu.roll(pc, 16, axis=0) + pltpu.roll(pc, 32, axis=0) + pltpu.roll(pc, 48, axis=0)
    score = _dot_split3(ps, ov_ref[...])
    n_blk = -(-(PAST_LEN + DEC_SEQ) // SEL_BLK)
    cur = jnp.full((S_ROWS, 128), PAST_LEN // SEL_BLK, jnp.int32)
    sel = _select_blocks(score, cur, n_blk)
    key_mask = (_dot(sel, e_ref[...]) - 1.0) * BIG

    m = jnp.full((S_ROWS, 1), M_FLOOR, F32)
    for p in range(N_PAGES):
        s = (logits(lambda k: head_rows(page_refs[2 * p], k)) + bsel_ref[p]
             + key_mask[:, p * PAGE_SIZE:(p + 1) * PAGE_SIZE])
        s_ref[p] = s
        m = jnp.maximum(m, rowmax(s))
    sn = logits(lambda k: nk_ref[:, col(k, 0)]) + bnew_ref[...]
    m = jnp.maximum(m, rowmax(sn))
    pn = jnp.exp(sn - m)
    l = rowsum(pn)
    acc = weighted(pn, lambda k: nk_ref[:, col(k, 1)])
    for p in range(N_PAGES):
        pe = jnp.exp(s_ref[p] - m)
        l = l + rowsum(pe)
        acc = acc + weighted(pe, lambda k: head_rows(page_refs[2 * p + 1], k))
    o_s = acc / jnp.where(l > 0.0, l, 1.0)

    sw = logits(lambda k: head_rows(kwin_ref, k)) + bwin_ref[...]
    sn = logits(lambda k: nw_ref[:, col(k, 0)]) + bnew_ref[...]
    m = jnp.maximum(jnp.maximum(rowmax(sw), rowmax(sn)), M_FLOOR)
    pw = jnp.exp(sw - m)
    pn = jnp.exp(sn - m)
    l = rowsum(pw) + rowsum(pn)
    o_w = weighted(pw, lambda k: head_rows(vwin_ref, k)) + weighted(pn, lambda k: nw_ref[:, col(k, 1)])
    o_w = o_w / jnp.where(l > 0.0, l, 1.0)

    gates = jax.nn.sigmoid(gs_ref[0])
    o_ref[0] = gates[:, 0:1] * o_c + gates[:, 1:2] * o_s + gates[:, 2:3] * o_w


def _attn_sample(pt_flat, q_s, g_s, ck, cv, cache_kv, nkv, state_win, nwin, bcmp, bsel, bnew, bwin, ov, emat):
    const2 = lambda shape: pl.BlockSpec(shape, lambda n, pt: (0, 0))
    page_specs = [pl.BlockSpec((None, None, PAGE_SIZE, None, N_KV, HEAD_DIM),
                               lambda n, pt, p=p, kind=kind: (0, pt[n * N_PAGES + p], 0, kind, 0, 0))
                  for p in range(N_PAGES) for kind in (2, 3)]
    win_spec = lambda kind: pl.BlockSpec((None, None, WINDOW, None, N_KV, HEAD_DIM),
                                         lambda n, pt: (0, n, 0, kind, 0, 0))
    grid_spec = pltpu.PrefetchScalarGridSpec(
        num_scalar_prefetch=1,
        grid=(DEC_BATCH,),
        in_specs=[
            pl.BlockSpec((1, S_ROWS, HEAD_DIM), lambda n, pt: (n, 0, 0)),
            pl.BlockSpec((1, S_ROWS, 128), lambda n, pt: (n, 0, 0)),
            pl.BlockSpec((1, N_KV * SEG_ROWS, HEAD_DIM), lambda n, pt: (n, 0, 0)),
            pl.BlockSpec((1, N_KV * SEG_ROWS, HEAD_DIM), lambda n, pt: (n, 0, 0)),
        ] + page_specs + [
            pl.BlockSpec((1, 8, 1024), lambda n, pt: (n, 0, 0)),
            win_spec(0), win_spec(1),
            pl.BlockSpec((1, 8, 1024), lambda n, pt: (n, 0, 0)),
            const2((S_ROWS, 128)),
            pl.BlockSpec((N_PAGES, S_ROWS, 128), lambda n, pt: (0, 0, 0)),
            const2((S_ROWS, 128)),
            const2((S_ROWS, WINDOW)),
            const2((128, 128)),
            const2((128, PAST_LEN)),
        ],
        out_specs=pl.BlockSpec((1, S_ROWS, HEAD_DIM), lambda n, pt: (n, 0, 0)),
        scratch_shapes=[
            pltpu.VMEM((N_PAGES, S_ROWS, 128), F32),
            pltpu.VMEM((128, 1024), F32), pltpu.VMEM((128, 1024), F32),
        ],
    )
    return pl.pallas_call(
        _attn_sample_body,
        grid_spec=grid_spec,
        out_shape=jax.ShapeDtypeStruct((DEC_BATCH, S_ROWS, HEAD_DIM), F32),
        compiler_params=_cparams(1),
        name="attn_sample",
    )(pt_flat, q_s, g_s, ck, cv, *([cache_kv] * len(page_specs)), nkv, state_win, state_win, nwin,
      bcmp, bsel, bnew, bwin, ov, emat)


def _t5_bucket(dist):
    d = jnp.maximum(dist, 0)
    df = jnp.maximum(d, 1).astype(F32)
    large = MAX_EXACT + (jnp.log(df / MAX_EXACT) / math.log(MAX_DIST / MAX_EXACT)
                         * (N_BUCKETS - MAX_EXACT)).astype(jnp.int32)
    large = jnp.minimum(large, N_BUCKETS - 1)
    return jnp.where(d < MAX_EXACT, d, large)


def _bias_lookup_body(rb_ref, idx_ref, o_ref):
    idx = idx_ref[0]
    for h in range(N_HEADS):
        acc = jnp.full(idx.shape, NEG, F32)
        for b in range(N_BUCKETS):
            acc = jnp.where(idx == b, rb_ref[b * N_HEADS + h], acc)
        o_ref[0, h] = acc


def _bias_table(rel_bias, dist, valid, name):
    p, r, _ = dist.shape
    idx = jnp.where(jnp.asarray(valid), _t5_bucket(jnp.asarray(dist, jnp.int32)), -1)
    return pl.pallas_call(
        _bias_lookup_body,
        grid=(p,),
        in_specs=[pl.BlockSpec(memory_space=pltpu.SMEM), pl.BlockSpec((1, r, 128), lambda i: (i, 0, 0))],
        out_specs=pl.BlockSpec((1, N_HEADS, r, 128), lambda i: (i, 0, 0, 0)),
        out_shape=jax.ShapeDtypeStruct((p, N_HEADS, r, 128), F32),
        compiler_params=_cparams(1),
        name=name,
    )(rel_bias.astype(F32).reshape(-1), idx)


def _overlap(nc, nb):
    cs = np.arange(nc)[:, None] * CMP_STRIDE
    js = np.arange(nb)[None, :] * SEL_BLK
    ov = np.clip(np.minimum(cs + CMP_LEN, js + SEL_BLK) - np.maximum(cs, js), 0, None) / CMP_LEN
    out = np.zeros((128, 128), np.float32)
    out[:nc, :nb] = ov
    return jnp.asarray(out, BF16)


def _position_tables(rel_bias):
    nc = SEG_ROWS - 1
    t = np.arange(128)[None, :, None]
    c = np.arange(128)[None, None, :]
    cend = c * CMP_STRIDE + CMP_LEN - 1
    d = np.arange(MASKED_TILE + 1)[:, None, None] * 128 + t - c
    tz = _bias_table(rel_bias, d, (d >= 0) & (d < WINDOW), "bias_tiles")
    d = np.arange(SEQ // 128)[:, None, None] * 128 + t - cend
    bias_cmp = _bias_table(rel_bias, d, (d >= 0) & (c < nc), "bias_cmp")
    ts = np.arange(8)[None, :, None]
    qpos = PAST_LEN + ts
    live = ts < DEC_SEQ
    d_cmp = qpos - cend
    d_sel = qpos - (np.arange(N_PAGES)[:, None, None] * PAGE_SIZE + c)
    d_new = ts - c
    d_win = qpos - (PAST_LEN - WINDOW + np.arange(WINDOW // 128)[:, None, None] * 128 + c)
    d = np.concatenate([d_cmp, d_sel, d_new, d_win], axis=0)
    valid = np.concatenate([(d_cmp >= 0) & (c < nc), d_sel >= 0, (d_new >= 0) & (c < DEC_SEQ),
                            (d_win >= 0) & (d_win < WINDOW)], axis=0) & live
    o = _bias_table(rel_bias, d, valid, "bias_sample")[:, :, :DEC_SEQ]
    o = jnp.transpose(o.reshape(-1, N_KV, GROUP, DEC_SEQ, 128), (0, 2, 1, 3, 4)).reshape(-1, S_ROWS, 128)
    bcmp, bsel, bnew = o[0], o[1:1 + N_PAGES], o[1 + N_PAGES]
    bwin = jnp.transpose(o[2 + N_PAGES:], (1, 0, 2)).reshape(S_ROWS, WINDOW)
    keys = np.arange(SEQ)
    e_all = (np.arange(128)[:, None] == (keys // SEL_BLK)[None, :]).astype(np.float32)
    e_tiles = jnp.asarray(e_all.reshape(128, SEQ // 256, 256).transpose(1, 0, 2), BF16)
    ovt_p = jnp.transpose(_overlap(nc, SEQ // SEL_BLK))[:SEQ // SEL_BLK]
    return dict(tz=tz, bias_cmp=bias_cmp, bcmp=bcmp, bsel=bsel, bnew=bnew, bwin=bwin,
                ovt_p=ovt_p, ov_s=_overlap(nc, -(-(PAST_LEN + DEC_SEQ) // SEL_BLK)),
                e_tiles=e_tiles, e_all=jnp.asarray(e_all, BF16))


def _permute_w_in(w_in):
    c0 = 2 * D_RNN + N_HEADS * HEAD_DIM + 6 * N_KV * HEAD_DIM
    g_nsa = w_in[:, c0:c0 + 3 * N_HEADS].reshape(D_MODEL, N_KV, GROUP, 3)
    g_nsa = jnp.transpose(g_nsa, (0, 1, 3, 2)).reshape(D_MODEL, N_KV, 3 * GROUP)
    g_nsa = jnp.pad(g_nsa, ((0, 0), (0, 0), (0, 128 - 3 * GROUP))).reshape(D_MODEL, N_KV * 128)
    return jnp.concatenate([w_in[:, :c0], w_in[:, c0 + 3 * N_HEADS:], g_nsa], axis=1).astype(BF16)


def _cmp_weights(w1_k, w1_v, b1_k, b1_v, w2_k, w2_v, pos):
    def cat(w1):
        w = w1.reshape(2, CMP_STRIDE * HEAD_DIM, CMP_HID)
        return jnp.concatenate([w[0], w[1]], axis=1).reshape(N_PAIR, 256, 2 * CMP_HID)
    w1 = jnp.stack([cat(w1_k), cat(w1_v)]).astype(BF16)
    posm = jnp.pad(pos.reshape(2, CMP_STRIDE * HEAD_DIM), ((0, 6), (0, 0)))
    b1 = jnp.stack([b1_k, b1_v]).reshape(2, 1, CMP_HID)
    w2 = jnp.stack([w2_k, w2_v]).astype(BF16)
    return w1, posm, b1, w2


def kernel(x_prompt, x_sample, cache_kv, page_table, state_win, state_conv, state_h, rel_bias, ln_final, ln_ffn1, w_ffn1_gate, w_ffn1_up, w_ffn1_down, ln_mix, w_in, conv_w, conv_b, rg_wa, rg_ba, rg_wi, rg_bi, rg_lambda, cmp_pos, cmp_k_w1, cmp_k_b1, cmp_k_w2, cmp_v_w1, cmp_v_b1, cmp_v_w2, w_br_rnn, w_br_attn, w_out, ln_ffn2, w_ffn2_gate, w_ffn2_up, w_ffn2_down):
    tabs = _position_tables(rel_bias)
    x = jnp.concatenate([x_prompt.reshape(M_PROMPT, D_MODEL), x_sample.reshape(M_SAMPLE, D_MODEL)], axis=0)

    x = _ffn(x, ln_ffn1[0], w_ffn1_gate[0].astype(BF16), w_ffn1_up[0].astype(BF16), w_ffn1_down[0].astype(BF16),
             ln_final, False)
    z = _in_proj(x, ln_mix[0], _permute_w_in(w_in[0]))
    z_s = z[M_PROMPT:]

    vec = lambda v: v.reshape(1, D_RNN)
    rnn_w = (conv_w[0], vec(conv_b[0]), rg_wa[0].astype(BF16), vec(rg_ba[0]), rg_wi[0].astype(BF16),
             vec(rg_bi[0]), vec(rg_lambda[0]))
    g_p, h_p = _rnn_prompt(z, *rnn_w)
    tmajor = lambda a: jnp.transpose(a.reshape(DEC_BATCH, -1, D_RNN), (1, 0, 2))
    g_s, h_s = _rnn_sample(tmajor(z_s[:, C_UGATE:C_UGATE + D_RNN]), tmajor(z_s[:, C_UX:C_UX + D_RNN]),
                           tmajor(state_conv[0]), state_h[0], *rnn_w)
    grnn = jnp.concatenate([g_p, jnp.transpose(g_s, (1, 0, 2)).reshape(M_SAMPLE, D_RNN)], axis=0)

    cw = _cmp_weights(cmp_k_w1[0], cmp_v_w1[0], cmp_k_b1[0], cmp_v_b1[0], cmp_k_w2[0], cmp_v_w2[0], cmp_pos[0])
    pt_prompt = jnp.arange(BATCH * N_PAGES, dtype=jnp.int32)
    pt_sample = page_table.reshape(-1).astype(jnp.int32)
    ck_p, cv_p = _compress(z.reshape(M_TOK // PAGE_SIZE, PAGE_SIZE, D_Z), pt_prompt, C_PAG // 1024, BATCH, *cw,
                           name="compress_prompt")
    ck_s, cv_s = _compress(cache_kv, pt_sample, None, DEC_BATCH, *cw, name="compress_sample")
    o_p = _attn_prompt(z, ck_p.reshape(BATCH * N_KV, SEG_ROWS, HEAD_DIM), cv_p.reshape(BATCH * N_KV, SEG_ROWS, HEAD_DIM),
                       tabs["bias_cmp"], tabs["tz"], tabs["ovt_p"], tabs["e_tiles"])

    def rows_gkt(a, width):
        a = a.reshape(DEC_BATCH, DEC_SEQ, N_KV, GROUP, width)
        return jnp.transpose(a, (0, 3, 2, 1, 4)).reshape(DEC_BATCH, S_ROWS, width)

    q_s = rows_gkt(z_s[:, C_Q:C_Q + N_HEADS * HEAD_DIM], HEAD_DIM)
    gn = z_s[:, C_GNSA:].reshape(M_SAMPLE, N_KV, 128)[:, :, :3 * GROUP].reshape(M_SAMPLE, N_KV, 3, GROUP)
    g_s3 = jnp.pad(rows_gkt(jnp.transpose(gn, (0, 1, 3, 2)), 3), ((0, 0), (0, 0), (0, 125)))
    pad8 = lambda a: jnp.pad(a.reshape(DEC_BATCH, DEC_SEQ, -1), ((0, 0), (0, 8 - DEC_SEQ), (0, 0)))
    nkv = pad8(z_s[:, C_PAG + 2 * N_KV * HEAD_DIM:C_PAG + 4 * N_KV * HEAD_DIM])
    nwin = pad8(z_s[:, C_WIN:C_WIN + 2 * N_KV * HEAD_DIM])
    o_s = _attn_sample(pt_sample, q_s, g_s3, ck_s, cv_s, cache_kv, nkv, state_win, nwin,
                       tabs["bcmp"], tabs["bsel"], tabs["bnew"], tabs["bwin"], tabs["ov_s"], tabs["e_all"])
    o_s = jnp.transpose(o_s.reshape(DEC_BATCH, GROUP, N_KV, DEC_SEQ, HEAD_DIM), (0, 3, 2, 1, 4))
    oattn = jnp.concatenate([o_p, o_s.reshape(M_SAMPLE, N_HEADS * HEAD_DIM).astype(BF16)], axis=0)

    merged = _merge(z, grnn, oattn, w_br_rnn[0].astype(BF16), w_br_attn[0].astype(BF16))
    x = _out_proj(x, merged, w_out[0].astype(BF16))
    y = _ffn(x, ln_ffn2[0], w_ffn2_gate[0].astype(BF16), w_ffn2_up[0].astype(BF16), w_ffn2_down[0].astype(BF16),
             ln_final, True)

    kv = z[:, C_PAG:C_PAG + 4 * N_KV * HEAD_DIM]
    wn = z[:, C_WIN:C_WIN + 2 * N_KV * HEAD_DIM]
    ux = z[:, C_UX:C_UX + D_RNN]
    win_p = wn[:M_PROMPT].reshape(BATCH, SEQ, 2, N_KV, HEAD_DIM)[:, SEQ - WINDOW:]
    win_s = jnp.concatenate([state_win.reshape(DEC_BATCH, WINDOW, 2, N_KV, HEAD_DIM),
                             wn[M_PROMPT:].reshape(DEC_BATCH, DEC_SEQ, 2, N_KV, HEAD_DIM)], axis=1)
    return (
        y[:M_PROMPT].reshape(BATCH, SEQ, D_MODEL),
        y[M_PROMPT:].reshape(DEC_BATCH, DEC_SEQ, D_MODEL),
        kv[:M_PROMPT].reshape(1, BATCH, SEQ, 4, N_KV, HEAD_DIM),
        kv[M_PROMPT:].reshape(1, DEC_BATCH, DEC_SEQ, 4, N_KV, HEAD_DIM),
        win_p[None],
        win_s[None, :, DEC_SEQ:],
        ux[:M_PROMPT].reshape(BATCH, SEQ, D_RNN)[None, :, SEQ - (CONV_W - 1):],
        ux[M_PROMPT:].reshape(DEC_BATCH, DEC_SEQ, D_RNN)[None, :, DEC_SEQ - (CONV_W - 1):],
        h_p[None, :, 7],
        h_s[None],
    )
```

```python
import functools
import math

import numpy as np
import jax
import jax.numpy as jnp
from jax import lax
from jax.experimental import pallas as pl
from jax.experimental.pallas import tpu as pltpu

F32 = jnp.float32
BF16 = jnp.bfloat16

D_MODEL = 4096
BATCH = 4
SEQ = 2048
DEC_BATCH = 128
DEC_SEQ = 4
PAST_LEN = 2048
PAGE_SIZE = 128
N_PAGES = PAST_LEN // PAGE_SIZE
D_RNN = D_MODEL // 2
RNN_BLOCKS = 16
RNN_BW = D_RNN // RNN_BLOCKS
CONV_W = 4
LRU_C = 8.0
N_HEADS = 16
HEAD_DIM = 128
N_KV = 4
GROUP = N_HEADS // N_KV
CMP_LEN = 32
CMP_STRIDE = 16
CMP_HID = 2 * HEAD_DIM
SEL_BLK = 64
N_SEL = 8
WINDOW = 512
N_BUCKETS = 32
MAX_EXACT = 16
MAX_DIST = 128
D_FF = ((8 * D_MODEL // 3 + 255) // 256) * 256
EPS = 1e-6
NEG = -1e30
BIG = 1e30
M_FLOOR = -1e29
Q_SCALE = HEAD_DIM ** -0.5

M_PROMPT = BATCH * SEQ
M_SAMPLE = DEC_BATCH * DEC_SEQ
M_TOK = M_PROMPT + M_SAMPLE

C_UGATE = 0
C_UX = C_UGATE + D_RNN
C_Q = C_UX + D_RNN
C_PAG = C_Q + N_HEADS * HEAD_DIM
C_WIN = C_PAG + 4 * N_KV * HEAD_DIM
C_GRNN = C_WIN + 2 * N_KV * HEAD_DIM
C_GATTN = C_GRNN + D_MODEL
C_GNSA = C_GATTN + D_MODEL
D_Z = C_GNSA + N_KV * 128

TM = 512
TF = 256
TN_IN = 1280
TN_MM = 1024
VMEM_LIMIT = 56 * 2 ** 20


def _cparams(n_axes, vmem=VMEM_LIMIT):
    return pltpu.CompilerParams(dimension_semantics=("arbitrary",) * n_axes, vmem_limit_bytes=vmem)


def _dot(a, b):
    return jnp.dot(a, b, preferred_element_type=F32)


def _dot_nt(a, b):
    return lax.dot_general(a, b, (((1,), (1,)), ((), ())), preferred_element_type=F32)


def _dot_split3(a, b):
    a1 = a.astype(BF16)
    r1 = a - a1.astype(F32)
    a2 = r1.astype(BF16)
    a3 = (r1 - a2.astype(F32)).astype(BF16)
    return _dot(a1, b) + _dot(a2, b) + _dot(a3, b)


def _dot_nt_split3(a, b):
    b1 = b.astype(BF16)
    r1 = b - b1.astype(F32)
    b2 = r1.astype(BF16)
    b3 = (r1 - b2.astype(F32)).astype(BF16)
    return _dot_nt(a, b1) + _dot_nt(a, b2) + _dot_nt(a, b3)


def _kv_head_rows(ref, k):
    n_rows = ref.shape[0]
    return ref.reshape(n_rows * N_KV, HEAD_DIM)[pl.ds(k, n_rows, stride=N_KV), :]


def _rms(x, g):
    return x * lax.rsqrt(jnp.mean(x * x, axis=-1, keepdims=True) + EPS) * g


def _ffn_body(x_ref, ln_ref, wg_ref, wu_ref, wd_ref, lnf_ref, o_ref, xn_ref, *, n_f, final_norm):
    f = pl.program_id(1)

    @pl.when(f == 0)
    def _():
        x = x_ref[...]
        xn_ref[...] = _rms(x, ln_ref[...]).astype(BF16)
        o_ref[...] = 2.0 * x

    xn = xn_ref[...]
    g = _dot(xn, wg_ref[...])
    u = _dot(xn, wu_ref[...])
    h = (g * jax.nn.sigmoid(g) * u).astype(BF16)
    o_ref[...] += _dot(h, wd_ref[...])

    @pl.when(f == n_f - 1)
    def _():
        y = 0.5 * o_ref[...]
        if final_norm:
            y = _rms(y, lnf_ref[...])
        o_ref[...] = y


def _ffn(x, ln, wg, wu, wd, lnf, final_norm):
    m = x.shape[0]
    n_f = D_FF // TF
    return pl.pallas_call(
        functools.partial(_ffn_body, n_f=n_f, final_norm=final_norm),
        grid=(m // TM, n_f),
        in_specs=[
            pl.BlockSpec((TM, D_MODEL), lambda i, f: (i, 0), pipeline_mode=pl.Buffered(1)),
            pl.BlockSpec((1, D_MODEL), lambda i, f: (0, 0)),
            pl.BlockSpec((D_MODEL, TF), lambda i, f: (0, f)),
            pl.BlockSpec((D_MODEL, TF), lambda i, f: (0, f)),
            pl.BlockSpec((TF, D_MODEL), lambda i, f: (f, 0)),
            pl.BlockSpec((1, D_MODEL), lambda i, f: (0, 0)),
        ],
        out_specs=pl.BlockSpec((TM, D_MODEL), lambda i, f: (i, 0)),
        out_shape=jax.ShapeDtypeStruct((m, D_MODEL), F32),
        scratch_shapes=[pltpu.VMEM((TM, D_MODEL), BF16)],
        compiler_params=_cparams(2),
        name="ffn",
    )(x, ln.reshape(1, D_MODEL), wg, wu, wd, lnf.reshape(1, D_MODEL))


def _in_proj_body(x_ref, ln_ref, w_ref, o_ref, xn_ref):
    @pl.when(pl.program_id(1) == 0)
    def _():
        xn_ref[...] = _rms(x_ref[...], ln_ref[...]).astype(BF16)

    o_ref[...] = _dot(xn_ref[...], w_ref[...])


def _in_proj(x, ln, w):
    m = x.shape[0]
    return pl.pallas_call(
        _in_proj_body,
        grid=(m // TM, D_Z // TN_IN),
        in_specs=[
            pl.BlockSpec((TM, D_MODEL), lambda i, j: (i, 0), pipeline_mode=pl.Buffered(1)),
            pl.BlockSpec((1, D_MODEL), lambda i, j: (0, 0)),
            pl.BlockSpec((D_MODEL, TN_IN), lambda i, j: (0, j)),
        ],
        out_specs=pl.BlockSpec((TM, TN_IN), lambda i, j: (i, j)),
        out_shape=jax.ShapeDtypeStruct((m, D_Z), F32),
        scratch_shapes=[pltpu.VMEM((TM, D_MODEL), BF16)],
        compiler_params=_cparams(2),
        name="in_proj",
    )(x, ln.reshape(1, D_MODEL), w)


def _merge_body(gr_ref, oa_ref, wr_ref, wa_ref, zr_ref, za_ref, o_ref):
    y_rnn = _dot(gr_ref[...], wr_ref[...])
    y_attn = _dot(oa_ref[...], wa_ref[...])
    o_ref[...] = (jax.nn.sigmoid(zr_ref[...]) * y_rnn + jax.nn.sigmoid(za_ref[...]) * y_attn).astype(BF16)


def _merge(z, grnn, oattn, w_rnn, w_attn):
    m = z.shape[0]
    cr, ca = C_GRNN // TN_MM, C_GATTN // TN_MM
    return pl.pallas_call(
        _merge_body,
        grid=(m // TM, D_MODEL // TN_MM),
        in_specs=[
            pl.BlockSpec((TM, D_RNN), lambda i, j: (i, 0)),
            pl.BlockSpec((TM, N_HEADS * HEAD_DIM), lambda i, j: (i, 0)),
            pl.BlockSpec((D_RNN, TN_MM), lambda i, j: (0, j)),
            pl.BlockSpec((N_HEADS * HEAD_DIM, TN_MM), lambda i, j: (0, j)),
            pl.BlockSpec((TM, TN_MM), lambda i, j: (i, cr + j)),
            pl.BlockSpec((TM, TN_MM), lambda i, j: (i, ca + j)),
        ],
        out_specs=pl.BlockSpec((TM, TN_MM), lambda i, j: (i, j)),
        out_shape=jax.ShapeDtypeStruct((m, D_MODEL), BF16),
        compiler_params=_cparams(2),
        name="merge",
    )(grnn, oattn, w_rnn, w_attn, z, z)


def _out_proj_body(a_ref, w_ref, x_ref, o_ref):
    o_ref[...] = x_ref[...] + _dot(a_ref[...], w_ref[...])


def _out_proj(x, a, w):
    m = x.shape[0]
    return pl.pallas_call(
        _out_proj_body,
        grid=(m // TM, D_MODEL // TN_MM),
        in_specs=[
            pl.BlockSpec((TM, D_MODEL), lambda i, j: (i, 0)),
            pl.BlockSpec((D_MODEL, TN_MM), lambda i, j: (0, j)),
            pl.BlockSpec((TM, TN_MM), lambda i, j: (i, j)),
        ],
        out_specs=pl.BlockSpec((TM, TN_MM), lambda i, j: (i, j)),
        out_shape=jax.ShapeDtypeStruct((m, D_MODEL), F32),
        compiler_params=_cparams(2),
        name="out_proj",
    )(a, w, x)


def _softplus(v):
    return jnp.maximum(v, 0.0) + jnp.log1p(jnp.exp(-jnp.abs(v)))


def _lru_coeffs(xc, wa_ref, ba, wi_ref, bi, sp, n_blk):
    xb = xc.astype(BF16)
    ra = jnp.concatenate([_dot(xb[:, b * RNN_BW:(b + 1) * RNN_BW], wa_ref[b]) for b in range(n_blk)], axis=1)
    ia = jnp.concatenate([_dot(xb[:, b * RNN_BW:(b + 1) * RNN_BW], wi_ref[b]) for b in range(n_blk)], axis=1)
    r = jax.nn.sigmoid(ra + ba)
    i = jax.nn.sigmoid(ia + bi)
    log_a = -LRU_C * r * sp
    a = jnp.exp(log_a)
    bt = jnp.sqrt(-jnp.tanh(log_a) * (a * a + 1.0)) * (i * xc)
    return a, bt


RNN_TC = 256


def _rnn_prompt_body(ug_ref, ux_ref, cw_ref, cb_ref, wa_ref, ba_ref, wi_ref, bi_ref, lam_ref,
                     g_ref, h_ref, tail_ref, hc_ref):
    c = pl.program_id(1)
    tc = RNN_TC

    @pl.when(c == 0)
    def _():
        tail_ref[...] = jnp.zeros_like(tail_ref)
        hc_ref[...] = jnp.zeros_like(hc_ref)

    u = ux_ref[...]
    tail = tail_ref[...]
    row8 = lax.broadcasted_iota(jnp.int32, (8, D_RNN), 0)
    xc = cb_ref[...] + cw_ref[CONV_W - 1:CONV_W, :] * u
    for j in range(1, CONV_W):
        r = pltpu.roll(u, j, axis=0)
        first = jnp.where(row8 >= j, r[0:8], pltpu.roll(tail, j, axis=0))
        shifted = jnp.concatenate([first, r[8:]], axis=0)
        xc = xc + cw_ref[CONV_W - 1 - j:CONV_W - j, :] * shifted
    tail_ref[...] = u[tc - 8:tc]

    a, bt = _lru_coeffs(xc, wa_ref, ba_ref[...], wi_ref, bi_ref[...], _softplus(-lam_ref[...]), RNN_BLOCKS)

    row = lax.broadcasted_iota(jnp.int32, (tc, D_RNN), 0)
    s = 1
    while s < tc:
        keep = row >= s
        a_sh = jnp.where(keep, pltpu.roll(a, s, axis=0), 1.0)
        b_sh = jnp.where(keep, pltpu.roll(bt, s, axis=0), 0.0)
        bt = a * b_sh + bt
        a = a * a_sh
        s *= 2
    h = bt + a * hc_ref[7:8, :]
    hc_ref[...] = h[tc - 8:tc]
    g_ref[...] = (h * jax.nn.gelu(ug_ref[...])).astype(BF16)

    @pl.when(c == pl.num_programs(1) - 1)
    def _():
        h_ref[0] = h[tc - 8:tc]


def _rnn_prompt(z, cw, cb, wa, ba, wi, bi, lam):
    nc = SEQ // RNN_TC
    vec = lambda: pl.BlockSpec((1, D_RNN), lambda n, c: (0, 0))
    blk = lambda: pl.BlockSpec((RNN_BLOCKS, RNN_BW, RNN_BW), lambda n, c: (0, 0, 0))
    return pl.pallas_call(
        _rnn_prompt_body,
        grid=(BATCH, nc),
        in_specs=[
            pl.BlockSpec((RNN_TC, D_RNN), lambda n, c: (n * nc + c, C_UGATE // D_RNN)),
            pl.BlockSpec((RNN_TC, D_RNN), lambda n, c: (n * nc + c, C_UX // D_RNN)),
            pl.BlockSpec((CONV_W, D_RNN), lambda n, c: (0, 0)),
            vec(), blk(), vec(), blk(), vec(), vec(),
        ],
        out_specs=[
            pl.BlockSpec((RNN_TC, D_RNN), lambda n, c: (n * nc + c, 0)),
            pl.BlockSpec((1, 8, D_RNN), lambda n, c: (n, 0, 0)),
        ],
        out_shape=[jax.ShapeDtypeStruct((M_PROMPT, D_RNN), BF16),
                   jax.ShapeDtypeStruct((BATCH, 8, D_RNN), F32)],
        scratch_shapes=[pltpu.VMEM((8, D_RNN), F32), pltpu.VMEM((8, D_RNN), F32)],
        compiler_params=_cparams(2),
        name="rnn_prompt",
    )(z, z, cw, cb, wa, ba, wi, bi, lam)


RNN_SC = 512


def _rnn_sample_body(ug_ref, ux_ref, buf_ref, h0_ref, cw_ref, cb_ref, wa_ref, ba_ref, wi_ref, bi_ref, lam_ref,
                     g_ref, h_ref):
    full = [buf_ref[j] for j in range(CONV_W - 1)] + [ux_ref[t] for t in range(DEC_SEQ)]
    sp = _softplus(-lam_ref[...])
    h = h0_ref[...]
    for t in range(DEC_SEQ):
        xc = cb_ref[...]
        for k in range(CONV_W):
            xc = xc + full[t + k] * cw_ref[k:k + 1, :]
        a, bt = _lru_coeffs(xc, wa_ref, ba_ref[...], wi_ref, bi_ref[...], sp, RNN_SC // RNN_BW)
        h = a * h + bt
        g_ref[t] = (h * jax.nn.gelu(ug_ref[t])).astype(BF16)
    h_ref[...] = h


def _rnn_sample(ug_t, ux_t, buf_t, h0, cw, cb, wa, ba, wi, bi, lam):
    nb = RNN_SC // RNN_BW
    vec = lambda: pl.BlockSpec((1, RNN_SC), lambda c: (0, c))
    blk = lambda: pl.BlockSpec((nb, RNN_BW, RNN_BW), lambda c: (c, 0, 0))
    return pl.pallas_call(
        _rnn_sample_body,
        grid=(D_RNN // RNN_SC,),
        in_specs=[
            pl.BlockSpec((DEC_SEQ, DEC_BATCH, RNN_SC), lambda c: (0, 0, c)),
            pl.BlockSpec((DEC_SEQ, DEC_BATCH, RNN_SC), lambda c: (0, 0, c)),
            pl.BlockSpec((CONV_W - 1, DEC_BATCH, RNN_SC), lambda c: (0, 0, c)),
            pl.BlockSpec((DEC_BATCH, RNN_SC), lambda c: (0, c)),
            pl.BlockSpec((CONV_W, RNN_SC), lambda c: (0, c)),
            vec(), blk(), vec(), blk(), vec(), vec(),
        ],
        out_specs=[
            pl.BlockSpec((DEC_SEQ, DEC_BATCH, RNN_SC), lambda c: (0, 0, c)),
            pl.BlockSpec((DEC_BATCH, RNN_SC), lambda c: (0, c)),
        ],
        out_shape=[jax.ShapeDtypeStruct((DEC_SEQ, DEC_BATCH, D_RNN), BF16),
                   jax.ShapeDtypeStruct((DEC_BATCH, D_RNN), F32)],
        compiler_params=_cparams(1),
        name="rnn_sample",
    )(ug_t, ux_t, buf_t, h0, cw, cb, wa, ba, wi, bi, lam)


N_SEG = PAGE_SIZE // CMP_STRIDE
SEG_ROWS = N_PAGES * N_SEG
N_PAIR = CMP_STRIDE // 2


CMP_PG = 4


def _compress_body(pt_ref, *refs, paged):
    n_in = CMP_PG * (2 if paged else 1)
    page_refs = refs[:n_in]
    w1_ref, pos_ref, b1_ref, w2_ref, ck_ref, cv_ref, stage_ref, pterm_ref, slab_ref = refs[n_in:]
    n = pl.program_id(0)
    q = pl.program_id(1)

    @pl.when((n == 0) & (q == 0))
    def _():
        for kind in range(2):
            acc = jnp.zeros((8, 2 * CMP_HID), F32)
            for pr in range(N_PAIR):
                acc = acc + _dot(pos_ref[:, pr * 256:(pr + 1) * 256].astype(BF16), w1_ref[kind, pr])
            pterm_ref[kind] = acc

    for j in range(CMP_PG):
        for kind in range(2):
            for k in range(N_KV):
                if paged:
                    slab = _kv_head_rows(page_refs[j * 2 + kind], k)
                else:
                    kk = kind * N_KV + k
                    slab = page_refs[j][:, kk * HEAD_DIM:(kk + 1) * HEAD_DIM]
                slab_ref[(j * 2 + kind) * N_KV + k] = slab

    for j in range(CMP_PG):
        seg0 = (q * CMP_PG + j) * N_SEG
        for kind in range(2):
            for k in range(N_KV):
                for l in range(CMP_STRIDE):
                    piece = slab_ref[(j * 2 + kind) * N_KV + k, pl.ds(l, N_SEG, stride=CMP_STRIDE), :]
                    stage_ref[kind, l // 2, pl.ds(k * SEG_ROWS + seg0, N_SEG),
                              pl.ds((l % 2) * HEAD_DIM, HEAD_DIM)] = piece

    @pl.when(q == N_PAGES // CMP_PG - 1)
    def _():
        for kind, out_ref in ((0, ck_ref), (1, cv_ref)):
            acc = jnp.zeros((N_KV * SEG_ROWS, 2 * CMP_HID), F32)
            for pr in range(N_PAIR):
                acc = acc + _dot(stage_ref[kind, pr].astype(BF16), w1_ref[kind, pr])
            nxt = pltpu.roll(acc[:, CMP_HID:], N_KV * SEG_ROWS - 1, axis=0)
            pt = pterm_ref[kind]
            posterm = pt[0:1, :CMP_HID] + pt[1:2, CMP_HID:] + b1_ref[kind]
            hid = acc[:, :CMP_HID] + nxt + posterm
            out_ref[0] = _dot(jax.nn.gelu(hid).astype(BF16), w2_ref[kind]).astype(BF16)


def _compress(src, pt_flat, col_blk, n_seq, w1, pos, b1, w2, name):
    paged = col_blk is None
    page_of = lambda n, p, pt, j: pt[n * N_PAGES + p * CMP_PG + j]
    if paged:
        page_specs = [pl.BlockSpec((None, None, PAGE_SIZE, None, N_KV, HEAD_DIM),
                                   lambda n, p, pt, j=j, kind=kind: (0, page_of(n, p, pt, j), 0, kind, 0, 0))
                      for j in range(CMP_PG) for kind in range(2)]
    else:
        page_specs = [pl.BlockSpec((None, PAGE_SIZE, 2 * N_KV * HEAD_DIM),
                                   lambda n, p, pt, j=j: (page_of(n, p, pt, j), 0, col_blk))
                      for j in range(CMP_PG)]
    grid_spec = pltpu.PrefetchScalarGridSpec(
        num_scalar_prefetch=1,
        grid=(n_seq, N_PAGES // CMP_PG),
        in_specs=page_specs + [
            pl.BlockSpec((2, N_PAIR, 256, 2 * CMP_HID), lambda n, p, pt: (0, 0, 0, 0)),
            pl.BlockSpec((8, CMP_STRIDE * HEAD_DIM), lambda n, p, pt: (0, 0)),
            pl.BlockSpec((2, 1, CMP_HID), lambda n, p, pt: (0, 0, 0)),
            pl.BlockSpec((2, CMP_HID, HEAD_DIM), lambda n, p, pt: (0, 0, 0)),
        ],
        out_specs=[
            pl.BlockSpec((1, N_KV * SEG_ROWS, HEAD_DIM), lambda n, p, pt: (n, 0, 0)),
            pl.BlockSpec((1, N_KV * SEG_ROWS, HEAD_DIM), lambda n, p, pt: (n, 0, 0)),
        ],
        scratch_shapes=[pltpu.VMEM((2, N_PAIR, N_KV * SEG_ROWS, 256), F32),
                        pltpu.VMEM((2, 8, 2 * CMP_HID), F32),
                        pltpu.VMEM((CMP_PG * 2 * N_KV, PAGE_SIZE, HEAD_DIM), F32)],
    )
    shp = jax.ShapeDtypeStruct((n_seq, N_KV * SEG_ROWS, HEAD_DIM), BF16)
    return pl.pallas_call(
        functools.partial(_compress_body, paged=paged),
        grid_spec=grid_spec,
        out_shape=[shp, shp],
        compiler_params=_cparams(2),
        name=name,
    )(pt_flat, *([src] * len(page_specs)), w1, pos, b1, w2)


def _select_blocks(score, cur, n_blk):
    jj = lax.broadcasted_iota(jnp.int32, score.shape, 1)
    forced = (jj == 0) | (jj == cur) | (jj == cur - 1)
    sc = jnp.where(forced, BIG, jnp.where(jj <= cur, score, NEG))
    rank = jnp.zeros(score.shape, F32)
    for i in range(n_blk):
        si = sc[:, i:i + 1]
        beats = (si > sc) | ((si == sc) & (jj > i))
        rank = rank + jnp.where(beats, 1.0, 0.0)
    sel = (rank < float(min(N_SEL, n_blk))) & (jj <= cur) & (jj < n_blk)
    return jnp.where(sel, 1.0, 0.0).astype(BF16)


def _select_blocks_t(score, cur):
    n_blk = score.shape[0]
    jj = lax.broadcasted_iota(jnp.int32, score.shape, 0)
    forced = (jj == 0) | (jj == cur) | (jj == cur - 1)
    sc = jnp.where(forced, BIG, jnp.where(jj <= cur, score, NEG))
    rank = jnp.zeros(score.shape, F32)
    for i in range(n_blk):
        si = sc[i:i + 1, :]
        beats = (si > sc) | ((si == sc) & (jj > i))
        rank = rank + jnp.where(beats, 1.0, 0.0)
    sel = (rank < float(min(N_SEL, n_blk))) & (jj <= cur)
    return jnp.where(sel, 1.0, 0.0)


def _softmax_rows(logits):
    m = jnp.maximum(jnp.max(logits, axis=-1, keepdims=True), M_FLOOR)
    e = jnp.exp(logits - m)
    s = jnp.sum(e, axis=-1, keepdims=True)
    return e / jnp.where(s > 0.0, s, 1.0)


MASKED_TILE = WINDOW // 128 + 1
ATT_HP = 2


def _attn_prompt_body(zq_ref, zg_ref, ck_ref, cv_ref, ks_ref, vs_ref, kw_ref, vw_ref,
                      bc_ref, tz_ref, ovt_ref, e_ref, o_ref,
                      ksb, vsb, kwb, vwb, selm_ref, s_ref, mel_ref, lel_ref, acc_ref):
    qt = pl.program_id(2)
    rows = GROUP * 128

    @pl.when(qt == 0)
    def _():
        ksb[...] = ks_ref[...].astype(BF16)
        vsb[...] = vs_ref[...].astype(BF16)
        kwb[...] = kw_ref[...].astype(BF16)
        vwb[...] = vw_ref[...].astype(BF16)

    heads = range(ATT_HP)
    lanes = lambda h: slice(h * HEAD_DIM, (h + 1) * HEAD_DIM)
    groups = lambda h: slice(h * GROUP, (h + 1) * GROUP)
    n_blk = SEQ // SEL_BLK
    cur = jnp.right_shift(qt * 128 + lax.broadcasted_iota(jnp.int32, (n_blk, 128), 1), 6)
    qq, o_c = [], []
    for h in heads:
        q = zq_ref[:, h * GROUP * HEAD_DIM:(h + 1) * GROUP * HEAD_DIM] * Q_SCALE
        qh = jnp.concatenate([q[:, g * HEAD_DIM:(g + 1) * HEAD_DIM] for g in range(GROUP)], axis=0).astype(BF16)
        qq.append(qh)
        pc = _softmax_rows(_dot_nt(qh, ck_ref[h]) + bc_ref[0, groups(h)].reshape(rows, 128))
        o_c.append(_dot(pc.astype(BF16), cv_ref[h]))
        ps = pc[0:128] + pc[128:256] + pc[256:384] + pc[384:512]
        sel_t = _select_blocks_t(_dot_nt_split3(ovt_ref[...], ps), cur)
        sel = jnp.concatenate([sel_t, jnp.zeros((128 - n_blk, 128), F32)], axis=0).T.astype(BF16)
        for j in range(SEQ // 256):
            selm_ref[h, j] = (_dot(sel, e_ref[j]) - 1.0) * BIG

    def attend(k_ref, v_ref, lo, tile_of_delta, use_sel):
        hi = jnp.right_shift(qt, 1) + 1
        mel_ref[...] = jnp.full(mel_ref.shape, M_FLOOR, F32)

        def logits_pass(j, carry):
            off = pl.multiple_of(j * 256, 256)
            d0 = qt - 2 * j
            i0, i1 = tile_of_delta(d0), tile_of_delta(d0 - 1)
            for h in heads:
                s = _dot_nt(qq[h], k_ref[pl.ds(off, 256), lanes(h)]).reshape(GROUP, 128, 256)
                s = s + jnp.concatenate([tz_ref[i0, groups(h)], tz_ref[i1, groups(h)]], axis=-1)
                if use_sel:
                    s = s + selm_ref[h, j][None]
                s = s.reshape(rows, 256)
                s_ref[h, j] = s
                mel_ref[h] = jnp.maximum(mel_ref[h], jnp.maximum(s[:, :128], s[:, 128:]))
            return carry

        lax.fori_loop(lo, hi, logits_pass, 0)
        m = [jnp.max(mel_ref[h], axis=-1, keepdims=True) for h in heads]
        lel_ref[...] = jnp.zeros_like(lel_ref)
        acc_ref[...] = jnp.zeros_like(acc_ref)

        def value_pass(j, carry):
            off = pl.multiple_of(j * 256, 256)
            for h in heads:
                pe = jnp.exp(s_ref[h, j] - m[h])
                lel_ref[h] += pe[:, :128] + pe[:, 128:]
                acc_ref[h] += _dot(pe.astype(BF16), v_ref[pl.ds(off, 256), lanes(h)])
            return carry

        lax.fori_loop(lo, hi, value_pass, 0)
        outs = []
        for h in heads:
            l = jnp.sum(lel_ref[h], axis=-1, keepdims=True)
            outs.append(acc_ref[h] / jnp.where(l > 0.0, l, 1.0))
        return outs

    n_win = WINDOW // 128
    o_s = attend(ksb, vsb, 0, lambda d: jnp.where(d < 0, MASKED_TILE, jnp.minimum(d, 2)), True)
    o_w = attend(kwb, vwb, jnp.right_shift(jnp.maximum(qt - n_win, 0), 1),
                 lambda d: jnp.where((d < 0) | (d > n_win), MASKED_TILE, d), False)

    gates = jax.nn.sigmoid(zg_ref[...])
    outs = []
    for h in heads:
        for g in range(GROUP):
            r = slice(g * 128, (g + 1) * 128)
            gate = lambda branch: gates[:, h * 128 + branch * GROUP + g:h * 128 + branch * GROUP + g + 1]
            outs.append(gate(0) * o_c[h][r] + gate(1) * o_s[h][r] + gate(2) * o_w[h][r])
    o_ref[...] = jnp.concatenate(outs, axis=1).astype(BF16)


def _attn_prompt(z, ck, cv, bias_cmp, tz, ov, emat):
    nq = SEQ // 128
    nhp = N_KV // ATT_HP
    kw = ATT_HP * HEAD_DIM
    qw = ATT_HP * GROUP * HEAD_DIM
    kv_col = lambda base, kind: (lambda n, k, t: (n, (base + kind * N_KV * HEAD_DIM) // kw + k))
    kvspec = lambda base, kind: pl.BlockSpec((SEQ, kw), kv_col(base, kind))
    rows = GROUP * 128
    return pl.pallas_call(
        _attn_prompt_body,
        grid=(BATCH, nhp, nq),
        in_specs=[
            pl.BlockSpec((128, qw), lambda n, k, t: (n * nq + t, C_Q // qw + k)),
            pl.BlockSpec((128, ATT_HP * 128), lambda n, k, t: (n * nq + t, C_GNSA // (ATT_HP * 128) + k)),
            pl.BlockSpec((ATT_HP, SEG_ROWS, HEAD_DIM), lambda n, k, t: (n * nhp + k, 0, 0)),
            pl.BlockSpec((ATT_HP, SEG_ROWS, HEAD_DIM), lambda n, k, t: (n * nhp + k, 0, 0)),
            kvspec(C_PAG, 2), kvspec(C_PAG, 3), kvspec(C_WIN, 0), kvspec(C_WIN, 1),
            pl.BlockSpec((1, ATT_HP * GROUP, 128, 128), lambda n, k, t: (t, k, 0, 0)),
            pl.BlockSpec((MASKED_TILE + 1, ATT_HP * GROUP, 128, 128), lambda n, k, t: (0, k, 0, 0)),
            pl.BlockSpec((SEQ // SEL_BLK, 128), lambda n, k, t: (0, 0)),
            pl.BlockSpec((SEQ // 256, 128, 256), lambda n, k, t: (0, 0, 0)),
        ],
        out_specs=pl.BlockSpec((128, qw), lambda n, k, t: (n * nq + t, k)),
        out_shape=jax.ShapeDtypeStruct((M_PROMPT, N_HEADS * HEAD_DIM), BF16),
        scratch_shapes=[pltpu.VMEM((SEQ, kw), BF16)] * 4 + [
            pltpu.VMEM((ATT_HP, SEQ // 256, 128, 256), F32),
            pltpu.VMEM((ATT_HP, SEQ // 256, rows, 256), F32),
            pltpu.VMEM((ATT_HP, rows, 128), F32), pltpu.VMEM((ATT_HP, rows, 128), F32),
            pltpu.VMEM((ATT_HP, rows, HEAD_DIM), F32)],
        compiler_params=_cparams(3),
        name="attn_prompt",
    )(z, z, ck, cv, z, z, z, z, bias_cmp, tz, ov, emat)


S_ROWS = GROUP * N_KV * DEC_SEQ


def _attn_sample_body(pt_ref, q_ref, gs_ref, ck_ref, cv_ref, *rest):
    page_refs = rest[:2 * N_PAGES]
    (nkv_ref, kwin_ref, vwin_ref, nwin_ref, bcmp_ref, bsel_ref, bnew_ref, bwin_ref, ov_ref, e_ref,
     o_ref, s_ref, nk_ref, nw_ref) = rest[2 * N_PAGES:]
    kv_of_row = jnp.bitwise_and(jnp.right_shift(lax.broadcasted_iota(jnp.int32, (S_ROWS, 1), 0), 2), N_KV - 1)
    qq = (q_ref[0] * Q_SCALE).astype(BF16)

    def logits(get_k):
        out = None
        for k in range(N_KV):
            s = jnp.where(kv_of_row == k, _dot_nt(qq, get_k(k).astype(BF16)), 0.0)
            out = s if out is None else out + s
        return out

    def weighted(pe, get_v):
        out = None
        for k in range(N_KV):
            o = _dot(jnp.where(kv_of_row == k, pe, 0.0).astype(BF16), get_v(k).astype(BF16))
            out = o if out is None else out + o
        return out

    col = lambda k, half: pl.ds(half * N_KV * HEAD_DIM + k * HEAD_DIM, HEAD_DIM)

    head_rows = _kv_head_rows
    rowmax = lambda s: jnp.max(s, axis=-1, keepdims=True)
    rowsum = lambda s: jnp.sum(s, axis=-1, keepdims=True)

    @pl.when(pl.program_id(0) == 0)
    def _():
        nk_ref[...] = jnp.zeros_like(nk_ref)
        nw_ref[...] = jnp.zeros_like(nw_ref)

    nk_ref[0:8, :] = nkv_ref[0]
    nw_ref[0:8, :] = nwin_ref[0]

    pc = _softmax_rows(logits(lambda k: ck_ref[0, pl.ds(k * SEG_ROWS, SEG_ROWS), :]) + bcmp_ref[...])
    o_c = weighted(pc, lambda k: cv_ref[0, pl.ds(k * SEG_ROWS, SEG_ROWS), :])
    ps = pc + pltpu.roll(pc, 16, axis=0) + pltpu.roll(pc, 32, axis=0) + pltpu.roll(pc, 48, axis=0)
    score = _dot_split3(ps, ov_ref[...])
    n_blk = -(-(PAST_LEN + DEC_SEQ) // SEL_BLK)
    cur = jnp.full((S_ROWS, 128), PAST_LEN // SEL_BLK, jnp.int32)
    sel = _select_blocks(score, cur, n_blk)
    key_mask = (_dot(sel, e_ref[...]) - 1.0) * BIG

    m = jnp.full((S_ROWS, 1), M_FLOOR, F32)
    for p in range(N_PAGES):
        s = (logits(lambda k: head_rows(page_refs[2 * p], k)) + bsel_ref[p]
             + key_mask[:, p * PAGE_SIZE:(p + 1) * PAGE_SIZE])
        s_ref[p] = s
        m = jnp.maximum(m, rowmax(s))
    sn = logits(lambda k: nk_ref[:, col(k, 0)]) + bnew_ref[...]
    m = jnp.maximum(m, rowmax(sn))
    pn = jnp.exp(sn - m)
    l = rowsum(pn)
    acc = weighted(pn, lambda k: nk_ref[:, col(k, 1)])
    for p in range(N_PAGES):
        pe = jnp.exp(s_ref[p] - m)
        l = l + rowsum(pe)
        acc = acc + weighted(pe, lambda k: head_rows(page_refs[2 * p + 1], k))
    o_s = acc / jnp.where(l > 0.0, l, 1.0)

    sw = logits(lambda k: head_rows(kwin_ref, k)) + bwin_ref[...]
    sn = logits(lambda k: nw_ref[:, col(k, 0)]) + bnew_ref[...]
    m = jnp.maximum(jnp.maximum(rowmax(sw), rowmax(sn)), M_FLOOR)
    pw = jnp.exp(sw - m)
    pn = jnp.exp(sn - m)
    l = rowsum(pw) + rowsum(pn)
    o_w = weighted(pw, lambda k: head_rows(vwin_ref, k)) + weighted(pn, lambda k: nw_ref[:, col(k, 1)])
    o_w = o_w / jnp.where(l > 0.0, l, 1.0)

    gates = jax.nn.sigmoid(gs_ref[0])
    o_ref[0] = gates[:, 0:1] * o_c + gates[:, 1:2] * o_s + gates[:, 2:3] * o_w


def _attn_sample(pt_flat, q_s, g_s, ck, cv, cache_kv, nkv, state_win, nwin, bcmp, bsel, bnew, bwin, ov, emat):
    const2 = lambda shape: pl.BlockSpec(shape, lambda n, pt: (0, 0))
    page_specs = [pl.BlockSpec((None, None, PAGE_SIZE, None, N_KV, HEAD_DIM),
                               lambda n, pt, p=p, kind=kind: (0, pt[n * N_PAGES + p], 0, kind, 0, 0))
                  for p in range(N_PAGES) for kind in (2, 3)]
    win_spec = lambda kind: pl.BlockSpec((None, None, WINDOW, None, N_KV, HEAD_DIM),
                                         lambda n, pt: (0, n, 0, kind, 0, 0))
    grid_spec = pltpu.PrefetchScalarGridSpec(
        num_scalar_prefetch=1,
        grid=(DEC_BATCH,),
        in_specs=[
            pl.BlockSpec((1, S_ROWS, HEAD_DIM), lambda n, pt: (n, 0, 0)),
            pl.BlockSpec((1, S_ROWS, 128), lambda n, pt: (n, 0, 0)),
            pl.BlockSpec((1, N_KV * SEG_ROWS, HEAD_DIM), lambda n, pt: (n, 0, 0)),
            pl.BlockSpec((1, N_KV * SEG_ROWS, HEAD_DIM), lambda n, pt: (n, 0, 0)),
        ] + page_specs + [
            pl.BlockSpec((1, 8, 1024), lambda n, pt: (n, 0, 0)),
            win_spec(0), win_spec(1),
            pl.BlockSpec((1, 8, 1024), lambda n, pt: (n, 0, 0)),
            const2((S_ROWS, 128)),
            pl.BlockSpec((N_PAGES, S_ROWS, 128), lambda n, pt: (0, 0, 0)),
            const2((S_ROWS, 128)),
            const2((S_ROWS, WINDOW)),
            const2((128, 128)),
            const2((128, PAST_LEN)),
        ],
        out_specs=pl.BlockSpec((1, S_ROWS, HEAD_DIM), lambda n, pt: (n, 0, 0)),
        scratch_shapes=[
            pltpu.VMEM((N_PAGES, S_ROWS, 128), F32),
            pltpu.VMEM((128, 1024), F32), pltpu.VMEM((128, 1024), F32),
        ],
    )
    return pl.pallas_call(
        _attn_sample_body,
        grid_spec=grid_spec,
        out_shape=jax.ShapeDtypeStruct((DEC_BATCH, S_ROWS, HEAD_DIM), F32),
        compiler_params=_cparams(1),
        name="attn_sample",
    )(pt_flat, q_s, g_s, ck, cv, *([cache_kv] * len(page_specs)), nkv, state_win, state_win, nwin,
      bcmp, bsel, bnew, bwin, ov, emat)


def _t5_bucket(dist):
    d = jnp.maximum(dist, 0)
    df = jnp.maximum(d, 1).astype(F32)
    large = MAX_EXACT + (jnp.log(df / MAX_EXACT) / math.log(MAX_DIST / MAX_EXACT)
                         * (N_BUCKETS - MAX_EXACT)).astype(jnp.int32)
    large = jnp.minimum(large, N_BUCKETS - 1)
    return jnp.where(d < MAX_EXACT, d, large)


def _bias_lookup_body(rb_ref, idx_ref, o_ref):
    idx = idx_ref[0]
    for h in range(N_HEADS):
        acc = jnp.full(idx.shape, NEG, F32)
        for b in range(N_BUCKETS):
            acc = jnp.where(idx == b, rb_ref[b * N_HEADS + h], acc)
        o_ref[0, h] = acc


def _bias_table(rel_bias, dist, valid, name):
    p, r, _ = dist.shape
    idx = jnp.where(jnp.asarray(valid), _t5_bucket(jnp.asarray(dist, jnp.int32)), -1)
    return pl.pallas_call(
        _bias_lookup_body,
        grid=(p,),
        in_specs=[pl.BlockSpec(memory_space=pltpu.SMEM), pl.BlockSpec((1, r, 128), lambda i: (i, 0, 0))],
        out_specs=pl.BlockSpec((1, N_HEADS, r, 128), lambda i: (i, 0, 0, 0)),
        out_shape=jax.ShapeDtypeStruct((p, N_HEADS, r, 128), F32),
        compiler_params=_cparams(1),
        name=name,
    )(rel_bias.astype(F32).reshape(-1), idx)


def _overlap(nc, nb):
    cs = np.arange(nc)[:, None] * CMP_STRIDE
    js = np.arange(nb)[None, :] * SEL_BLK
    ov = np.clip(np.minimum(cs + CMP_LEN, js + SEL_BLK) - np.maximum(cs, js), 0, None) / CMP_LEN
    out = np.zeros((128, 128), np.float32)
    out[:nc, :nb] = ov
    return jnp.asarray(out, BF16)


def _position_tables(rel_bias):
    nc = SEG_ROWS - 1
    t = np.arange(128)[None, :, None]
    c = np.arange(128)[None, None, :]
    cend = c * CMP_STRIDE + CMP_LEN - 1
    d = np.arange(MASKED_TILE + 1)[:, None, None] * 128 + t - c
    tz = _bias_table(rel_bias, d, (d >= 0) & (d < WINDOW), "bias_tiles")
    d = np.arange(SEQ // 128)[:, None, None] * 128 + t - cend
    bias_cmp = _bias_table(rel_bias, d, (d >= 0) & (c < nc), "bias_cmp")
    ts = np.arange(8)[None, :, None]
    qpos = PAST_LEN + ts
    live = ts < DEC_SEQ
    d_cmp = qpos - cend
    d_sel = qpos - (np.arange(N_PAGES)[:, None, None] * PAGE_SIZE + c)
    d_new = ts - c
    d_win = qpos - (PAST_LEN - WINDOW + np.arange(WINDOW // 128)[:, None, None] * 128 + c)
    d = np.concatenate([d_cmp, d_sel, d_new, d_win], axis=0)
    valid = np.concatenate([(d_cmp >= 0) & (c < nc), d_sel >= 0, (d_new >= 0) & (c < DEC_SEQ),
                            (d_win >= 0) & (d_win < WINDOW)], axis=0) & live
    o = _bias_table(rel_bias, d, valid, "bias_sample")[:, :, :DEC_SEQ]
    o = jnp.transpose(o.reshape(-1, N_KV, GROUP, DEC_SEQ, 128), (0, 2, 1, 3, 4)).reshape(-1, S_ROWS, 128)
    bcmp, bsel, bnew = o[0], o[1:1 + N_PAGES], o[1 + N_PAGES]
    bwin = jnp.transpose(o[2 + N_PAGES:], (1, 0, 2)).reshape(S_ROWS, WINDOW)
    keys = np.arange(SEQ)
    e_all = (np.arange(128)[:, None] == (keys // SEL_BLK)[None, :]).astype(np.float32)
    e_tiles = jnp.asarray(e_all.reshape(128, SEQ // 256, 256).transpose(1, 0, 2), BF16)
    ovt_p = jnp.transpose(_overlap(nc, SEQ // SEL_BLK))[:SEQ // SEL_BLK]
    return dict(tz=tz, bias_cmp=bias_cmp, bcmp=bcmp, bsel=bsel, bnew=bnew, bwin=bwin,
                ovt_p=ovt_p, ov_s=_overlap(nc, -(-(PAST_LEN + DEC_SEQ) // SEL_BLK)),
                e_tiles=e_tiles, e_all=jnp.asarray(e_all, BF16))


def _permute_w_in(w_in):
    c0 = 2 * D_RNN + N_HEADS * HEAD_DIM + 6 * N_KV * HEAD_DIM
    g_nsa = w_in[:, c0:c0 + 3 * N_HEADS].reshape(D_MODEL, N_KV, GROUP, 3)
    g_nsa = jnp.transpose(g_nsa, (0, 1, 3, 2)).reshape(D_MODEL, N_KV, 3 * GROUP)
    g_nsa = jnp.pad(g_nsa, ((0, 0), (0, 0), (0, 128 - 3 * GROUP))).reshape(D_MODEL, N_KV * 128)
    return jnp.concatenate([w_in[:, :c0], w_in[:, c0 + 3 * N_HEADS:], g_nsa], axis=1).astype(BF16)


def _cmp_weights(w1_k, w1_v, b1_k, b1_v, w2_k, w2_v, pos):
    def cat(w1):
        w = w1.reshape(2, CMP_STRIDE * HEAD_DIM, CMP_HID)
        return jnp.concatenate([w[0], w[1]], axis=1).reshape(N_PAIR, 256, 2 * CMP_HID)
    w1 = jnp.stack([cat(w1_k), cat(w1_v)]).astype(BF16)
    posm = jnp.pad(pos.reshape(2, CMP_STRIDE * HEAD_DIM), ((0, 6), (0, 0)))
    b1 = jnp.stack([b1_k, b1_v]).reshape(2, 1, CMP_HID)
    w2 = jnp.stack([w2_k, w2_v]).astype(BF16)
    return w1, posm, b1, w2


def kernel(x_prompt, x_sample, cache_kv, page_table, state_win, state_conv, state_h, rel_bias, ln_final, ln_ffn1, w_ffn1_gate, w_ffn1_up, w_ffn1_down, ln_mix, w_in, conv_w, conv_b, rg_wa, rg_ba, rg_wi, rg_bi, rg_lambda, cmp_pos, cmp_k_w1, cmp_k_b1, cmp_k_w2, cmp_v_w1, cmp_v_b1, cmp_v_w2, w_br_rnn, w_br_attn, w_out, ln_ffn2, w_ffn2_gate, w_ffn2_up, w_ffn2_down):
    tabs = _position_tables(rel_bias)
    x = jnp.concatenate([x_prompt.reshape(M_PROMPT, D_MODEL), x_sample.reshape(M_SAMPLE, D_MODEL)], axis=0)

    x = _ffn(x, ln_ffn1[0], w_ffn1_gate[0].astype(BF16), w_ffn1_up[0].astype(BF16), w_ffn1_down[0].astype(BF16),
             ln_final, False)
    z = _in_proj(x, ln_mix[0], _permute_w_in(w_in[0]))
    z_s = z[M_PROMPT:]

    vec = lambda v: v.reshape(1, D_RNN)
    rnn_w = (conv_w[0], vec(conv_b[0]), rg_wa[0].astype(BF16), vec(rg_ba[0]), rg_wi[0].astype(BF16),
             vec(rg_bi[0]), vec(rg_lambda[0]))
    g_p, h_p = _rnn_prompt(z, *rnn_w)
    tmajor = lambda a: jnp.transpose(a.reshape(DEC_BATCH, -1, D_RNN), (1, 0, 2))
    g_s, h_s = _rnn_sample(tmajor(z_s[:, C_UGATE:C_UGATE + D_RNN]), tmajor(z_s[:, C_UX:C_UX + D_RNN]),
                           tmajor(state_conv[0]), state_h[0], *rnn_w)
    grnn = jnp.concatenate([g_p, jnp.transpose(g_s, (1, 0, 2)).reshape(M_SAMPLE, D_RNN)], axis=0)

    cw = _cmp_weights(cmp_k_w1[0], cmp_v_w1[0], cmp_k_b1[0], cmp_v_b1[0], cmp_k_w2[0], cmp_v_w2[0], cmp_pos[0])
    pt_prompt = jnp.arange(BATCH * N_PAGES, dtype=jnp.int32)
    pt_sample = page_table.reshape(-1).astype(jnp.int32)
    ck_p, cv_p = _compress(z.reshape(M_TOK // PAGE_SIZE, PAGE_SIZE, D_Z), pt_prompt, C_PAG // 1024, BATCH, *cw,
                           name="compress_prompt")
    ck_s, cv_s = _compress(cache_kv, pt_sample, None, DEC_BATCH, *cw, name="compress_sample")
    o_p = _attn_prompt(z, ck_p.reshape(BATCH * N_KV, SEG_ROWS, HEAD_DIM), cv_p.reshape(BATCH * N_KV, SEG_ROWS, HEAD_DIM),
                       tabs["bias_cmp"], tabs["tz"], tabs["ovt_p"], tabs["e_tiles"])

    def rows_gkt(a, width):
        a = a.reshape(DEC_BATCH, DEC_SEQ, N_KV, GROUP, width)
        return jnp.transpose(a, (0, 3, 2, 1, 4)).reshape(DEC_BATCH, S_ROWS, width)

    q_s = rows_gkt(z_s[:, C_Q:C_Q + N_HEADS * HEAD_DIM], HEAD_DIM)
    gn = z_s[:, C_GNSA:].reshape(M_SAMPLE, N_KV, 128)[:, :, :3 * GROUP].reshape(M_SAMPLE, N_KV, 3, GROUP)
    g_s3 = jnp.pad(rows_gkt(jnp.transpose(gn, (0, 1, 3, 2)), 3), ((0, 0), (0, 0), (0, 125)))
    pad8 = lambda a: jnp.pad(a.reshape(DEC_BATCH, DEC_SEQ, -1), ((0, 0), (0, 8 - DEC_SEQ), (0, 0)))
    nkv = pad8(z_s[:, C_PAG + 2 * N_KV * HEAD_DIM:C_PAG + 4 * N_KV * HEAD_DIM])
    nwin = pad8(z_s[:, C_WIN:C_WIN + 2 * N_KV * HEAD_DIM])
    o_s = _attn_sample(pt_sample, q_s, g_s3, ck_s, cv_s, cache_kv, nkv, state_win, nwin,
                       tabs["bcmp"], tabs["bsel"], tabs["bnew"], tabs["bwin"], tabs["ov_s"], tabs["e_all"])
    o_s = jnp.transpose(o_s.reshape(DEC_BATCH, GROUP, N_KV, DEC_SEQ, HEAD_DIM), (0, 3, 2, 1, 4))
    oattn = jnp.concatenate([o_p, o_s.reshape(M_SAMPLE, N_HEADS * HEAD_DIM).astype(BF16)], axis=0)

    merged = _merge(z, grnn, oattn, w_br_rnn[0].astype(BF16), w_br_attn[0].astype(BF16))
    x = _out_proj(x, merged, w_out[0].astype(BF16))
    y = _ffn(x, ln_ffn2[0], w_ffn2_gate[0].astype(BF16), w_ffn2_up[0].astype(BF16), w_ffn2_down[0].astype(BF16),
             ln_final, True)

    kv = z[:, C_PAG:C_PAG + 4 * N_KV * HEAD_DIM]
    wn = z[:, C_WIN:C_WIN + 2 * N_KV * HEAD_DIM]
    ux = z[:, C_UX:C_UX + D_RNN]
    win_p = wn[:M_PROMPT].reshape(BATCH, SEQ, 2, N_KV, HEAD_DIM)[:, SEQ - WINDOW:]
    win_s = jnp.concatenate([state_win.reshape(DEC_BATCH, WINDOW, 2, N_KV, HEAD_DIM),
                             wn[M_PROMPT:].reshape(DEC_BATCH, DEC_SEQ, 2, N_KV, HEAD_DIM)], axis=1)
    return (
        y[:M_PROMPT].reshape(BATCH, SEQ, D_MODEL),
        y[M_PROMPT:].reshape(DEC_BATCH, DEC_SEQ, D_MODEL),
        kv[:M_PROMPT].reshape(1, BATCH, SEQ, 4, N_KV, HEAD_DIM),
        kv[M_PROMPT:].reshape(1, DEC_BATCH, DEC_SEQ, 4, N_KV, HEAD_DIM),
        win_p[None],
        win_s[None, :, DEC_SEQ:],
        ux[:M_PROMPT].reshape(BATCH, SEQ, D_RNN)[None, :, SEQ - (CONV_W - 1):],
        ux[M_PROMPT:].reshape(DEC_BATCH, DEC_SEQ, D_RNN)[None, :, DEC_SEQ - (CONV_W - 1):],
        h_p[None, :, 7],
        h_s[None],
    )
```

```python
import functools
import math

import numpy as np
import jax
import jax.numpy as jnp
from jax import lax
from jax.experimental import pallas as pl
from jax.experimental.pallas import tpu as pltpu

F32 = jnp.float32
BF16 = jnp.bfloat16

D_MODEL = 4096
BATCH = 4
SEQ = 2048
DEC_BATCH = 128
DEC_SEQ = 4
PAST_LEN = 2048
PAGE_SIZE = 128
N_PAGES = PAST_LEN // PAGE_SIZE
D_RNN = D_MODEL // 2
RNN_BLOCKS = 16
RNN_BW = D_RNN // RNN_BLOCKS
CONV_W = 4
LRU_C = 8.0
N_HEADS = 16
HEAD_DIM = 128
N_KV = 4
GROUP = N_HEADS // N_KV
CMP_LEN = 32
CMP_STRIDE = 16
CMP_HID = 2 * HEAD_DIM
SEL_BLK = 64
N_SEL = 8
WINDOW = 512
N_BUCKETS = 32
MAX_EXACT = 16
MAX_DIST = 128
D_FF = ((8 * D_MODEL // 3 + 255) // 256) * 256
EPS = 1e-6
NEG = -1e30
BIG = 1e30
M_FLOOR = -1e29
Q_SCALE = HEAD_DIM ** -0.5

M_PROMPT = BATCH * SEQ
M_SAMPLE = DEC_BATCH * DEC_SEQ
M_TOK = M_PROMPT + M_SAMPLE

C_UGATE = 0
C_UX = C_UGATE + D_RNN
C_Q = C_UX + D_RNN
C_PAG = C_Q + N_HEADS * HEAD_DIM
C_WIN = C_PAG + 4 * N_KV * HEAD_DIM
C_GRNN = C_WIN + 2 * N_KV * HEAD_DIM
C_GATTN = C_GRNN + D_MODEL
C_GNSA = C_GATTN + D_MODEL
D_Z = C_GNSA + N_KV * 128

TM = 512
TF = 256
TN_IN = 1280
TN_MM = 1024
VMEM_LIMIT = 56 * 2 ** 20


def _cparams(n_axes, vmem=VMEM_LIMIT):
    return pltpu.CompilerParams(dimension_semantics=("arbitrary",) * n_axes, vmem_limit_bytes=vmem)


def _dot(a, b):
    return jnp.dot(a, b, preferred_element_type=F32)


def _dot_nt(a, b):
    return lax.dot_general(a, b, (((1,), (1,)), ((), ())), preferred_element_type=F32)


def _dot_split3(a, b):
    a1 = a.astype(BF16)
    r1 = a - a1.astype(F32)
    a2 = r1.astype(BF16)
    a3 = (r1 - a2.astype(F32)).astype(BF16)
    return _dot(a1, b) + _dot(a2, b) + _dot(a3, b)


def _dot_nt_split3(a, b):
    b1 = b.astype(BF16)
    r1 = b - b1.astype(F32)
    b2 = r1.astype(BF16)
    b3 = (r1 - b2.astype(F32)).astype(BF16)
    return _dot_nt(a, b1) + _dot_nt(a, b2) + _dot_nt(a, b3)


def _kv_head_rows(ref, k):
    n_rows = ref.shape[0]
    return ref.reshape(n_rows * N_KV, HEAD_DIM)[pl.ds(k, n_rows, stride=N_KV), :]


def _rms(x, g):
    return x * lax.rsqrt(jnp.mean(x * x, axis=-1, keepdims=True) + EPS) * g


def _ffn_body(x_ref, ln_ref, wg_ref, wu_ref, wd_ref, lnf_ref, o_ref, xn_ref, *, n_f, final_norm):
    f = pl.program_id(1)

    @pl.when(f == 0)
    def _():
        x = x_ref[...]
        xn_ref[...] = _rms(x, ln_ref[...]).astype(BF16)
        o_ref[...] = 2.0 * x

    xn = xn_ref[...]
    g = _dot(xn, wg_ref[...])
    u = _dot(xn, wu_ref[...])
    h = (g * jax.nn.sigmoid(g) * u).astype(BF16)
    o_ref[...] += _dot(h, wd_ref[...])

    @pl.when(f == n_f - 1)
    def _():
        y = 0.5 * o_ref[...]
        if final_norm:
            y = _rms(y, lnf_ref[...])
        o_ref[...] = y


def _ffn(x, ln, wg, wu, wd, lnf, final_norm):
    m = x.shape[0]
    n_f = D_FF // TF
    return pl.pallas_call(
        functools.partial(_ffn_body, n_f=n_f, final_norm=final_norm),
        grid=(m // TM, n_f),
        in_specs=[
            pl.BlockSpec((TM, D_MODEL), lambda i, f: (i, 0), pipeline_mode=pl.Buffered(1)),
            pl.BlockSpec((1, D_MODEL), lambda i, f: (0, 0)),
            pl.BlockSpec((D_MODEL, TF), lambda i, f: (0, f)),
            pl.BlockSpec((D_MODEL, TF), lambda i, f: (0, f)),
            pl.BlockSpec((TF, D_MODEL), lambda i, f: (f, 0)),
            pl.BlockSpec((1, D_MODEL), lambda i, f: (0, 0)),
        ],
        out_specs=pl.BlockSpec((TM, D_MODEL), lambda i, f: (i, 0)),
        out_shape=jax.ShapeDtypeStruct((m, D_MODEL), F32),
        scratch_shapes=[pltpu.VMEM((TM, D_MODEL), BF16)],
        compiler_params=_cparams(2),
        name="ffn",
    )(x, ln.reshape(1, D_MODEL), wg, wu, wd, lnf.reshape(1, D_MODEL))


def _in_proj_body(x_ref, ln_ref, w_ref, o_ref, xn_ref):
    @pl.when(pl.program_id(1) == 0)
    def _():
        xn_ref[...] = _rms(x_ref[...], ln_ref[...]).astype(BF16)

    o_ref[...] = _dot(xn_ref[...], w_ref[...])


def _in_proj(x, ln, w):
    m = x.shape[0]
    return pl.pallas_call(
        _in_proj_body,
        grid=(m // TM, D_Z // TN_IN),
        in_specs=[
            pl.BlockSpec((TM, D_MODEL), lambda i, j: (i, 0), pipeline_mode=pl.Buffered(1)),
            pl.BlockSpec((1, D_MODEL), lambda i, j: (0, 0)),
            pl.BlockSpec((D_MODEL, TN_IN), lambda i, j: (0, j)),
        ],
        out_specs=pl.BlockSpec((TM, TN_IN), lambda i, j: (i, j)),
        out_shape=jax.ShapeDtypeStruct((m, D_Z), F32),
        scratch_shapes=[pltpu.VMEM((TM, D_MODEL), BF16)],
        compiler_params=_cparams(2),
        name="in_proj",
    )(x, ln.reshape(1, D_MODEL), w)


def _merge_body(gr_ref, oa_ref, wr_ref, wa_ref, zr_ref, za_ref, o_ref):
    y_rnn = _dot(gr_ref[...], wr_ref[...])
    y_attn = _dot(oa_ref[...], wa_ref[...])
    o_ref[...] = (jax.nn.sigmoid(zr_ref[...]) * y_rnn + jax.nn.sigmoid(za_ref[...]) * y_attn).astype(BF16)


def _merge(z, grnn, oattn, w_rnn, w_attn):
    m = z.shape[0]
    cr, ca = C_GRNN // TN_MM, C_GATTN // TN_MM
    return pl.pallas_call(
        _merge_body,
        grid=(m // TM, D_MODEL // TN_MM),
        in_specs=[
            pl.BlockSpec((TM, D_RNN), lambda i, j: (i, 0)),
            pl.BlockSpec((TM, N_HEADS * HEAD_DIM), lambda i, j: (i, 0)),
            pl.BlockSpec((D_RNN, TN_MM), lambda i, j: (0, j)),
            pl.BlockSpec((N_HEADS * HEAD_DIM, TN_MM), lambda i, j: (0, j)),
            pl.BlockSpec((TM, TN_MM), lambda i, j: (i, cr + j)),
            pl.BlockSpec((TM, TN_MM), lambda i, j: (i, ca + j)),
        ],
        out_specs=pl.BlockSpec((TM, TN_MM), lambda i, j: (i, j)),
        out_shape=jax.ShapeDtypeStruct((m, D_MODEL), BF16),
        compiler_params=_cparams(2),
        name="merge",
    )(grnn, oattn, w_rnn, w_attn, z, z)


def _out_proj_body(a_ref, w_ref, x_ref, o_ref):
    o_ref[...] = x_ref[...] + _dot(a_ref[...], w_ref[...])


def _out_proj(x, a, w):
    m = x.shape[0]
    return pl.pallas_call(
        _out_proj_body,
        grid=(m // TM, D_MODEL // TN_MM),
        in_specs=[
            pl.BlockSpec((TM, D_MODEL), lambda i, j: (i, 0)),
            pl.BlockSpec((D_MODEL, TN_MM), lambda i, j: (0, j)),
            pl.BlockSpec((TM, TN_MM), lambda i, j: (i, j)),
        ],
        out_specs=pl.BlockSpec((TM, TN_MM), lambda i, j: (i, j)),
        out_shape=jax.ShapeDtypeStruct((m, D_MODEL), F32),
        compiler_params=_cparams(2),
        name="out_proj",
    )(a, w, x)


def _softplus(v):
    return jnp.maximum(v, 0.0) + jnp.log1p(jnp.exp(-jnp.abs(v)))


def _lru_coeffs(xc, wa_ref, ba, wi_ref, bi, sp, n_blk):
    xb = xc.astype(BF16)
    ra = jnp.concatenate([_dot(xb[:, b * RNN_BW:(b + 1) * RNN_BW], wa_ref[b]) for b in range(n_blk)], axis=1)
    ia = jnp.concatenate([_dot(xb[:, b * RNN_BW:(b + 1) * RNN_BW], wi_ref[b]) for b in range(n_blk)], axis=1)
    r = jax.nn.sigmoid(ra + ba)
    i = jax.nn.sigmoid(ia + bi)
    log_a = -LRU_C * r * sp
    a = jnp.exp(log_a)
    bt = jnp.sqrt(-jnp.tanh(log_a) * (a * a + 1.0)) * (i * xc)
    return a, bt


RNN_TC = 256


def _rnn_prompt_body(ug_ref, ux_ref, cw_ref, cb_ref, wa_ref, ba_ref, wi_ref, bi_ref, lam_ref,
                     g_ref, h_ref, tail_ref, hc_ref):
    c = pl.program_id(1)
    tc = RNN_TC

    @pl.when(c == 0)
    def _():
        tail_ref[...] = jnp.zeros_like(tail_ref)
        hc_ref[...] = jnp.zeros_like(hc_ref)

    u = ux_ref[...]
    tail = tail_ref[...]
    row8 = lax.broadcasted_iota(jnp.int32, (8, D_RNN), 0)
    xc = cb_ref[...] + cw_ref[CONV_W - 1:CONV_W, :] * u
    for j in range(1, CONV_W):
        r = pltpu.roll(u, j, axis=0)
        first = jnp.where(row8 >= j, r[0:8], pltpu.roll(tail, j, axis=0))
        shifted = jnp.concatenate([first, r[8:]], axis=0)
        xc = xc + cw_ref[CONV_W - 1 - j:CONV_W - j, :] * shifted
    tail_ref[...] = u[tc - 8:tc]

    a, bt = _lru_coeffs(xc, wa_ref, ba_ref[...], wi_ref, bi_ref[...], _softplus(-lam_ref[...]), RNN_BLOCKS)

    row = lax.broadcasted_iota(jnp.int32, (tc, D_RNN), 0)
    s = 1
    while s < tc:
        keep = row >= s
        a_sh = jnp.where(keep, pltpu.roll(a, s, axis=0), 1.0)
        b_sh = jnp.where(keep, pltpu.roll(bt, s, axis=0), 0.0)
        bt = a * b_sh + bt
        a = a * a_sh
        s *= 2
    h = bt + a * hc_ref[7:8, :]
    hc_ref[...] = h[tc - 8:tc]
    g_ref[...] = (h * jax.nn.gelu(ug_ref[...])).astype(BF16)

    @pl.when(c == pl.num_programs(1) - 1)
    def _():
        h_ref[0] = h[tc - 8:tc]


def _rnn_prompt(z, cw, cb, wa, ba, wi, bi, lam):
    nc = SEQ // RNN_TC
    vec = lambda: pl.BlockSpec((1, D_RNN), lambda n, c: (0, 0))
    blk = lambda: pl.BlockSpec((RNN_BLOCKS, RNN_BW, RNN_BW), lambda n, c: (0, 0, 0))
    return pl.pallas_call(
        _rnn_prompt_body,
        grid=(BATCH, nc),
        in_specs=[
            pl.BlockSpec((RNN_TC, D_RNN), lambda n, c: (n * nc + c, C_UGATE // D_RNN)),
            pl.BlockSpec((RNN_TC, D_RNN), lambda n, c: (n * nc + c, C_UX // D_RNN)),
            pl.BlockSpec((CONV_W, D_RNN), lambda n, c: (0, 0)),
            vec(), blk(), vec(), blk(), vec(), vec(),
        ],
        out_specs=[
            pl.BlockSpec((RNN_TC, D_RNN), lambda n, c: (n * nc + c, 0)),
            pl.BlockSpec((1, 8, D_RNN), lambda n, c: (n, 0, 0)),
        ],
        out_shape=[jax.ShapeDtypeStruct((M_PROMPT, D_RNN), BF16),
                   jax.ShapeDtypeStruct((BATCH, 8, D_RNN), F32)],
        scratch_shapes=[pltpu.VMEM((8, D_RNN), F32), pltpu.VMEM((8, D_RNN), F32)],
        compiler_params=_cparams(2),
        name="rnn_prompt",
    )(z, z, cw, cb, wa, ba, wi, bi, lam)


RNN_SC = 512


def _rnn_sample_body(ug_ref, ux_ref, buf_ref, h0_ref, cw_ref, cb_ref, wa_ref, ba_ref, wi_ref, bi_ref, lam_ref,
                     g_ref, h_ref):
    full = [buf_ref[j] for j in range(CONV_W - 1)] + [ux_ref[t] for t in range(DEC_SEQ)]
    sp = _softplus(-lam_ref[...])
    h = h0_ref[...]
    for t in range(DEC_SEQ):
        xc = cb_ref[...]
        for k in range(CONV_W):
            xc = xc + full[t + k] * cw_ref[k:k + 1, :]
        a, bt = _lru_coeffs(xc, wa_ref, ba_ref[...], wi_ref, bi_ref[...], sp, RNN_SC // RNN_BW)
        h = a * h + bt
        g_ref[t] = (h * jax.nn.gelu(ug_ref[t])).astype(BF16)
    h_ref[...] = h


def _rnn_sample(ug_t, ux_t, buf_t, h0, cw, cb, wa, ba, wi, bi, lam):
    nb = RNN_SC // RNN_BW
    vec = lambda: pl.BlockSpec((1, RNN_SC), lambda c: (0, c))
    blk = lambda: pl.BlockSpec((nb, RNN_BW, RNN_BW), lambda c: (c, 0, 0))
    return pl.pallas_call(
        _rnn_sample_body,
        grid=(D_RNN // RNN_SC,),
        in_specs=[
            pl.BlockSpec((DEC_SEQ, DEC_BATCH, RNN_SC), lambda c: (0, 0, c)),
            pl.BlockSpec((DEC_SEQ, DEC_BATCH, RNN_SC), lambda c: (0, 0, c)),
            pl.BlockSpec((CONV_W - 1, DEC_BATCH, RNN_SC), lambda c: (0, 0, c)),
            pl.BlockSpec((DEC_BATCH, RNN_SC), lambda c: (0, c)),
            pl.BlockSpec((CONV_W, RNN_SC), lambda c: (0, c)),
            vec(), blk(), vec(), blk(), vec(), vec(),
        ],
        out_specs=[
            pl.BlockSpec((DEC_SEQ, DEC_BATCH, RNN_SC), lambda c: (0, 0, c)),
            pl.BlockSpec((DEC_BATCH, RNN_SC), lambda c: (0, c)),
        ],
        out_shape=[jax.ShapeDtypeStruct((DEC_SEQ, DEC_BATCH, D_RNN), BF16),
                   jax.ShapeDtypeStruct((DEC_BATCH, D_RNN), F32)],
        compiler_params=_cparams(1),
        name="rnn_sample",
    )(ug_t, ux_t, buf_t, h0, cw, cb, wa, ba, wi, bi, lam)


N_SEG = PAGE_SIZE // CMP_STRIDE
SEG_ROWS = N_PAGES * N_SEG
N_PAIR = CMP_STRIDE // 2


CMP_PG = 4
SLAB_PITCH = 24


def _compress_body(pt_ref, *refs, paged):
    n_in = CMP_PG * (2 if paged else 1)
    page_refs = refs[:n_in]
    w1_ref, pos_ref, b1_ref, w2_ref, ck_ref, cv_ref, stage_ref, pterm_ref, slab_ref = refs[n_in:]
    n = pl.program_id(0)
    q = pl.program_id(1)

    @pl.when((n == 0) & (q == 0))
    def _():
        for kind in range(2):
            acc = jnp.zeros((8, 2 * CMP_HID), F32)
            for pr in range(N_PAIR):
                acc = acc + _dot(pos_ref[:, pr * 256:(pr + 1) * 256].astype(BF16), w1_ref[kind, pr])
            pterm_ref[kind] = acc

    for j in range(CMP_PG):
        for kind in range(2):
            for k in range(N_KV):
                if paged:
                    slab = _kv_head_rows(page_refs[j * 2 + kind], k)
                else:
                    kk = kind * N_KV + k
                    slab = page_refs[j][:, kk * HEAD_DIM:(kk + 1) * HEAD_DIM]
                for s in range(N_SEG):
                    slab_ref[(j * 2 + kind) * N_KV + k, pl.ds(s * SLAB_PITCH, CMP_STRIDE), :] = (
                        slab[s * CMP_STRIDE:(s + 1) * CMP_STRIDE])

    for j in range(CMP_PG):
        seg0 = (q * CMP_PG + j) * N_SEG
        for kind in range(2):
            for k in range(N_KV):
                for l in range(CMP_STRIDE):
                    piece = slab_ref[(j * 2 + kind) * N_KV + k, pl.ds(l, N_SEG, stride=SLAB_PITCH), :]
                    stage_ref[kind, l // 2, pl.ds(k * SEG_ROWS + seg0, N_SEG),
                              pl.ds((l % 2) * HEAD_DIM, HEAD_DIM)] = piece

    @pl.when(q == N_PAGES // CMP_PG - 1)
    def _():
        for kind, out_ref in ((0, ck_ref), (1, cv_ref)):
            acc = jnp.zeros((N_KV * SEG_ROWS, 2 * CMP_HID), F32)
            for pr in range(N_PAIR):
                acc = acc + _dot(stage_ref[kind, pr].astype(BF16), w1_ref[kind, pr])
            nxt = pltpu.roll(acc[:, CMP_HID:], N_KV * SEG_ROWS - 1, axis=0)
            pt = pterm_ref[kind]
            posterm = pt[0:1, :CMP_HID] + pt[1:2, CMP_HID:] + b1_ref[kind]
            hid = acc[:, :CMP_HID] + nxt + posterm
            out_ref[0] = _dot(jax.nn.gelu(hid).astype(BF16), w2_ref[kind]).astype(BF16)


def _compress(src, pt_flat, col_blk, n_seq, w1, pos, b1, w2, name):
    paged = col_blk is None
    page_of = lambda n, p, pt, j: pt[n * N_PAGES + p * CMP_PG + j]
    out_spec = pl.BlockSpec((1, N_KV * SEG_ROWS, HEAD_DIM), lambda n, p, pt: (n, 0, 0))
    if paged:
        page_specs = [pl.BlockSpec((None, None, PAGE_SIZE, None, N_KV, HEAD_DIM),
                                   lambda n, p, pt, j=j, kind=kind: (0, page_of(n, p, pt, j), 0, kind, 0, 0))
                      for j in range(CMP_PG) for kind in range(2)]
    else:
        page_specs = [pl.BlockSpec((None, PAGE_SIZE, 2 * N_KV * HEAD_DIM),
                                   lambda n, p, pt, j=j: (page_of(n, p, pt, j), 0, col_blk))
                      for j in range(CMP_PG)]
    grid_spec = pltpu.PrefetchScalarGridSpec(
        num_scalar_prefetch=1,
        grid=(n_seq, N_PAGES // CMP_PG),
        in_specs=page_specs + [
            pl.BlockSpec((2, N_PAIR, 256, 2 * CMP_HID), lambda n, p, pt: (0, 0, 0, 0)),
            pl.BlockSpec((8, CMP_STRIDE * HEAD_DIM), lambda n, p, pt: (0, 0)),
            pl.BlockSpec((2, 1, CMP_HID), lambda n, p, pt: (0, 0, 0)),
            pl.BlockSpec((2, CMP_HID, HEAD_DIM), lambda n, p, pt: (0, 0, 0)),
        ],
        out_specs=[out_spec, out_spec],
        scratch_shapes=[pltpu.VMEM((2, N_PAIR, N_KV * SEG_ROWS, 256), F32),
                        pltpu.VMEM((2, 8, 2 * CMP_HID), F32),
                        pltpu.VMEM((CMP_PG * 2 * N_KV, N_SEG * SLAB_PITCH, HEAD_DIM), F32)],
    )
    shp = jax.ShapeDtypeStruct((n_seq, N_KV * SEG_ROWS, HEAD_DIM), BF16)
    return pl.pallas_call(
        functools.partial(_compress_body, paged=paged),
        grid_spec=grid_spec,
        out_shape=[shp, shp],
        compiler_params=_cparams(2),
        name=name,
    )(pt_flat, *([src] * len(page_specs)), w1, pos, b1, w2)


def _select_blocks(score, cur, n_blk):
    jj = lax.broadcasted_iota(jnp.int32, score.shape, 1)
    forced = (jj == 0) | (jj == cur) | (jj == cur - 1)
    sc = jnp.where(forced, BIG, jnp.where(jj <= cur, score, NEG))
    rank = jnp.zeros(score.shape, F32)
    for i in range(n_blk):
        si = sc[:, i:i + 1]
        beats = (si > sc) | ((si == sc) & (jj > i))
        rank = rank + jnp.where(beats, 1.0, 0.0)
    sel = (rank < float(min(N_SEL, n_blk))) & (jj <= cur) & (jj < n_blk)
    return jnp.where(sel, 1.0, 0.0).astype(BF16)


def _select_blocks_t(score, cur):
    n_blk = score.shape[0]
    jj = lax.broadcasted_iota(jnp.int32, score.shape, 0)
    forced = (jj == 0) | (jj == cur) | (jj == cur - 1)
    sc = jnp.where(forced, BIG, jnp.where(jj <= cur, score, NEG))
    rank = jnp.zeros(score.shape, F32)
    for i in range(n_blk):
        si = sc[i:i + 1, :]
        beats = (si > sc) | ((si == sc) & (jj > i))
        rank = rank + jnp.where(beats, 1.0, 0.0)
    sel = (rank < float(min(N_SEL, n_blk))) & (jj <= cur)
    return jnp.where(sel, 1.0, 0.0)


def _softmax_rows(logits):
    m = jnp.maximum(jnp.max(logits, axis=-1, keepdims=True), M_FLOOR)
    e = jnp.exp(logits - m)
    s = jnp.sum(e, axis=-1, keepdims=True)
    return e / jnp.where(s > 0.0, s, 1.0)


MASKED_TILE = WINDOW // 128 + 1
ATT_HP = 2


def _attn_prompt_body(zq_ref, zg_ref, ck_ref, cv_ref, ks_ref, vs_ref, kw_ref, vw_ref,
                      bc_ref, tz_ref, ovt_ref, e_ref, o_ref,
                      ksb, vsb, kwb, vwb, selm_ref, s_ref, mel_ref, lel_ref, acc_ref):
    qt = pl.program_id(2)
    rows = GROUP * 128

    @pl.when(qt == 0)
    def _():
        ksb[...] = ks_ref[...].astype(BF16)
        vsb[...] = vs_ref[...].astype(BF16)
        kwb[...] = kw_ref[...].astype(BF16)
        vwb[...] = vw_ref[...].astype(BF16)

    heads = range(ATT_HP)
    lanes = lambda h: slice(h * HEAD_DIM, (h + 1) * HEAD_DIM)
    groups = lambda h: slice(h * GROUP, (h + 1) * GROUP)
    n_blk = SEQ // SEL_BLK
    cur = jnp.right_shift(qt * 128 + lax.broadcasted_iota(jnp.int32, (n_blk, 128), 1), 6)
    qq, o_c = [], []
    for h in heads:
        q = zq_ref[:, h * GROUP * HEAD_DIM:(h + 1) * GROUP * HEAD_DIM] * Q_SCALE
        qh = jnp.concatenate([q[:, g * HEAD_DIM:(g + 1) * HEAD_DIM] for g in range(GROUP)], axis=0).astype(BF16)
        qq.append(qh)
        pc = _softmax_rows(_dot_nt(qh, ck_ref[h]) + bc_ref[0, groups(h)].reshape(rows, 128))
        o_c.append(_dot(pc.astype(BF16), cv_ref[h]))
        ps = pc[0:128] + pc[128:256] + pc[256:384] + pc[384:512]
        sel_t = _select_blocks_t(_dot_nt_split3(ovt_ref[...], ps), cur)
        sel = jnp.concatenate([sel_t, jnp.zeros((128 - n_blk, 128), F32)], axis=0).T.astype(BF16)
        for j in range(SEQ // 256):
            selm_ref[h, j] = (_dot(sel, e_ref[j]) - 1.0) * BIG

    def attend(k_ref, v_ref, lo, tile_of_delta, use_sel):
        hi = jnp.right_shift(qt, 1) + 1
        mel_ref[...] = jnp.full(mel_ref.shape, M_FLOOR, F32)

        def logits_pass(j, carry):
            off = pl.multiple_of(j * 256, 256)
            d0 = qt - 2 * j
            i0, i1 = tile_of_delta(d0), tile_of_delta(d0 - 1)
            for h in heads:
                s = _dot_nt(qq[h], k_ref[pl.ds(off, 256), lanes(h)]).reshape(GROUP, 128, 256)
                s = s + jnp.concatenate([tz_ref[i0, groups(h)], tz_ref[i1, groups(h)]], axis=-1)
                if use_sel:
                    s = s + selm_ref[h, j][None]
                s = s.reshape(rows, 256)
                s_ref[h, j] = s
                mel_ref[h] = jnp.maximum(mel_ref[h], jnp.maximum(s[:, :128], s[:, 128:]))
            return carry

        lax.fori_loop(lo, hi, logits_pass, 0)
        m = [jnp.max(mel_ref[h], axis=-1, keepdims=True) for h in heads]
        lel_ref[...] = jnp.zeros_like(lel_ref)
        acc_ref[...] = jnp.zeros_like(acc_ref)

        def value_pass(j, carry):
            off = pl.multiple_of(j * 256, 256)
            for h in heads:
                pe = jnp.exp(s_ref[h, j] - m[h])
                lel_ref[h] += pe[:, :128] + pe[:, 128:]
                acc_ref[h] += _dot(pe.astype(BF16), v_ref[pl.ds(off, 256), lanes(h)])
            return carry

        lax.fori_loop(lo, hi, value_pass, 0)
        outs = []
        for h in heads:
            l = jnp.sum(lel_ref[h], axis=-1, keepdims=True)
            outs.append(acc_ref[h] / jnp.where(l > 0.0, l, 1.0))
        return outs

    n_win = WINDOW // 128
    o_s = attend(ksb, vsb, 0, lambda d: jnp.where(d < 0, MASKED_TILE, jnp.minimum(d, 2)), True)
    o_w = attend(kwb, vwb, jnp.right_shift(jnp.maximum(qt - n_win, 0), 1),
                 lambda d: jnp.where((d < 0) | (d > n_win), MASKED_TILE, d), False)

    gates = jax.nn.sigmoid(zg_ref[...])
    outs = []
    for h in heads:
        for g in range(GROUP):
            r = slice(g * 128, (g + 1) * 128)
            gate = lambda branch: gates[:, h * 128 + branch * GROUP + g:h * 128 + branch * GROUP + g + 1]
            outs.append(gate(0) * o_c[h][r] + gate(1) * o_s[h][r] + gate(2) * o_w[h][r])
    o_ref[...] = jnp.concatenate(outs, axis=1).astype(BF16)


def _attn_prompt(z, ck, cv, bias_cmp, tz, ov, emat):
    nq = SEQ // 128
    nhp = N_KV // ATT_HP
    kw = ATT_HP * HEAD_DIM
    qw = ATT_HP * GROUP * HEAD_DIM
    kv_col = lambda base, kind: (lambda n, k, t: (n, (base + kind * N_KV * HEAD_DIM) // kw + k))
    kvspec = lambda base, kind: pl.BlockSpec((SEQ, kw), kv_col(base, kind))
    rows = GROUP * 128
    return pl.pallas_call(
        _attn_prompt_body,
        grid=(BATCH, nhp, nq),
        in_specs=[
            pl.BlockSpec((128, qw), lambda n, k, t: (n * nq + t, C_Q // qw + k)),
            pl.BlockSpec((128, ATT_HP * 128), lambda n, k, t: (n * nq + t, C_GNSA // (ATT_HP * 128) + k)),
            pl.BlockSpec((ATT_HP, SEG_ROWS, HEAD_DIM), lambda n, k, t: (n * nhp + k, 0, 0)),
            pl.BlockSpec((ATT_HP, SEG_ROWS, HEAD_DIM), lambda n, k, t: (n * nhp + k, 0, 0)),
            kvspec(C_PAG, 2), kvspec(C_PAG, 3), kvspec(C_WIN, 0), kvspec(C_WIN, 1),
            pl.BlockSpec((1, ATT_HP * GROUP, 128, 128), lambda n, k, t: (t, k, 0, 0)),
            pl.BlockSpec((MASKED_TILE + 1, ATT_HP * GROUP, 128, 128), lambda n, k, t: (0, k, 0, 0)),
            pl.BlockSpec((SEQ // SEL_BLK, 128), lambda n, k, t: (0, 0)),
            pl.BlockSpec((SEQ // 256, 128, 256), lambda n, k, t: (0, 0, 0)),
        ],
        out_specs=pl.BlockSpec((128, qw), lambda n, k, t: (n * nq + t, k)),
        out_shape=jax.ShapeDtypeStruct((M_PROMPT, N_HEADS * HEAD_DIM), BF16),
        scratch_shapes=[pltpu.VMEM((SEQ, kw), BF16)] * 4 + [
            pltpu.VMEM((ATT_HP, SEQ // 256, 128, 256), F32),
            pltpu.VMEM((ATT_HP, SEQ // 256, rows, 256), F32),
            pltpu.VMEM((ATT_HP, rows, 128), F32), pltpu.VMEM((ATT_HP, rows, 128), F32),
            pltpu.VMEM((ATT_HP, rows, HEAD_DIM), F32)],
        compiler_params=_cparams(3),
        name="attn_prompt",
    )(z, z, ck, cv, z, z, z, z, bias_cmp, tz, ov, emat)


S_ROWS = GROUP * N_KV * DEC_SEQ


ATT_SS = 2


def _attn_sample_body(pt_ref, q_ref, gs_ref, ck_ref, cv_ref, *rest):
    n_pg = ATT_SS * 2 * N_PAGES
    page_refs = rest[:n_pg]
    win_refs = rest[n_pg:n_pg + 2 * ATT_SS]
    (nkv_ref, nwin_ref, bcmp_ref, bsel_ref, bnew_ref, bwin_ref, ov_ref, e_ref,
     o_ref, s_ref, nk_ref, nw_ref) = rest[n_pg + 2 * ATT_SS:]
    kv_of_row = jnp.bitwise_and(jnp.right_shift(lax.broadcasted_iota(jnp.int32, (S_ROWS, 1), 0), 2), N_KV - 1)
    col = lambda k, half: pl.ds(half * N_KV * HEAD_DIM + k * HEAD_DIM, HEAD_DIM)
    head_rows = _kv_head_rows
    rowmax = lambda s: jnp.max(s, axis=-1, keepdims=True)
    rowsum = lambda s: jnp.sum(s, axis=-1, keepdims=True)

    @pl.when(pl.program_id(0) == 0)
    def _():
        nk_ref[...] = jnp.zeros_like(nk_ref)
        nw_ref[...] = jnp.zeros_like(nw_ref)

    def one_sequence(i):
        pages = page_refs[i * 2 * N_PAGES:(i + 1) * 2 * N_PAGES]
        kwin_ref, vwin_ref = win_refs[2 * i], win_refs[2 * i + 1]
        qq = (q_ref[i] * Q_SCALE).astype(BF16)

        def logits(get_k):
            out = None
            for k in range(N_KV):
                s = jnp.where(kv_of_row == k, _dot_nt(qq, get_k(k).astype(BF16)), 0.0)
                out = s if out is None else out + s
            return out

        def weighted(pe, get_v):
            out = None
            for k in range(N_KV):
                o = _dot(jnp.where(kv_of_row == k, pe, 0.0).astype(BF16), get_v(k).astype(BF16))
                out = o if out is None else out + o
            return out

        nk_ref[i, 0:8, :] = nkv_ref[i]
        nw_ref[i, 0:8, :] = nwin_ref[i]

        pc = _softmax_rows(logits(lambda k: ck_ref[i, pl.ds(k * SEG_ROWS, SEG_ROWS), :]) + bcmp_ref[...])
        o_c = weighted(pc, lambda k: cv_ref[i, pl.ds(k * SEG_ROWS, SEG_ROWS), :])
        ps = pc + pltpu.roll(pc, 16, axis=0) + pltpu.roll(pc, 32, axis=0) + pltpu.roll(pc, 48, axis=0)
        score = _dot_split3(ps, ov_ref[...])
        n_blk = -(-(PAST_LEN + DEC_SEQ) // SEL_BLK)
        cur = jnp.full((S_ROWS, 128), PAST_LEN // SEL_BLK, jnp.int32)
        sel = _select_blocks(score, cur, n_blk)
        key_mask = (_dot(sel, e_ref[...]) - 1.0) * BIG

        m = jnp.full((S_ROWS, 1), M_FLOOR, F32)
        for p in range(N_PAGES):
            s = (logits(lambda k: head_rows(pages[2 * p], k)) + bsel_ref[p]
                 + key_mask[:, p * PAGE_SIZE:(p + 1) * PAGE_SIZE])
            s_ref[i, p] = s
            m = jnp.maximum(m, rowmax(s))
        sn = logits(lambda k: nk_ref[i, :, col(k, 0)]) + bnew_ref[...]
        m = jnp.maximum(m, rowmax(sn))
        pn = jnp.exp(sn - m)
        l = rowsum(pn)
        acc = weighted(pn, lambda k: nk_ref[i, :, col(k, 1)])
        for p in range(N_PAGES):
            pe = jnp.exp(s_ref[i, p] - m)
            l = l + rowsum(pe)
            acc = acc + weighted(pe, lambda k: head_rows(pages[2 * p + 1], k))
        o_s = acc / jnp.where(l > 0.0, l, 1.0)

        sw = logits(lambda k: head_rows(kwin_ref, k)) + bwin_ref[...]
        sn = logits(lambda k: nw_ref[i, :, col(k, 0)]) + bnew_ref[...]
        m = jnp.maximum(jnp.maximum(rowmax(sw), rowmax(sn)), M_FLOOR)
        pw = jnp.exp(sw - m)
        pn = jnp.exp(sn - m)
        l = rowsum(pw) + rowsum(pn)
        o_w = weighted(pw, lambda k: head_rows(vwin_ref, k)) + weighted(pn, lambda k: nw_ref[i, :, col(k, 1)])
        o_w = o_w / jnp.where(l > 0.0, l, 1.0)

        gates = jax.nn.sigmoid(gs_ref[i])
        o_ref[i] = gates[:, 0:1] * o_c + gates[:, 1:2] * o_s + gates[:, 2:3] * o_w

    for i in range(ATT_SS):
        one_sequence(i)


def _attn_sample(pt_flat, q_s, g_s, ck, cv, cache_kv, nkv, state_win, nwin, bcmp, bsel, bnew, bwin, ov, emat):
    const2 = lambda shape: pl.BlockSpec(shape, lambda n, pt: (0, 0))
    per_seq = lambda rows, width: pl.BlockSpec((ATT_SS, rows, width), lambda n, pt: (n, 0, 0))
    page_specs = [pl.BlockSpec((None, None, PAGE_SIZE, None, N_KV, HEAD_DIM),
                               lambda n, pt, i=i, p=p, kind=kind:
                               (0, pt[(n * ATT_SS + i) * N_PAGES + p], 0, kind, 0, 0))
                  for i in range(ATT_SS) for p in range(N_PAGES) for kind in (2, 3)]
    win_specs = [pl.BlockSpec((None, None, WINDOW, None, N_KV, HEAD_DIM),
                              lambda n, pt, i=i, kind=kind: (0, n * ATT_SS + i, 0, kind, 0, 0))
                 for i in range(ATT_SS) for kind in (0, 1)]
    grid_spec = pltpu.PrefetchScalarGridSpec(
        num_scalar_prefetch=1,
        grid=(DEC_BATCH // ATT_SS,),
        in_specs=[
            per_seq(S_ROWS, HEAD_DIM), per_seq(S_ROWS, 128),
            per_seq(N_KV * SEG_ROWS, HEAD_DIM), per_seq(N_KV * SEG_ROWS, HEAD_DIM),
        ] + page_specs + win_specs + [
            per_seq(8, 1024), per_seq(8, 1024),
            const2((S_ROWS, 128)),
            pl.BlockSpec((N_PAGES, S_ROWS, 128), lambda n, pt: (0, 0, 0)),
            const2((S_ROWS, 128)),
            const2((S_ROWS, WINDOW)),
            const2((128, 128)),
            const2((128, PAST_LEN)),
        ],
        out_specs=per_seq(S_ROWS, HEAD_DIM),
        scratch_shapes=[
            pltpu.VMEM((ATT_SS, N_PAGES, S_ROWS, 128), F32),
            pltpu.VMEM((ATT_SS, 128, 1024), F32), pltpu.VMEM((ATT_SS, 128, 1024), F32),
        ],
    )
    return pl.pallas_call(
        _attn_sample_body,
        grid_spec=grid_spec,
        out_shape=jax.ShapeDtypeStruct((DEC_BATCH, S_ROWS, HEAD_DIM), F32),
        compiler_params=_cparams(1),
        name="attn_sample",
    )(pt_flat, q_s, g_s, ck, cv, *([cache_kv] * len(page_specs)), *([state_win] * len(win_specs)), nkv, nwin,
      bcmp, bsel, bnew, bwin, ov, emat)


def _t5_bucket(dist):
    d = jnp.maximum(dist, 0)
    df = jnp.maximum(d, 1).astype(F32)
    large = MAX_EXACT + (jnp.log(df / MAX_EXACT) / math.log(MAX_DIST / MAX_EXACT)
                         * (N_BUCKETS - MAX_EXACT)).astype(jnp.int32)
    large = jnp.minimum(large, N_BUCKETS - 1)
    return jnp.where(d < MAX_EXACT, d, large)


def _bias_lookup_body(rb_ref, idx_ref, o_ref):
    idx = idx_ref[0]
    for h in range(N_HEADS):
        acc = jnp.full(idx.shape, NEG, F32)
        for b in range(N_BUCKETS):
            acc = jnp.where(idx == b, rb_ref[b * N_HEADS + h], acc)
        o_ref[0, h] = acc


def _bias_table(rel_bias, dist, valid, name):
    p, r, _ = dist.shape
    idx = jnp.where(jnp.asarray(valid), _t5_bucket(jnp.asarray(dist, jnp.int32)), -1)
    return pl.pallas_call(
        _bias_lookup_body,
        grid=(p,),
        in_specs=[pl.BlockSpec(memory_space=pltpu.SMEM), pl.BlockSpec((1, r, 128), lambda i: (i, 0, 0))],
        out_specs=pl.BlockSpec((1, N_HEADS, r, 128), lambda i: (i, 0, 0, 0)),
        out_shape=jax.ShapeDtypeStruct((p, N_HEADS, r, 128), F32),
        compiler_params=_cparams(1),
        name=name,
    )(rel_bias.astype(F32).reshape(-1), idx)


def _overlap(nc, nb):
    cs = np.arange(nc)[:, None] * CMP_STRIDE
    js = np.arange(nb)[None, :] * SEL_BLK
    ov = np.clip(np.minimum(cs + CMP_LEN, js + SEL_BLK) - np.maximum(cs, js), 0, None) / CMP_LEN
    out = np.zeros((128, 128), np.float32)
    out[:nc, :nb] = ov
    return jnp.asarray(out, BF16)


def _position_tables(rel_bias):
    nc = SEG_ROWS - 1
    t = np.arange(128)[None, :, None]
    c = np.arange(128)[None, None, :]
    cend = c * CMP_STRIDE + CMP_LEN - 1
    d = np.arange(MASKED_TILE + 1)[:, None, None] * 128 + t - c
    tz = _bias_table(rel_bias, d, (d >= 0) & (d < WINDOW), "bias_tiles")
    d = np.arange(SEQ // 128)[:, None, None] * 128 + t - cend
    bias_cmp = _bias_table(rel_bias, d, (d >= 0) & (c < nc), "bias_cmp")
    ts = np.arange(8)[None, :, None]
    qpos = PAST_LEN + ts
    live = ts < DEC_SEQ
    d_cmp = qpos - cend
    d_sel = qpos - (np.arange(N_PAGES)[:, None, None] * PAGE_SIZE + c)
    d_new = ts - c
    d_win = qpos - (PAST_LEN - WINDOW + np.arange(WINDOW // 128)[:, None, None] * 128 + c)
    d = np.concatenate([d_cmp, d_sel, d_new, d_win], axis=0)
    valid = np.concatenate([(d_cmp >= 0) & (c < nc), d_sel >= 0, (d_new >= 0) & (c < DEC_SEQ),
                            (d_win >= 0) & (d_win < WINDOW)], axis=0) & live
    o = _bias_table(rel_bias, d, valid, "bias_sample")[:, :, :DEC_SEQ]
    o = jnp.transpose(o.reshape(-1, N_KV, GROUP, DEC_SEQ, 128), (0, 2, 1, 3, 4)).reshape(-1, S_ROWS, 128)
    bcmp, bsel, bnew = o[0], o[1:1 + N_PAGES], o[1 + N_PAGES]
    bwin = jnp.transpose(o[2 + N_PAGES:], (1, 0, 2)).reshape(S_ROWS, WINDOW)
    keys = np.arange(SEQ)
    e_all = (np.arange(128)[:, None] == (keys // SEL_BLK)[None, :]).astype(np.float32)
    e_tiles = jnp.asarray(e_all.reshape(128, SEQ // 256, 256).transpose(1, 0, 2), BF16)
    ovt_p = jnp.transpose(_overlap(nc, SEQ // SEL_BLK))[:SEQ // SEL_BLK]
    return dict(tz=tz, bias_cmp=bias_cmp, bcmp=bcmp, bsel=bsel, bnew=bnew, bwin=bwin,
                ovt_p=ovt_p, ov_s=_overlap(nc, -(-(PAST_LEN + DEC_SEQ) // SEL_BLK)),
                e_tiles=e_tiles, e_all=jnp.asarray(e_all, BF16))


def _permute_w_in(w_in):
    c0 = 2 * D_RNN + N_HEADS * HEAD_DIM + 6 * N_KV * HEAD_DIM
    g_nsa = w_in[:, c0:c0 + 3 * N_HEADS].reshape(D_MODEL, N_KV, GROUP, 3)
    g_nsa = jnp.transpose(g_nsa, (0, 1, 3, 2)).reshape(D_MODEL, N_KV, 3 * GROUP)
    g_nsa = jnp.pad(g_nsa, ((0, 0), (0, 0), (0, 128 - 3 * GROUP))).reshape(D_MODEL, N_KV * 128)
    return jnp.concatenate([w_in[:, :c0], w_in[:, c0 + 3 * N_HEADS:], g_nsa], axis=1).astype(BF16)


def _cmp_weights(w1_k, w1_v, b1_k, b1_v, w2_k, w2_v, pos):
    def cat(w1):
        w = w1.reshape(2, CMP_STRIDE * HEAD_DIM, CMP_HID)
        return jnp.concatenate([w[0], w[1]], axis=1).reshape(N_PAIR, 256, 2 * CMP_HID)
    w1 = jnp.stack([cat(w1_k), cat(w1_v)]).astype(BF16)
    posm = jnp.pad(pos.reshape(2, CMP_STRIDE * HEAD_DIM), ((0, 6), (0, 0)))
    b1 = jnp.stack([b1_k, b1_v]).reshape(2, 1, CMP_HID)
    w2 = jnp.stack([w2_k, w2_v]).astype(BF16)
    return w1, posm, b1, w2


def kernel(x_prompt, x_sample, cache_kv, page_table, state_win, state_conv, state_h, rel_bias, ln_final, ln_ffn1, w_ffn1_gate, w_ffn1_up, w_ffn1_down, ln_mix, w_in, conv_w, conv_b, rg_wa, rg_ba, rg_wi, rg_bi, rg_lambda, cmp_pos, cmp_k_w1, cmp_k_b1, cmp_k_w2, cmp_v_w1, cmp_v_b1, cmp_v_w2, w_br_rnn, w_br_attn, w_out, ln_ffn2, w_ffn2_gate, w_ffn2_up, w_ffn2_down):
    tabs = _position_tables(rel_bias)
    x = jnp.concatenate([x_prompt.reshape(M_PROMPT, D_MODEL), x_sample.reshape(M_SAMPLE, D_MODEL)], axis=0)

    x = _ffn(x, ln_ffn1[0], w_ffn1_gate[0].astype(BF16), w_ffn1_up[0].astype(BF16), w_ffn1_down[0].astype(BF16),
             ln_final, False)
    z = _in_proj(x, ln_mix[0], _permute_w_in(w_in[0]))
    z_s = z[M_PROMPT:]

    vec = lambda v: v.reshape(1, D_RNN)
    rnn_w = (conv_w[0], vec(conv_b[0]), rg_wa[0].astype(BF16), vec(rg_ba[0]), rg_wi[0].astype(BF16),
             vec(rg_bi[0]), vec(rg_lambda[0]))
    g_p, h_p = _rnn_prompt(z, *rnn_w)
    tmajor = lambda a: jnp.transpose(a.reshape(DEC_BATCH, -1, D_RNN), (1, 0, 2))
    g_s, h_s = _rnn_sample(tmajor(z_s[:, C_UGATE:C_UGATE + D_RNN]), tmajor(z_s[:, C_UX:C_UX + D_RNN]),
                           tmajor(state_conv[0]), state_h[0], *rnn_w)
    grnn = jnp.concatenate([g_p, jnp.transpose(g_s, (1, 0, 2)).reshape(M_SAMPLE, D_RNN)], axis=0)

    cw = _cmp_weights(cmp_k_w1[0], cmp_v_w1[0], cmp_k_b1[0], cmp_v_b1[0], cmp_k_w2[0], cmp_v_w2[0], cmp_pos[0])
    pt_prompt = jnp.arange(BATCH * N_PAGES, dtype=jnp.int32)
    pt_sample = page_table.reshape(-1).astype(jnp.int32)
    ck_p, cv_p = _compress(z.reshape(M_TOK // PAGE_SIZE, PAGE_SIZE, D_Z), pt_prompt, C_PAG // 1024, BATCH, *cw,
                           name="compress_prompt")
    ck_s, cv_s = _compress(cache_kv, pt_sample, None, DEC_BATCH, *cw, name="compress_sample")
    o_p = _attn_prompt(z, ck_p.reshape(BATCH * N_KV, SEG_ROWS, HEAD_DIM), cv_p.reshape(BATCH * N_KV, SEG_ROWS, HEAD_DIM),
                       tabs["bias_cmp"], tabs["tz"], tabs["ovt_p"], tabs["e_tiles"])

    def rows_gkt(a, width):
        a = a.reshape(DEC_BATCH, DEC_SEQ, N_KV, GROUP, width)
        return jnp.transpose(a, (0, 3, 2, 1, 4)).reshape(DEC_BATCH, S_ROWS, width)

    q_s = rows_gkt(z_s[:, C_Q:C_Q + N_HEADS * HEAD_DIM], HEAD_DIM)
    gn = z_s[:, C_GNSA:].reshape(M_SAMPLE, N_KV, 128)[:, :, :3 * GROUP].reshape(M_SAMPLE, N_KV, 3, GROUP)
    g_s3 = jnp.pad(rows_gkt(jnp.transpose(gn, (0, 1, 3, 2)), 3), ((0, 0), (0, 0), (0, 125)))
    pad8 = lambda a: jnp.pad(a.reshape(DEC_BATCH, DEC_SEQ, -1), ((0, 0), (0, 8 - DEC_SEQ), (0, 0)))
    nkv = pad8(z_s[:, C_PAG + 2 * N_KV * HEAD_DIM:C_PAG + 4 * N_KV * HEAD_DIM])
    nwin = pad8(z_s[:, C_WIN:C_WIN + 2 * N_KV * HEAD_DIM])
    o_s = _attn_sample(pt_sample, q_s, g_s3, ck_s, cv_s, cache_kv, nkv, state_win, nwin,
                       tabs["bcmp"], tabs["bsel"], tabs["bnew"], tabs["bwin"], tabs["ov_s"], tabs["e_all"])
    o_s = jnp.transpose(o_s.reshape(DEC_BATCH, GROUP, N_KV, DEC_SEQ, HEAD_DIM), (0, 3, 2, 1, 4))
    oattn = jnp.concatenate([o_p, o_s.reshape(M_SAMPLE, N_HEADS * HEAD_DIM).astype(BF16)], axis=0)

    merged = _merge(z, grnn, oattn, w_br_rnn[0].astype(BF16), w_br_attn[0].astype(BF16))
    x = _out_proj(x, merged, w_out[0].astype(BF16))
    y = _ffn(x, ln_ffn2[0], w_ffn2_gate[0].astype(BF16), w_ffn2_up[0].astype(BF16), w_ffn2_down[0].astype(BF16),
             ln_final, True)

    kv = z[:, C_PAG:C_PAG + 4 * N_KV * HEAD_DIM]
    wn = z[:, C_WIN:C_WIN + 2 * N_KV * HEAD_DIM]
    conv_rows = lambda zz, t: zz.reshape(-1, t, D_Z)[:, t - (CONV_W - 1):, C_UX:C_UX + D_RNN]
    win_p = wn[:M_PROMPT].reshape(BATCH, SEQ, 2, N_KV, HEAD_DIM)[:, SEQ - WINDOW:]
    win_s = jnp.concatenate([state_win.reshape(DEC_BATCH, WINDOW, 2, N_KV, HEAD_DIM),
                             wn[M_PROMPT:].reshape(DEC_BATCH, DEC_SEQ, 2, N_KV, HEAD_DIM)], axis=1)
    return (
        y[:M_PROMPT].reshape(BATCH, SEQ, D_MODEL),
        y[M_PROMPT:].reshape(DEC_BATCH, DEC_SEQ, D_MODEL),
        kv[:M_PROMPT].reshape(1, BATCH, SEQ, 4, N_KV, HEAD_DIM),
        kv[M_PROMPT:].reshape(1, DEC_BATCH, DEC_SEQ, 4, N_KV, HEAD_DIM),
        win_p[None],
        win_s[None, :, DEC_SEQ:],
        conv_rows(z[:M_PROMPT], SEQ)[None],
        conv_rows(z[M_PROMPT:], DEC_SEQ)[None],
        h_p[None, :, 7],
        h_s[None],
    )
```

```python
import functools
import math

import numpy as np
import jax
import jax.numpy as jnp
from jax import lax
from jax.experimental import pallas as pl
from jax.experimental.pallas import tpu as pltpu

F32 = jnp.float32
BF16 = jnp.bfloat16

D_MODEL = 4096
BATCH = 4
SEQ = 2048
DEC_BATCH = 128
DEC_SEQ = 4
PAST_LEN = 2048
PAGE_SIZE = 128
N_PAGES = PAST_LEN // PAGE_SIZE
D_RNN = D_MODEL // 2
RNN_BLOCKS = 16
RNN_BW = D_RNN // RNN_BLOCKS
CONV_W = 4
LRU_C = 8.0
N_HEADS = 16
HEAD_DIM = 128
N_KV = 4
GROUP = N_HEADS // N_KV
CMP_LEN = 32
CMP_STRIDE = 16
CMP_HID = 2 * HEAD_DIM
SEL_BLK = 64
N_SEL = 8
WINDOW = 512
N_BUCKETS = 32
MAX_EXACT = 16
MAX_DIST = 128
D_FF = ((8 * D_MODEL // 3 + 255) // 256) * 256
EPS = 1e-6
NEG = -1e30
BIG = 1e30
M_FLOOR = -1e29
Q_SCALE = HEAD_DIM ** -0.5

M_PROMPT = BATCH * SEQ
M_SAMPLE = DEC_BATCH * DEC_SEQ
M_TOK = M_PROMPT + M_SAMPLE

C_UGATE = 0
C_UX = C_UGATE + D_RNN
C_Q = C_UX + D_RNN
C_PAG = C_Q + N_HEADS * HEAD_DIM
C_WIN = C_PAG + 4 * N_KV * HEAD_DIM
C_GRNN = C_WIN + 2 * N_KV * HEAD_DIM
C_GATTN = C_GRNN + D_MODEL
C_GNSA = C_GATTN + D_MODEL
D_Z = C_GNSA + N_KV * 128

TM = 512
TF = 256
TN_IN = 1280
TN_MM = 1024
VMEM_LIMIT = 56 * 2 ** 20


def _cparams(n_axes, vmem=VMEM_LIMIT):
    return pltpu.CompilerParams(dimension_semantics=("arbitrary",) * n_axes, vmem_limit_bytes=vmem)


def _dot(a, b):
    return jnp.dot(a, b, preferred_element_type=F32)


def _dot_nt(a, b):
    return lax.dot_general(a, b, (((1,), (1,)), ((), ())), preferred_element_type=F32)


def _dot_split3(a, b):
    a1 = a.astype(BF16)
    r1 = a - a1.astype(F32)
    a2 = r1.astype(BF16)
    a3 = (r1 - a2.astype(F32)).astype(BF16)
    return _dot(a1, b) + _dot(a2, b) + _dot(a3, b)


def _dot_nt_split3(a, b):
    b1 = b.astype(BF16)
    r1 = b - b1.astype(F32)
    b2 = r1.astype(BF16)
    b3 = (r1 - b2.astype(F32)).astype(BF16)
    return _dot_nt(a, b1) + _dot_nt(a, b2) + _dot_nt(a, b3)


def _kv_head_rows(ref, k):
    n_rows = ref.shape[0]
    return ref.reshape(n_rows * N_KV, HEAD_DIM)[pl.ds(k, n_rows, stride=N_KV), :]


def _rms(x, g):
    return x * lax.rsqrt(jnp.mean(x * x, axis=-1, keepdims=True) + EPS) * g


def _ffn_body(x_ref, ln_ref, wg_ref, wu_ref, wd_ref, lnf_ref, o_ref, xn_ref, *, n_f, final_norm):
    f = pl.program_id(1)

    @pl.when(f == 0)
    def _():
        x = x_ref[...]
        xn_ref[...] = _rms(x, ln_ref[...]).astype(BF16)
        o_ref[...] = 2.0 * x

    xn = xn_ref[...]
    g = _dot(xn, wg_ref[...])
    u = _dot(xn, wu_ref[...])
    h = (g * jax.nn.sigmoid(g) * u).astype(BF16)
    o_ref[...] += _dot(h, wd_ref[...])

    @pl.when(f == n_f - 1)
    def _():
        y = 0.5 * o_ref[...]
        if final_norm:
            y = _rms(y, lnf_ref[...])
        o_ref[...] = y


def _ffn(x, ln, wg, wu, wd, lnf, final_norm, tile0=0, n_tiles=None):
    m = (x.shape[0] // TM if n_tiles is None else n_tiles) * TM
    n_f = D_FF // TF
    return pl.pallas_call(
        functools.partial(_ffn_body, n_f=n_f, final_norm=final_norm),
        grid=(m // TM, n_f),
        in_specs=[
            pl.BlockSpec((TM, D_MODEL), lambda i, f: (tile0 + i, 0), pipeline_mode=pl.Buffered(1)),
            pl.BlockSpec((1, D_MODEL), lambda i, f: (0, 0)),
            pl.BlockSpec((D_MODEL, TF), lambda i, f: (0, f)),
            pl.BlockSpec((D_MODEL, TF), lambda i, f: (0, f)),
            pl.BlockSpec((TF, D_MODEL), lambda i, f: (f, 0)),
            pl.BlockSpec((1, D_MODEL), lambda i, f: (0, 0)),
        ],
        out_specs=pl.BlockSpec((TM, D_MODEL), lambda i, f: (i, 0)),
        out_shape=jax.ShapeDtypeStruct((m, D_MODEL), F32),
        scratch_shapes=[pltpu.VMEM((TM, D_MODEL), BF16)],
        compiler_params=_cparams(2),
        name="ffn",
    )(x, ln.reshape(1, D_MODEL), wg, wu, wd, lnf.reshape(1, D_MODEL))


def _in_proj_body(x_ref, ln_ref, w_ref, o_ref, xn_ref):
    @pl.when(pl.program_id(1) == 0)
    def _():
        xn_ref[...] = _rms(x_ref[...], ln_ref[...]).astype(BF16)

    o_ref[...] = _dot(xn_ref[...], w_ref[...])


def _in_proj(x, ln, w):
    m = x.shape[0]
    return pl.pallas_call(
        _in_proj_body,
        grid=(m // TM, D_Z // TN_IN),
        in_specs=[
            pl.BlockSpec((TM, D_MODEL), lambda i, j: (i, 0), pipeline_mode=pl.Buffered(1)),
            pl.BlockSpec((1, D_MODEL), lambda i, j: (0, 0)),
            pl.BlockSpec((D_MODEL, TN_IN), lambda i, j: (0, j)),
        ],
        out_specs=pl.BlockSpec((TM, TN_IN), lambda i, j: (i, j)),
        out_shape=jax.ShapeDtypeStruct((m, D_Z), F32),
        scratch_shapes=[pltpu.VMEM((TM, D_MODEL), BF16)],
        compiler_params=_cparams(2),
        name="in_proj",
    )(x, ln.reshape(1, D_MODEL), w)


def _merge_body(gr_ref, oa_ref, wr_ref, wa_ref, zr_ref, za_ref, o_ref):
    y_rnn = _dot(gr_ref[...], wr_ref[...])
    y_attn = _dot(oa_ref[...], wa_ref[...])
    o_ref[...] = (jax.nn.sigmoid(zr_ref[...]) * y_rnn + jax.nn.sigmoid(za_ref[...]) * y_attn).astype(BF16)


def _merge(z, grnn, oattn, w_rnn, w_attn):
    m = z.shape[0]
    cr, ca = C_GRNN // TN_MM, C_GATTN // TN_MM
    return pl.pallas_call(
        _merge_body,
        grid=(m // TM, D_MODEL // TN_MM),
        in_specs=[
            pl.BlockSpec((TM, D_RNN), lambda i, j: (i, 0)),
            pl.BlockSpec((TM, N_HEADS * HEAD_DIM), lambda i, j: (i, 0)),
            pl.BlockSpec((D_RNN, TN_MM), lambda i, j: (0, j)),
            pl.BlockSpec((N_HEADS * HEAD_DIM, TN_MM), lambda i, j: (0, j)),
            pl.BlockSpec((TM, TN_MM), lambda i, j: (i, cr + j)),
            pl.BlockSpec((TM, TN_MM), lambda i, j: (i, ca + j)),
        ],
        out_specs=pl.BlockSpec((TM, TN_MM), lambda i, j: (i, j)),
        out_shape=jax.ShapeDtypeStruct((m, D_MODEL), BF16),
        compiler_params=_cparams(2),
        name="merge",
    )(grnn, oattn, w_rnn, w_attn, z, z)


def _out_proj_body(a_ref, w_ref, x_ref, o_ref):
    o_ref[...] = x_ref[...] + _dot(a_ref[...], w_ref[...])


def _out_proj(x, a, w):
    m = x.shape[0]
    return pl.pallas_call(
        _out_proj_body,
        grid=(m // TM, D_MODEL // TN_MM),
        in_specs=[
            pl.BlockSpec((TM, D_MODEL), lambda i, j: (i, 0)),
            pl.BlockSpec((D_MODEL, TN_MM), lambda i, j: (0, j)),
            pl.BlockSpec((TM, TN_MM), lambda i, j: (i, j)),
        ],
        out_specs=pl.BlockSpec((TM, TN_MM), lambda i, j: (i, j)),
        out_shape=jax.ShapeDtypeStruct((m, D_MODEL), F32),
        compiler_params=_cparams(2),
        name="out_proj",
    )(a, w, x)


def _softplus(v):
    return jnp.maximum(v, 0.0) + jnp.log1p(jnp.exp(-jnp.abs(v)))


def _lru_coeffs(xc, wa_ref, ba, wi_ref, bi, sp, n_blk):
    xb = xc.astype(BF16)
    ra = jnp.concatenate([_dot(xb[:, b * RNN_BW:(b + 1) * RNN_BW], wa_ref[b]) for b in range(n_blk)], axis=1)
    ia = jnp.concatenate([_dot(xb[:, b * RNN_BW:(b + 1) * RNN_BW], wi_ref[b]) for b in range(n_blk)], axis=1)
    r = jax.nn.sigmoid(ra + ba)
    i = jax.nn.sigmoid(ia + bi)
    log_a = -LRU_C * r * sp
    a = jnp.exp(log_a)
    bt = jnp.sqrt(-jnp.tanh(log_a) * (a * a + 1.0)) * (i * xc)
    return a, bt


RNN_TC = 256


def _rnn_prompt_body(ug_ref, ux_ref, cw_ref, cb_ref, wa_ref, ba_ref, wi_ref, bi_ref, lam_ref,
                     g_ref, h_ref, tail_ref, hc_ref):
    c = pl.program_id(1)
    tc = RNN_TC

    @pl.when(c == 0)
    def _():
        tail_ref[...] = jnp.zeros_like(tail_ref)
        hc_ref[...] = jnp.zeros_like(hc_ref)

    u = ux_ref[...]
    tail = tail_ref[...]
    row8 = lax.broadcasted_iota(jnp.int32, (8, D_RNN), 0)
    xc = cb_ref[...] + cw_ref[CONV_W - 1:CONV_W, :] * u
    for j in range(1, CONV_W):
        r = pltpu.roll(u, j, axis=0)
        first = jnp.where(row8 >= j, r[0:8], pltpu.roll(tail, j, axis=0))
        shifted = jnp.concatenate([first, r[8:]], axis=0)
        xc = xc + cw_ref[CONV_W - 1 - j:CONV_W - j, :] * shifted
    tail_ref[...] = u[tc - 8:tc]

    a, bt = _lru_coeffs(xc, wa_ref, ba_ref[...], wi_ref, bi_ref[...], _softplus(-lam_ref[...]), RNN_BLOCKS)

    row = lax.broadcasted_iota(jnp.int32, (tc, D_RNN), 0)
    s = 1
    while s < tc:
        keep = row >= s
        a_sh = jnp.where(keep, pltpu.roll(a, s, axis=0), 1.0)
        b_sh = jnp.where(keep, pltpu.roll(bt, s, axis=0), 0.0)
        bt = a * b_sh + bt
        a = a * a_sh
        s *= 2
    h = bt + a * hc_ref[7:8, :]
    hc_ref[...] = h[tc - 8:tc]
    g_ref[...] = (h * jax.nn.gelu(ug_ref[...])).astype(BF16)

    @pl.when(c == pl.num_programs(1) - 1)
    def _():
        h_ref[0] = h[tc - 8:tc]


def _rnn_prompt(z, cw, cb, wa, ba, wi, bi, lam):
    nc = SEQ // RNN_TC
    vec = lambda: pl.BlockSpec((1, D_RNN), lambda n, c: (0, 0))
    blk = lambda: pl.BlockSpec((RNN_BLOCKS, RNN_BW, RNN_BW), lambda n, c: (0, 0, 0))
    return pl.pallas_call(
        _rnn_prompt_body,
        grid=(BATCH, nc),
        in_specs=[
            pl.BlockSpec((RNN_TC, D_RNN), lambda n, c: (n * nc + c, C_UGATE // D_RNN)),
            pl.BlockSpec((RNN_TC, D_RNN), lambda n, c: (n * nc + c, C_UX // D_RNN)),
            pl.BlockSpec((CONV_W, D_RNN), lambda n, c: (0, 0)),
            vec(), blk(), vec(), blk(), vec(), vec(),
        ],
        out_specs=[
            pl.BlockSpec((RNN_TC, D_RNN), lambda n, c: (n * nc + c, 0)),
            pl.BlockSpec((1, 8, D_RNN), lambda n, c: (n, 0, 0)),
        ],
        out_shape=[jax.ShapeDtypeStruct((M_PROMPT, D_RNN), BF16),
                   jax.ShapeDtypeStruct((BATCH, 8, D_RNN), F32)],
        scratch_shapes=[pltpu.VMEM((8, D_RNN), F32), pltpu.VMEM((8, D_RNN), F32)],
        compiler_params=_cparams(2),
        name="rnn_prompt",
    )(z, z, cw, cb, wa, ba, wi, bi, lam)


RNN_SC = 512


def _rnn_sample_body(ug_ref, ux_ref, buf_ref, h0_ref, cw_ref, cb_ref, wa_ref, ba_ref, wi_ref, bi_ref, lam_ref,
                     g_ref, h_ref):
    full = [buf_ref[j] for j in range(CONV_W - 1)] + [ux_ref[t] for t in range(DEC_SEQ)]
    sp = _softplus(-lam_ref[...])
    h = h0_ref[...]
    for t in range(DEC_SEQ):
        xc = cb_ref[...]
        for k in range(CONV_W):
            xc = xc + full[t + k] * cw_ref[k:k + 1, :]
        a, bt = _lru_coeffs(xc, wa_ref, ba_ref[...], wi_ref, bi_ref[...], sp, RNN_SC // RNN_BW)
        h = a * h + bt
        g_ref[t] = (h * jax.nn.gelu(ug_ref[t])).astype(BF16)
    h_ref[...] = h


def _rnn_sample(ug_t, ux_t, buf_t, h0, cw, cb, wa, ba, wi, bi, lam):
    nb = RNN_SC // RNN_BW
    vec = lambda: pl.BlockSpec((1, RNN_SC), lambda c: (0, c))
    blk = lambda: pl.BlockSpec((nb, RNN_BW, RNN_BW), lambda c: (c, 0, 0))
    return pl.pallas_call(
        _rnn_sample_body,
        grid=(D_RNN // RNN_SC,),
        in_specs=[
            pl.BlockSpec((DEC_SEQ, DEC_BATCH, RNN_SC), lambda c: (0, 0, c)),
            pl.BlockSpec((DEC_SEQ, DEC_BATCH, RNN_SC), lambda c: (0, 0, c)),
            pl.BlockSpec((CONV_W - 1, DEC_BATCH, RNN_SC), lambda c: (0, 0, c)),
            pl.BlockSpec((DEC_BATCH, RNN_SC), lambda c: (0, c)),
            pl.BlockSpec((CONV_W, RNN_SC), lambda c: (0, c)),
            vec(), blk(), vec(), blk(), vec(), vec(),
        ],
        out_specs=[
            pl.BlockSpec((DEC_SEQ, DEC_BATCH, RNN_SC), lambda c: (0, 0, c)),
            pl.BlockSpec((DEC_BATCH, RNN_SC), lambda c: (0, c)),
        ],
        out_shape=[jax.ShapeDtypeStruct((DEC_SEQ, DEC_BATCH, D_RNN), BF16),
                   jax.ShapeDtypeStruct((DEC_BATCH, D_RNN), F32)],
        compiler_params=_cparams(1),
        name="rnn_sample",
    )(ug_t, ux_t, buf_t, h0, cw, cb, wa, ba, wi, bi, lam)


N_SEG = PAGE_SIZE // CMP_STRIDE
SEG_ROWS = N_PAGES * N_SEG
N_PAIR = CMP_STRIDE // 2


CMP_PG = 4
SLAB_PITCH = 24


def _compress_body(pt_ref, *refs, paged):
    n_in = CMP_PG * (2 if paged else 1)
    page_refs = refs[:n_in]
    w1_ref, pos_ref, b1_ref, w2_ref, ck_ref, cv_ref, stage_ref, pterm_ref, slab_ref = refs[n_in:]
    n = pl.program_id(0)
    q = pl.program_id(1)

    @pl.when((n == 0) & (q == 0))
    def _():
        for kind in range(2):
            acc = jnp.zeros((8, 2 * CMP_HID), F32)
            for pr in range(N_PAIR):
                acc = acc + _dot(pos_ref[:, pr * 256:(pr + 1) * 256].astype(BF16), w1_ref[kind, pr])
            pterm_ref[kind] = acc

    for j in range(CMP_PG):
        for kind in range(2):
            for k in range(N_KV):
                if paged:
                    slab = _kv_head_rows(page_refs[j * 2 + kind], k)
                else:
                    kk = kind * N_KV + k
                    slab = page_refs[j][:, kk * HEAD_DIM:(kk + 1) * HEAD_DIM]
                for s in range(N_SEG):
                    slab_ref[(j * 2 + kind) * N_KV + k, pl.ds(s * SLAB_PITCH, CMP_STRIDE), :] = (
                        slab[s * CMP_STRIDE:(s + 1) * CMP_STRIDE])

    for j in range(CMP_PG):
        seg0 = (q * CMP_PG + j) * N_SEG
        for kind in range(2):
            for k in range(N_KV):
                for l in range(CMP_STRIDE):
                    piece = slab_ref[(j * 2 + kind) * N_KV + k, pl.ds(l, N_SEG, stride=SLAB_PITCH), :]
                    stage_ref[kind, l // 2, pl.ds(k * SEG_ROWS + seg0, N_SEG),
                              pl.ds((l % 2) * HEAD_DIM, HEAD_DIM)] = piece

    @pl.when(q == N_PAGES // CMP_PG - 1)
    def _():
        for kind, out_ref in ((0, ck_ref), (1, cv_ref)):
            acc = jnp.zeros((N_KV * SEG_ROWS, 2 * CMP_HID), F32)
            for pr in range(N_PAIR):
                acc = acc + _dot(stage_ref[kind, pr].astype(BF16), w1_ref[kind, pr])
            nxt = pltpu.roll(acc[:, CMP_HID:], N_KV * SEG_ROWS - 1, axis=0)
            pt = pterm_ref[kind]
            posterm = pt[0:1, :CMP_HID] + pt[1:2, CMP_HID:] + b1_ref[kind]
            hid = acc[:, :CMP_HID] + nxt + posterm
            out_ref[0] = _dot(jax.nn.gelu(hid).astype(BF16), w2_ref[kind]).astype(BF16)


def _compress(src, pt_flat, col_blk, n_seq, w1, pos, b1, w2, name):
    paged = col_blk is None
    page_of = lambda n, p, pt, j: pt[n * N_PAGES + p * CMP_PG + j]
    out_spec = pl.BlockSpec((1, N_KV * SEG_ROWS, HEAD_DIM), lambda n, p, pt: (n, 0, 0))
    if paged:
        page_specs = [pl.BlockSpec((None, None, PAGE_SIZE, None, N_KV, HEAD_DIM),
                                   lambda n, p, pt, j=j, kind=kind: (0, page_of(n, p, pt, j), 0, kind, 0, 0))
                      for j in range(CMP_PG) for kind in range(2)]
    else:
        page_specs = [pl.BlockSpec((None, PAGE_SIZE, 2 * N_KV * HEAD_DIM),
                                   lambda n, p, pt, j=j: (page_of(n, p, pt, j), 0, col_blk))
                      for j in range(CMP_PG)]
    grid_spec = pltpu.PrefetchScalarGridSpec(
        num_scalar_prefetch=1,
        grid=(n_seq, N_PAGES // CMP_PG),
        in_specs=page_specs + [
            pl.BlockSpec((2, N_PAIR, 256, 2 * CMP_HID), lambda n, p, pt: (0, 0, 0, 0)),
            pl.BlockSpec((8, CMP_STRIDE * HEAD_DIM), lambda n, p, pt: (0, 0)),
            pl.BlockSpec((2, 1, CMP_HID), lambda n, p, pt: (0, 0, 0)),
            pl.BlockSpec((2, CMP_HID, HEAD_DIM), lambda n, p, pt: (0, 0, 0)),
        ],
        out_specs=[out_spec, out_spec],
        scratch_shapes=[pltpu.VMEM((2, N_PAIR, N_KV * SEG_ROWS, 256), F32),
                        pltpu.VMEM((2, 8, 2 * CMP_HID), F32),
                        pltpu.VMEM((CMP_PG * 2 * N_KV, N_SEG * SLAB_PITCH, HEAD_DIM), F32)],
    )
    shp = jax.ShapeDtypeStruct((n_seq, N_KV * SEG_ROWS, HEAD_DIM), BF16)
    return pl.pallas_call(
        functools.partial(_compress_body, paged=paged),
        grid_spec=grid_spec,
        out_shape=[shp, shp],
        compiler_params=_cparams(2),
        name=name,
    )(pt_flat, *([src] * len(page_specs)), w1, pos, b1, w2)


def _select_blocks(score, cur, n_blk):
    jj = lax.broadcasted_iota(jnp.int32, score.shape, 1)
    forced = (jj == 0) | (jj == cur) | (jj == cur - 1)
    sc = jnp.where(forced, BIG, jnp.where(jj <= cur, score, NEG))
    rank = jnp.zeros(score.shape, F32)
    for i in range(n_blk):
        si = sc[:, i:i + 1]
        beats = (si > sc) | ((si == sc) & (jj > i))
        rank = rank + jnp.where(beats, 1.0, 0.0)
    sel = (rank < float(min(N_SEL, n_blk))) & (jj <= cur) & (jj < n_blk)
    return jnp.where(sel, 1.0, 0.0).astype(BF16)


def _select_blocks_t(score, cur):
    n_blk = score.shape[0]
    jj = lax.broadcasted_iota(jnp.int32, score.shape, 0)
    forced = (jj == 0) | (jj == cur) | (jj == cur - 1)
    sc = jnp.where(forced, BIG, jnp.where(jj <= cur, score, NEG))
    rank = jnp.zeros(score.shape, F32)
    for i in range(n_blk):
        si = sc[i:i + 1, :]
        beats = (si > sc) | ((si == sc) & (jj > i))
        rank = rank + jnp.where(beats, 1.0, 0.0)
    sel = (rank < float(min(N_SEL, n_blk))) & (jj <= cur)
    return jnp.where(sel, 1.0, 0.0)


def _softmax_rows(logits):
    m = jnp.maximum(jnp.max(logits, axis=-1, keepdims=True), M_FLOOR)
    e = jnp.exp(logits - m)
    s = jnp.sum(e, axis=-1, keepdims=True)
    return e / jnp.where(s > 0.0, s, 1.0)


MASKED_TILE = WINDOW // 128 + 1
ATT_HP = 2


def _attn_prompt_body(zq_ref, zg_ref, ck_ref, cv_ref, ks_ref, vs_ref, kw_ref, vw_ref,
                      bc_ref, tz_ref, ovt_ref, e_ref, o_ref,
                      ksb, vsb, kwb, vwb, selm_ref, s_ref, mel_ref, lel_ref, acc_ref):
    qt = pl.program_id(2)
    rows = GROUP * 128

    @pl.when(qt == 0)
    def _():
        ksb[...] = ks_ref[...].astype(BF16)
        vsb[...] = vs_ref[...].astype(BF16)
        kwb[...] = kw_ref[...].astype(BF16)
        vwb[...] = vw_ref[...].astype(BF16)

    heads = range(ATT_HP)
    lanes = lambda h: slice(h * HEAD_DIM, (h + 1) * HEAD_DIM)
    groups = lambda h: slice(h * GROUP, (h + 1) * GROUP)
    n_blk = SEQ // SEL_BLK
    cur = jnp.right_shift(qt * 128 + lax.broadcasted_iota(jnp.int32, (n_blk, 128), 1), 6)
    qq, o_c = [], []
    for h in heads:
        q = zq_ref[:, h * GROUP * HEAD_DIM:(h + 1) * GROUP * HEAD_DIM] * Q_SCALE
        qh = jnp.concatenate([q[:, g * HEAD_DIM:(g + 1) * HEAD_DIM] for g in range(GROUP)], axis=0).astype(BF16)
        qq.append(qh)
        pc = _softmax_rows(_dot_nt(qh, ck_ref[h]) + bc_ref[0, groups(h)].reshape(rows, 128))
        o_c.append(_dot(pc.astype(BF16), cv_ref[h]))
        ps = pc[0:128] + pc[128:256] + pc[256:384] + pc[384:512]
        sel_t = _select_blocks_t(_dot_nt_split3(ovt_ref[...], ps), cur)
        sel = jnp.concatenate([sel_t, jnp.zeros((128 - n_blk, 128), F32)], axis=0).T.astype(BF16)
        for j in range(SEQ // 256):
            selm_ref[h, j] = (_dot(sel, e_ref[j]) - 1.0) * BIG

    def attend(k_ref, v_ref, lo, tile_of_delta, use_sel):
        hi = jnp.right_shift(qt, 1) + 1
        mel_ref[...] = jnp.full(mel_ref.shape, M_FLOOR, F32)

        def logits_pass(j, carry):
            off = pl.multiple_of(j * 256, 256)
            d0 = qt - 2 * j
            i0, i1 = tile_of_delta(d0), tile_of_delta(d0 - 1)
            for h in heads:
                s = _dot_nt(qq[h], k_ref[pl.ds(off, 256), lanes(h)]).reshape(GROUP, 128, 256)
                s = s + jnp.concatenate([tz_ref[i0, groups(h)], tz_ref[i1, groups(h)]], axis=-1)
                if use_sel:
                    s = s + selm_ref[h, j][None]
                s = s.reshape(rows, 256)
                s_ref[h, j] = s
                mel_ref[h] = jnp.maximum(mel_ref[h], jnp.maximum(s[:, :128], s[:, 128:]))
            return carry

        lax.fori_loop(lo, hi, logits_pass, 0)
        m = [jnp.max(mel_ref[h], axis=-1, keepdims=True) for h in heads]
        lel_ref[...] = jnp.zeros_like(lel_ref)
        acc_ref[...] = jnp.zeros_like(acc_ref)

        def value_pass(j, carry):
            off = pl.multiple_of(j * 256, 256)
            for h in heads:
                pe = jnp.exp(s_ref[h, j] - m[h])
                lel_ref[h] += pe[:, :128] + pe[:, 128:]
                acc_ref[h] += _dot(pe.astype(BF16), v_ref[pl.ds(off, 256), lanes(h)])
            return carry

        lax.fori_loop(lo, hi, value_pass, 0)
        outs = []
        for h in heads:
            l = jnp.sum(lel_ref[h], axis=-1, keepdims=True)
            outs.append(acc_ref[h] / jnp.where(l > 0.0, l, 1.0))
        return outs

    n_win = WINDOW // 128
    o_s = attend(ksb, vsb, 0, lambda d: jnp.where(d < 0, MASKED_TILE, jnp.minimum(d, 2)), True)
    o_w = attend(kwb, vwb, jnp.right_shift(jnp.maximum(qt - n_win, 0), 1),
                 lambda d: jnp.where((d < 0) | (d > n_win), MASKED_TILE, d), False)

    gates = jax.nn.sigmoid(zg_ref[...])
    outs = []
    for h in heads:
        for g in range(GROUP):
            r = slice(g * 128, (g + 1) * 128)
            gate = lambda branch: gates[:, h * 128 + branch * GROUP + g:h * 128 + branch * GROUP + g + 1]
            outs.append(gate(0) * o_c[h][r] + gate(1) * o_s[h][r] + gate(2) * o_w[h][r])
    o_ref[...] = jnp.concatenate(outs, axis=1).astype(BF16)


def _attn_prompt(z, ck, cv, bias_cmp, tz, ov, emat):
    nq = SEQ // 128
    nhp = N_KV // ATT_HP
    kw = ATT_HP * HEAD_DIM
    qw = ATT_HP * GROUP * HEAD_DIM
    kv_col = lambda base, kind: (lambda n, k, t: (n, (base + kind * N_KV * HEAD_DIM) // kw + k))
    kvspec = lambda base, kind: pl.BlockSpec((SEQ, kw), kv_col(base, kind))
    rows = GROUP * 128
    return pl.pallas_call(
        _attn_prompt_body,
        grid=(BATCH, nhp, nq),
        in_specs=[
            pl.BlockSpec((128, qw), lambda n, k, t: (n * nq + t, C_Q // qw + k)),
            pl.BlockSpec((128, ATT_HP * 128), lambda n, k, t: (n * nq + t, C_GNSA // (ATT_HP * 128) + k)),
            pl.BlockSpec((ATT_HP, SEG_ROWS, HEAD_DIM), lambda n, k, t: (n * nhp + k, 0, 0)),
            pl.BlockSpec((ATT_HP, SEG_ROWS, HEAD_DIM), lambda n, k, t: (n * nhp + k, 0, 0)),
            kvspec(C_PAG, 2), kvspec(C_PAG, 3), kvspec(C_WIN, 0), kvspec(C_WIN, 1),
            pl.BlockSpec((1, ATT_HP * GROUP, 128, 128), lambda n, k, t: (t, k, 0, 0)),
            pl.BlockSpec((MASKED_TILE + 1, ATT_HP * GROUP, 128, 128), lambda n, k, t: (0, k, 0, 0)),
            pl.BlockSpec((SEQ // SEL_BLK, 128), lambda n, k, t: (0, 0)),
            pl.BlockSpec((SEQ // 256, 128, 256), lambda n, k, t: (0, 0, 0)),
        ],
        out_specs=pl.BlockSpec((128, qw), lambda n, k, t: (n * nq + t, k)),
        out_shape=jax.ShapeDtypeStruct((M_PROMPT, N_HEADS * HEAD_DIM), BF16),
        scratch_shapes=[pltpu.VMEM((SEQ, kw), BF16)] * 4 + [
            pltpu.VMEM((ATT_HP, SEQ // 256, 128, 256), F32),
            pltpu.VMEM((ATT_HP, SEQ // 256, rows, 256), F32),
            pltpu.VMEM((ATT_HP, rows, 128), F32), pltpu.VMEM((ATT_HP, rows, 128), F32),
            pltpu.VMEM((ATT_HP, rows, HEAD_DIM), F32)],
        compiler_params=_cparams(3),
        name="attn_prompt",
    )(z, z, ck, cv, z, z, z, z, bias_cmp, tz, ov, emat)


S_ROWS = GROUP * N_KV * DEC_SEQ


ATT_SS = 2


def _attn_sample_body(pt_ref, q_ref, gs_ref, ck_ref, cv_ref, *rest):
    n_pg = ATT_SS * 2 * N_PAGES
    page_refs = rest[:n_pg]
    win_refs = rest[n_pg:n_pg + 2 * ATT_SS]
    (nkv_ref, nwin_ref, bcmp_ref, bsel_ref, bnew_ref, bwin_ref, ov_ref, e_ref,
     o_ref, s_ref, nk_ref, nw_ref) = rest[n_pg + 2 * ATT_SS:]
    kv_of_row = jnp.bitwise_and(jnp.right_shift(lax.broadcasted_iota(jnp.int32, (S_ROWS, 1), 0), 2), N_KV - 1)
    col = lambda k, half: pl.ds(half * N_KV * HEAD_DIM + k * HEAD_DIM, HEAD_DIM)
    head_rows = _kv_head_rows
    rowmax = lambda s: jnp.max(s, axis=-1, keepdims=True)
    rowsum = lambda s: jnp.sum(s, axis=-1, keepdims=True)

    @pl.when(pl.program_id(0) == 0)
    def _():
        nk_ref[...] = jnp.zeros_like(nk_ref)
        nw_ref[...] = jnp.zeros_like(nw_ref)

    def one_sequence(i):
        pages = page_refs[i * 2 * N_PAGES:(i + 1) * 2 * N_PAGES]
        kwin_ref, vwin_ref = win_refs[2 * i], win_refs[2 * i + 1]
        qq = (q_ref[i] * Q_SCALE).astype(BF16)

        def logits(get_k):
            out = None
            for k in range(N_KV):
                s = jnp.where(kv_of_row == k, _dot_nt(qq, get_k(k).astype(BF16)), 0.0)
                out = s if out is None else out + s
            return out

        def weighted(pe, get_v):
            out = None
            for k in range(N_KV):
                o = _dot(jnp.where(kv_of_row == k, pe, 0.0).astype(BF16), get_v(k).astype(BF16))
                out = o if out is None else out + o
            return out

        nk_ref[i, 0:8, :] = nkv_ref[i]
        nw_ref[i, 0:8, :] = nwin_ref[i]

        pc = _softmax_rows(logits(lambda k: ck_ref[i, pl.ds(k * SEG_ROWS, SEG_ROWS), :]) + bcmp_ref[...])
        o_c = weighted(pc, lambda k: cv_ref[i, pl.ds(k * SEG_ROWS, SEG_ROWS), :])
        ps = pc + pltpu.roll(pc, 16, axis=0) + pltpu.roll(pc, 32, axis=0) + pltpu.roll(pc, 48, axis=0)
        score = _dot_split3(ps, ov_ref[...])
        n_blk = -(-(PAST_LEN + DEC_SEQ) // SEL_BLK)
        cur = jnp.full((S_ROWS, 128), PAST_LEN // SEL_BLK, jnp.int32)
        sel = _select_blocks(score, cur, n_blk)
        key_mask = (_dot(sel, e_ref[...]) - 1.0) * BIG

        m = jnp.full((S_ROWS, 1), M_FLOOR, F32)
        for p in range(N_PAGES):
            s = (logits(lambda k: head_rows(pages[2 * p], k)) + bsel_ref[p]
                 + key_mask[:, p * PAGE_SIZE:(p + 1) * PAGE_SIZE])
            s_ref[i, p] = s
            m = jnp.maximum(m, rowmax(s))
        sn = logits(lambda k: nk_ref[i, :, col(k, 0)]) + bnew_ref[...]
        m = jnp.maximum(m, rowmax(sn))
        pn = jnp.exp(sn - m)
        l = rowsum(pn)
        acc = weighted(pn, lambda k: nk_ref[i, :, col(k, 1)])
        for p in range(N_PAGES):
            pe = jnp.exp(s_ref[i, p] - m)
            l = l + rowsum(pe)
            acc = acc + weighted(pe, lambda k: head_rows(pages[2 * p + 1], k))
        o_s = acc / jnp.where(l > 0.0, l, 1.0)

        sw = logits(lambda k: head_rows(kwin_ref, k)) + bwin_ref[...]
        sn = logits(lambda k: nw_ref[i, :, col(k, 0)]) + bnew_ref[...]
        m = jnp.maximum(jnp.maximum(rowmax(sw), rowmax(sn)), M_FLOOR)
        pw = jnp.exp(sw - m)
        pn = jnp.exp(sn - m)
        l = rowsum(pw) + rowsum(pn)
        o_w = weighted(pw, lambda k: head_rows(vwin_ref, k)) + weighted(pn, lambda k: nw_ref[i, :, col(k, 1)])
        o_w = o_w / jnp.where(l > 0.0, l, 1.0)

        gates = jax.nn.sigmoid(gs_ref[i])
        o_ref[i] = gates[:, 0:1] * o_c + gates[:, 1:2] * o_s + gates[:, 2:3] * o_w

    for i in range(ATT_SS):
        one_sequence(i)


def _attn_sample(pt_flat, q_s, g_s, ck, cv, cache_kv, nkv, state_win, nwin, bcmp, bsel, bnew, bwin, ov, emat):
    const2 = lambda shape: pl.BlockSpec(shape, lambda n, pt: (0, 0))
    per_seq = lambda rows, width: pl.BlockSpec((ATT_SS, rows, width), lambda n, pt: (n, 0, 0))
    page_specs = [pl.BlockSpec((None, None, PAGE_SIZE, None, N_KV, HEAD_DIM),
                               lambda n, pt, i=i, p=p, kind=kind:
                               (0, pt[(n * ATT_SS + i) * N_PAGES + p], 0, kind, 0, 0))
                  for i in range(ATT_SS) for p in range(N_PAGES) for kind in (2, 3)]
    win_specs = [pl.BlockSpec((None, None, WINDOW, None, N_KV, HEAD_DIM),
                              lambda n, pt, i=i, kind=kind: (0, n * ATT_SS + i, 0, kind, 0, 0))
                 for i in range(ATT_SS) for kind in (0, 1)]
    grid_spec = pltpu.PrefetchScalarGridSpec(
        num_scalar_prefetch=1,
        grid=(DEC_BATCH // ATT_SS,),
        in_specs=[
            per_seq(S_ROWS, HEAD_DIM), per_seq(S_ROWS, 128),
            per_seq(N_KV * SEG_ROWS, HEAD_DIM), per_seq(N_KV * SEG_ROWS, HEAD_DIM),
        ] + page_specs + win_specs + [
            per_seq(8, 1024), per_seq(8, 1024),
            const2((S_ROWS, 128)),
            pl.BlockSpec((N_PAGES, S_ROWS, 128), lambda n, pt: (0, 0, 0)),
            const2((S_ROWS, 128)),
            const2((S_ROWS, WINDOW)),
            const2((128, 128)),
            const2((128, PAST_LEN)),
        ],
        out_specs=per_seq(S_ROWS, HEAD_DIM),
        scratch_shapes=[
            pltpu.VMEM((ATT_SS, N_PAGES, S_ROWS, 128), F32),
            pltpu.VMEM((ATT_SS, 128, 1024), F32), pltpu.VMEM((ATT_SS, 128, 1024), F32),
        ],
    )
    return pl.pallas_call(
        _attn_sample_body,
        grid_spec=grid_spec,
        out_shape=jax.ShapeDtypeStruct((DEC_BATCH, S_ROWS, HEAD_DIM), F32),
        compiler_params=_cparams(1),
        name="attn_sample",
    )(pt_flat, q_s, g_s, ck, cv, *([cache_kv] * len(page_specs)), *([state_win] * len(win_specs)), nkv, nwin,
      bcmp, bsel, bnew, bwin, ov, emat)


def _t5_bucket(dist):
    d = jnp.maximum(dist, 0)
    df = jnp.maximum(d, 1).astype(F32)
    large = MAX_EXACT + (jnp.log(df / MAX_EXACT) / math.log(MAX_DIST / MAX_EXACT)
                         * (N_BUCKETS - MAX_EXACT)).astype(jnp.int32)
    large = jnp.minimum(large, N_BUCKETS - 1)
    return jnp.where(d < MAX_EXACT, d, large)


def _bias_lookup_body(rb_ref, idx_ref, o_ref):
    idx = idx_ref[0]
    for h in range(N_HEADS):
        acc = jnp.full(idx.shape, NEG, F32)
        for b in range(N_BUCKETS):
            acc = jnp.where(idx == b, rb_ref[b * N_HEADS + h], acc)
        o_ref[0, h] = acc


def _bias_table(rel_bias, dist, valid, name):
    p, r, _ = dist.shape
    idx = jnp.where(jnp.asarray(valid), _t5_bucket(jnp.asarray(dist, jnp.int32)), -1)
    return pl.pallas_call(
        _bias_lookup_body,
        grid=(p,),
        in_specs=[pl.BlockSpec(memory_space=pltpu.SMEM), pl.BlockSpec((1, r, 128), lambda i: (i, 0, 0))],
        out_specs=pl.BlockSpec((1, N_HEADS, r, 128), lambda i: (i, 0, 0, 0)),
        out_shape=jax.ShapeDtypeStruct((p, N_HEADS, r, 128), F32),
        compiler_params=_cparams(1),
        name=name,
    )(rel_bias.astype(F32).reshape(-1), idx)


def _overlap(nc, nb):
    cs = np.arange(nc)[:, None] * CMP_STRIDE
    js = np.arange(nb)[None, :] * SEL_BLK
    ov = np.clip(np.minimum(cs + CMP_LEN, js + SEL_BLK) - np.maximum(cs, js), 0, None) / CMP_LEN
    out = np.zeros((128, 128), np.float32)
    out[:nc, :nb] = ov
    return jnp.asarray(out, BF16)


def _position_tables(rel_bias):
    nc = SEG_ROWS - 1
    t = np.arange(128)[None, :, None]
    c = np.arange(128)[None, None, :]
    cend = c * CMP_STRIDE + CMP_LEN - 1
    d = np.arange(MASKED_TILE + 1)[:, None, None] * 128 + t - c
    tz = _bias_table(rel_bias, d, (d >= 0) & (d < WINDOW), "bias_tiles")
    d = np.arange(SEQ // 128)[:, None, None] * 128 + t - cend
    bias_cmp = _bias_table(rel_bias, d, (d >= 0) & (c < nc), "bias_cmp")
    ts = np.arange(8)[None, :, None]
    qpos = PAST_LEN + ts
    live = ts < DEC_SEQ
    d_cmp = qpos - cend
    d_sel = qpos - (np.arange(N_PAGES)[:, None, None] * PAGE_SIZE + c)
    d_new = ts - c
    d_win = qpos - (PAST_LEN - WINDOW + np.arange(WINDOW // 128)[:, None, None] * 128 + c)
    d = np.concatenate([d_cmp, d_sel, d_new, d_win], axis=0)
    valid = np.concatenate([(d_cmp >= 0) & (c < nc), d_sel >= 0, (d_new >= 0) & (c < DEC_SEQ),
                            (d_win >= 0) & (d_win < WINDOW)], axis=0) & live
    o = _bias_table(rel_bias, d, valid, "bias_sample")[:, :, :DEC_SEQ]
    o = jnp.transpose(o.reshape(-1, N_KV, GROUP, DEC_SEQ, 128), (0, 2, 1, 3, 4)).reshape(-1, S_ROWS, 128)
    bcmp, bsel, bnew = o[0], o[1:1 + N_PAGES], o[1 + N_PAGES]
    bwin = jnp.transpose(o[2 + N_PAGES:], (1, 0, 2)).reshape(S_ROWS, WINDOW)
    keys = np.arange(SEQ)
    e_all = (np.arange(128)[:, None] == (keys // SEL_BLK)[None, :]).astype(np.float32)
    e_tiles = jnp.asarray(e_all.reshape(128, SEQ // 256, 256).transpose(1, 0, 2), BF16)
    ovt_p = jnp.transpose(_overlap(nc, SEQ // SEL_BLK))[:SEQ // SEL_BLK]
    return dict(tz=tz, bias_cmp=bias_cmp, bcmp=bcmp, bsel=bsel, bnew=bnew, bwin=bwin,
                ovt_p=ovt_p, ov_s=_overlap(nc, -(-(PAST_LEN + DEC_SEQ) // SEL_BLK)),
                e_tiles=e_tiles, e_all=jnp.asarray(e_all, BF16))


W_TC = 512
W_TR = 2048
N_GNSA = 3 * N_HEADS


def _permute_w_in_body(a_ref, b_ref, g_ref, s_ref, o_ref):
    j = pl.program_id(1)
    first, last = C_GRNN // W_TC, C_GNSA // W_TC

    @pl.when(j < first)
    def _():
        o_ref[...] = a_ref[...].astype(BF16)

    @pl.when((j >= first) & (j < last))
    def _():
        lane = lax.broadcasted_iota(jnp.int32, b_ref.shape, 1)
        b = jnp.where(lane < N_GNSA, b_ref[...], 0.0).astype(BF16)
        o_ref[...] = (_dot(a_ref[...].astype(BF16), s_ref[0]) + _dot(b, s_ref[1])).astype(BF16)

    @pl.when(j == last)
    def _():
        o_ref[...] = g_ref[...]


def _permute_w_in(w_in):
    c0 = C_GRNN
    g_nsa = w_in[:, c0:c0 + N_GNSA].reshape(D_MODEL, N_KV, GROUP, 3)
    g_nsa = jnp.transpose(g_nsa, (0, 1, 3, 2)).reshape(D_MODEL, N_KV, 3 * GROUP)
    g_nsa = jnp.pad(g_nsa, ((0, 0), (0, 0), (0, 128 - 3 * GROUP))).reshape(D_MODEL, N_KV * 128).astype(BF16)
    r = np.arange(W_TC)[:, None]
    c = np.arange(W_TC)[None, :]
    shift = jnp.asarray(np.stack([r == c + N_GNSA, r == c + N_GNSA - W_TC]).astype(np.float32), BF16)
    first, last = C_GRNN // W_TC, C_GNSA // W_TC
    return pl.pallas_call(
        _permute_w_in_body,
        grid=(D_MODEL // W_TR, D_Z // W_TC),
        in_specs=[
            pl.BlockSpec((W_TR, W_TC), lambda i, j: (i, jnp.minimum(j, last - 1))),
            pl.BlockSpec((W_TR, W_TC), lambda i, j: (i, jnp.clip(j + 1, first + 1, last))),
            pl.BlockSpec((W_TR, W_TC), lambda i, j: (i, 0)),
            pl.BlockSpec((2, W_TC, W_TC), lambda i, j: (0, 0, 0)),
        ],
        out_specs=pl.BlockSpec((W_TR, W_TC), lambda i, j: (i, j)),
        out_shape=jax.ShapeDtypeStruct((D_MODEL, D_Z), BF16),
        compiler_params=_cparams(2),
        name="w_in_layout",
    )(w_in, w_in, g_nsa, shift)


def _cmp_weights(w1_k, w1_v, b1_k, b1_v, w2_k, w2_v, pos):
    def cat(w1):
        w = w1.reshape(2, CMP_STRIDE * HEAD_DIM, CMP_HID)
        return jnp.concatenate([w[0], w[1]], axis=1).reshape(N_PAIR, 256, 2 * CMP_HID)
    w1 = jnp.stack([cat(w1_k), cat(w1_v)]).astype(BF16)
    posm = jnp.pad(pos.reshape(2, CMP_STRIDE * HEAD_DIM), ((0, 6), (0, 0)))
    b1 = jnp.stack([b1_k, b1_v]).reshape(2, 1, CMP_HID)
    w2 = jnp.stack([w2_k, w2_v]).astype(BF16)
    return w1, posm, b1, w2


def kernel(x_prompt, x_sample, cache_kv, page_table, state_win, state_conv, state_h, rel_bias, ln_final, ln_ffn1, w_ffn1_gate, w_ffn1_up, w_ffn1_down, ln_mix, w_in, conv_w, conv_b, rg_wa, rg_ba, rg_wi, rg_bi, rg_lambda, cmp_pos, cmp_k_w1, cmp_k_b1, cmp_k_w2, cmp_v_w1, cmp_v_b1, cmp_v_w2, w_br_rnn, w_br_attn, w_out, ln_ffn2, w_ffn2_gate, w_ffn2_up, w_ffn2_down):
    tabs = _position_tables(rel_bias)
    x = jnp.concatenate([x_prompt.reshape(M_PROMPT, D_MODEL), x_sample.reshape(M_SAMPLE, D_MODEL)], axis=0)

    x = _ffn(x, ln_ffn1[0], w_ffn1_gate[0].astype(BF16), w_ffn1_up[0].astype(BF16), w_ffn1_down[0].astype(BF16),
             ln_final, False)
    z = _in_proj(x, ln_mix[0], _permute_w_in(w_in[0]))
    z_s = z[M_PROMPT:]

    vec = lambda v: v.reshape(1, D_RNN)
    rnn_w = (conv_w[0], vec(conv_b[0]), rg_wa[0].astype(BF16), vec(rg_ba[0]), rg_wi[0].astype(BF16),
             vec(rg_bi[0]), vec(rg_lambda[0]))
    g_p, h_p = _rnn_prompt(z, *rnn_w)
    tmajor = lambda a: jnp.transpose(a.reshape(DEC_BATCH, -1, D_RNN), (1, 0, 2))
    g_s, h_s = _rnn_sample(tmajor(z_s[:, C_UGATE:C_UGATE + D_RNN]), tmajor(z_s[:, C_UX:C_UX + D_RNN]),
                           tmajor(state_conv[0]), state_h[0], *rnn_w)
    grnn = jnp.concatenate([g_p, jnp.transpose(g_s, (1, 0, 2)).reshape(M_SAMPLE, D_RNN)], axis=0)

    cw = _cmp_weights(cmp_k_w1[0], cmp_v_w1[0], cmp_k_b1[0], cmp_v_b1[0], cmp_k_w2[0], cmp_v_w2[0], cmp_pos[0])
    pt_prompt = jnp.arange(BATCH * N_PAGES, dtype=jnp.int32)
    pt_sample = page_table.reshape(-1).astype(jnp.int32)
    ck_p, cv_p = _compress(z.reshape(M_TOK // PAGE_SIZE, PAGE_SIZE, D_Z), pt_prompt, C_PAG // 1024, BATCH, *cw,
                           name="compress_prompt")
    ck_s, cv_s = _compress(cache_kv, pt_sample, None, DEC_BATCH, *cw, name="compress_sample")
    o_p = _attn_prompt(z, ck_p.reshape(BATCH * N_KV, SEG_ROWS, HEAD_DIM), cv_p.reshape(BATCH * N_KV, SEG_ROWS, HEAD_DIM),
                       tabs["bias_cmp"], tabs["tz"], tabs["ovt_p"], tabs["e_tiles"])

    def rows_gkt(a, width):
        a = a.reshape(DEC_BATCH, DEC_SEQ, N_KV, GROUP, width)
        return jnp.transpose(a, (0, 3, 2, 1, 4)).reshape(DEC_BATCH, S_ROWS, width)

    q_s = rows_gkt(z_s[:, C_Q:C_Q + N_HEADS * HEAD_DIM], HEAD_DIM)
    gn = z_s[:, C_GNSA:].reshape(M_SAMPLE, N_KV, 128)[:, :, :3 * GROUP].reshape(M_SAMPLE, N_KV, 3, GROUP)
    g_s3 = jnp.pad(rows_gkt(jnp.transpose(gn, (0, 1, 3, 2)), 3), ((0, 0), (0, 0), (0, 125)))
    pad8 = lambda a: jnp.pad(a.reshape(DEC_BATCH, DEC_SEQ, -1), ((0, 0), (0, 8 - DEC_SEQ), (0, 0)))
    nkv = pad8(z_s[:, C_PAG + 2 * N_KV * HEAD_DIM:C_PAG + 4 * N_KV * HEAD_DIM])
    nwin = pad8(z_s[:, C_WIN:C_WIN + 2 * N_KV * HEAD_DIM])
    o_s = _attn_sample(pt_sample, q_s, g_s3, ck_s, cv_s, cache_kv, nkv, state_win, nwin,
                       tabs["bcmp"], tabs["bsel"], tabs["bnew"], tabs["bwin"], tabs["ov_s"], tabs["e_all"])
    o_s = jnp.transpose(o_s.reshape(DEC_BATCH, GROUP, N_KV, DEC_SEQ, HEAD_DIM), (0, 3, 2, 1, 4))
    oattn = jnp.concatenate([o_p, o_s.reshape(M_SAMPLE, N_HEADS * HEAD_DIM).astype(BF16)], axis=0)

    merged = _merge(z, grnn, oattn, w_br_rnn[0].astype(BF16), w_br_attn[0].astype(BF16))
    x = _out_proj(x, merged, w_out[0].astype(BF16))
    ffn2 = functools.partial(_ffn, x, ln_ffn2[0], w_ffn2_gate[0].astype(BF16), w_ffn2_up[0].astype(BF16),
                             w_ffn2_down[0].astype(BF16), ln_final, True)
    y_p = ffn2(tile0=0, n_tiles=M_PROMPT // TM)
    y_s = ffn2(tile0=M_PROMPT // TM, n_tiles=M_SAMPLE // TM)

    kv = z[:, C_PAG:C_PAG + 4 * N_KV * HEAD_DIM]
    wn = z[:, C_WIN:C_WIN + 2 * N_KV * HEAD_DIM]
    keep = CONV_W - 1
    conv_p = jnp.stack([lax.slice(z, ((n + 1) * SEQ - keep, C_UX), ((n + 1) * SEQ, C_UX + D_RNN))
                        for n in range(BATCH)])
    conv_s = z_s[:, C_UX:C_UX + D_RNN].reshape(DEC_BATCH, DEC_SEQ, D_RNN)[:, DEC_SEQ - keep:]
    win_p = wn[:M_PROMPT].reshape(BATCH, SEQ, 2, N_KV, HEAD_DIM)[:, SEQ - WINDOW:]
    win_s = jnp.concatenate([state_win.reshape(DEC_BATCH, WINDOW, 2, N_KV, HEAD_DIM),
                             wn[M_PROMPT:].reshape(DEC_BATCH, DEC_SEQ, 2, N_KV, HEAD_DIM)], axis=1)
    return (
        y_p.reshape(BATCH, SEQ, D_MODEL),
        y_s.reshape(DEC_BATCH, DEC_SEQ, D_MODEL),
        kv[:M_PROMPT].reshape(1, BATCH, SEQ, 4, N_KV, HEAD_DIM),
        kv[M_PROMPT:].reshape(1, DEC_BATCH, DEC_SEQ, 4, N_KV, HEAD_DIM),
        win_p[None],
        win_s[None, :, DEC_SEQ:],
        conv_p[None],
        conv_s[None],
        h_p[None, :, 7],
        h_s[None],
    )
```

```python
import functools
import math

import numpy as np
import jax
import jax.numpy as jnp
from jax import lax
from jax.experimental import pallas as pl
from jax.experimental.pallas import tpu as pltpu

F32 = jnp.float32
BF16 = jnp.bfloat16

D_MODEL = 4096
BATCH = 4
SEQ = 2048
DEC_BATCH = 128
DEC_SEQ = 4
PAST_LEN = 2048
PAGE_SIZE = 128
N_PAGES = PAST_LEN // PAGE_SIZE
D_RNN = D_MODEL // 2
RNN_BLOCKS = 16
RNN_BW = D_RNN // RNN_BLOCKS
CONV_W = 4
LRU_C = 8.0
N_HEADS = 16
HEAD_DIM = 128
N_KV = 4
GROUP = N_HEADS // N_KV
CMP_LEN = 32
CMP_STRIDE = 16
CMP_HID = 2 * HEAD_DIM
SEL_BLK = 64
N_SEL = 8
WINDOW = 512
N_BUCKETS = 32
MAX_EXACT = 16
MAX_DIST = 128
D_FF = ((8 * D_MODEL // 3 + 255) // 256) * 256
EPS = 1e-6
NEG = -1e30
BIG = 1e30
M_FLOOR = -1e29
Q_SCALE = HEAD_DIM ** -0.5

M_PROMPT = BATCH * SEQ
M_SAMPLE = DEC_BATCH * DEC_SEQ
M_TOK = M_PROMPT + M_SAMPLE

C_UGATE = 0
C_UX = C_UGATE + D_RNN
C_Q = C_UX + D_RNN
C_PAG = C_Q + N_HEADS * HEAD_DIM
C_WIN = C_PAG + 4 * N_KV * HEAD_DIM
C_GRNN = C_WIN + 2 * N_KV * HEAD_DIM
C_GATTN = C_GRNN + D_MODEL
C_GNSA = C_GATTN + D_MODEL
D_Z = C_GNSA + N_KV * 128

TM = 512
TF = 256
TN_IN = 1280
TN_MM = 1024
VMEM_LIMIT = 56 * 2 ** 20


def _cparams(n_axes, vmem=VMEM_LIMIT):
    return pltpu.CompilerParams(dimension_semantics=("arbitrary",) * n_axes, vmem_limit_bytes=vmem)


def _dot(a, b):
    return jnp.dot(a, b, preferred_element_type=F32)


def _dot_nt(a, b):
    return lax.dot_general(a, b, (((1,), (1,)), ((), ())), preferred_element_type=F32)


def _dot_split3(a, b):
    a1 = a.astype(BF16)
    r1 = a - a1.astype(F32)
    a2 = r1.astype(BF16)
    a3 = (r1 - a2.astype(F32)).astype(BF16)
    return _dot(a1, b) + _dot(a2, b) + _dot(a3, b)


def _dot_nt_split3(a, b):
    b1 = b.astype(BF16)
    r1 = b - b1.astype(F32)
    b2 = r1.astype(BF16)
    b3 = (r1 - b2.astype(F32)).astype(BF16)
    return _dot_nt(a, b1) + _dot_nt(a, b2) + _dot_nt(a, b3)


def _kv_head_rows(ref, k):
    n_rows = ref.shape[0]
    return ref.reshape(n_rows * N_KV, HEAD_DIM)[pl.ds(k, n_rows, stride=N_KV), :]


def _rms(x, g):
    return x * lax.rsqrt(jnp.mean(x * x, axis=-1, keepdims=True) + EPS) * g


def _ffn_body(x_ref, ln_ref, wg_ref, wu_ref, wd_ref, lnf_ref, o_ref, xn_ref, *, n_f, final_norm):
    f = pl.program_id(1)

    @pl.when(f == 0)
    def _():
        x = x_ref[...]
        xn_ref[...] = _rms(x, ln_ref[...]).astype(BF16)
        o_ref[...] = 2.0 * x

    xn = xn_ref[...]
    g = _dot(xn, wg_ref[...])
    u = _dot(xn, wu_ref[...])
    h = (g * jax.nn.sigmoid(g) * u).astype(BF16)
    o_ref[...] += _dot(h, wd_ref[...])

    @pl.when(f == n_f - 1)
    def _():
        y = 0.5 * o_ref[...]
        if final_norm:
            y = _rms(y, lnf_ref[...])
        o_ref[...] = y


def _ffn(x, ln, wg, wu, wd, lnf, final_norm, tile0=0, n_tiles=None):
    m = (x.shape[0] // TM if n_tiles is None else n_tiles) * TM
    n_f = D_FF // TF
    return pl.pallas_call(
        functools.partial(_ffn_body, n_f=n_f, final_norm=final_norm),
        grid=(m // TM, n_f),
        in_specs=[
            pl.BlockSpec((TM, D_MODEL), lambda i, f: (tile0 + i, 0), pipeline_mode=pl.Buffered(1)),
            pl.BlockSpec((1, D_MODEL), lambda i, f: (0, 0)),
            pl.BlockSpec((D_MODEL, TF), lambda i, f: (0, f)),
            pl.BlockSpec((D_MODEL, TF), lambda i, f: (0, f)),
            pl.BlockSpec((TF, D_MODEL), lambda i, f: (f, 0)),
            pl.BlockSpec((1, D_MODEL), lambda i, f: (0, 0)),
        ],
        out_specs=pl.BlockSpec((TM, D_MODEL), lambda i, f: (i, 0)),
        out_shape=jax.ShapeDtypeStruct((m, D_MODEL), F32),
        scratch_shapes=[pltpu.VMEM((TM, D_MODEL), BF16)],
        compiler_params=_cparams(2),
        name="ffn",
    )(x, ln.reshape(1, D_MODEL), wg, wu, wd, lnf.reshape(1, D_MODEL))


def _in_proj_body(x_ref, ln_ref, w_ref, o_ref, xn_ref):
    @pl.when(pl.program_id(1) == 0)
    def _():
        xn_ref[...] = _rms(x_ref[...], ln_ref[...]).astype(BF16)

    o_ref[...] = _dot(xn_ref[...], w_ref[...])


def _in_proj(x, ln, w):
    m = x.shape[0]
    return pl.pallas_call(
        _in_proj_body,
        grid=(m // TM, D_Z // TN_IN),
        in_specs=[
            pl.BlockSpec((TM, D_MODEL), lambda i, j: (i, 0), pipeline_mode=pl.Buffered(1)),
            pl.BlockSpec((1, D_MODEL), lambda i, j: (0, 0)),
            pl.BlockSpec((D_MODEL, TN_IN), lambda i, j: (0, j)),
        ],
        out_specs=pl.BlockSpec((TM, TN_IN), lambda i, j: (i, j)),
        out_shape=jax.ShapeDtypeStruct((m, D_Z), F32),
        scratch_shapes=[pltpu.VMEM((TM, D_MODEL), BF16)],
        compiler_params=_cparams(2),
        name="in_proj",
    )(x, ln.reshape(1, D_MODEL), w)


def _merge_body(gr_ref, oa_ref, wr_ref, wa_ref, zr_ref, za_ref, o_ref):
    y_rnn = _dot(gr_ref[...], wr_ref[...])
    y_attn = _dot(oa_ref[...], wa_ref[...])
    o_ref[...] = (jax.nn.sigmoid(zr_ref[...]) * y_rnn + jax.nn.sigmoid(za_ref[...]) * y_attn).astype(BF16)


def _merge(z, grnn, oattn, w_rnn, w_attn):
    m = z.shape[0]
    cr, ca = C_GRNN // TN_MM, C_GATTN // TN_MM
    return pl.pallas_call(
        _merge_body,
        grid=(m // TM, D_MODEL // TN_MM),
        in_specs=[
            pl.BlockSpec((TM, D_RNN), lambda i, j: (i, 0)),
            pl.BlockSpec((TM, N_HEADS * HEAD_DIM), lambda i, j: (i, 0)),
            pl.BlockSpec((D_RNN, TN_MM), lambda i, j: (0, j)),
            pl.BlockSpec((N_HEADS * HEAD_DIM, TN_MM), lambda i, j: (0, j)),
            pl.BlockSpec((TM, TN_MM), lambda i, j: (i, cr + j)),
            pl.BlockSpec((TM, TN_MM), lambda i, j: (i, ca + j)),
        ],
        out_specs=pl.BlockSpec((TM, TN_MM), lambda i, j: (i, j)),
        out_shape=jax.ShapeDtypeStruct((m, D_MODEL), BF16),
        compiler_params=_cparams(2),
        name="merge",
    )(grnn, oattn, w_rnn, w_attn, z, z)


def _out_proj_body(a_ref, w_ref, x_ref, o_ref):
    o_ref[...] = x_ref[...] + _dot(a_ref[...], w_ref[...])


def _out_proj(x, a, w):
    m = x.shape[0]
    return pl.pallas_call(
        _out_proj_body,
        grid=(m // TM, D_MODEL // TN_MM),
        in_specs=[
            pl.BlockSpec((TM, D_MODEL), lambda i, j: (i, 0)),
            pl.BlockSpec((D_MODEL, TN_MM), lambda i, j: (0, j)),
            pl.BlockSpec((TM, TN_MM), lambda i, j: (i, j)),
        ],
        out_specs=pl.BlockSpec((TM, TN_MM), lambda i, j: (i, j)),
        out_shape=jax.ShapeDtypeStruct((m, D_MODEL), F32),
        compiler_params=_cparams(2),
        name="out_proj",
    )(a, w, x)


def _softplus(v):
    return jnp.maximum(v, 0.0) + jnp.log1p(jnp.exp(-jnp.abs(v)))


def _lru_coeffs(xc, wa_ref, ba, wi_ref, bi, sp, n_blk):
    xb = xc.astype(BF16)
    ra = jnp.concatenate([_dot(xb[:, b * RNN_BW:(b + 1) * RNN_BW], wa_ref[b]) for b in range(n_blk)], axis=1)
    ia = jnp.concatenate([_dot(xb[:, b * RNN_BW:(b + 1) * RNN_BW], wi_ref[b]) for b in range(n_blk)], axis=1)
    r = jax.nn.sigmoid(ra + ba)
    i = jax.nn.sigmoid(ia + bi)
    log_a = -LRU_C * r * sp
    a = jnp.exp(log_a)
    bt = jnp.sqrt(-jnp.tanh(log_a) * (a * a + 1.0)) * (i * xc)
    return a, bt


RNN_TC = 256


def _rnn_prompt_body(ug_ref, ux_ref, cw_ref, cb_ref, wa_ref, ba_ref, wi_ref, bi_ref, lam_ref,
                     g_ref, h_ref, tail_ref, hc_ref):
    c = pl.program_id(1)
    tc = RNN_TC

    @pl.when(c == 0)
    def _():
        tail_ref[...] = jnp.zeros_like(tail_ref)
        hc_ref[...] = jnp.zeros_like(hc_ref)

    u = ux_ref[...]
    tail = tail_ref[...]
    row8 = lax.broadcasted_iota(jnp.int32, (8, D_RNN), 0)
    xc = cb_ref[...] + cw_ref[CONV_W - 1:CONV_W, :] * u
    for j in range(1, CONV_W):
        r = pltpu.roll(u, j, axis=0)
        first = jnp.where(row8 >= j, r[0:8], pltpu.roll(tail, j, axis=0))
        shifted = jnp.concatenate([first, r[8:]], axis=0)
        xc = xc + cw_ref[CONV_W - 1 - j:CONV_W - j, :] * shifted
    tail_ref[...] = u[tc - 8:tc]

    a, bt = _lru_coeffs(xc, wa_ref, ba_ref[...], wi_ref, bi_ref[...], _softplus(-lam_ref[...]), RNN_BLOCKS)

    row = lax.broadcasted_iota(jnp.int32, (tc, D_RNN), 0)
    s = 1
    while s < tc:
        keep = row >= s
        a_sh = jnp.where(keep, pltpu.roll(a, s, axis=0), 1.0)
        b_sh = jnp.where(keep, pltpu.roll(bt, s, axis=0), 0.0)
        bt = a * b_sh + bt
        a = a * a_sh
        s *= 2
    h = bt + a * hc_ref[7:8, :]
    hc_ref[...] = h[tc - 8:tc]
    g_ref[...] = (h * jax.nn.gelu(ug_ref[...])).astype(BF16)

    @pl.when(c == pl.num_programs(1) - 1)
    def _():
        h_ref[0] = h[tc - 8:tc]


def _rnn_prompt(z, cw, cb, wa, ba, wi, bi, lam):
    nc = SEQ // RNN_TC
    vec = lambda: pl.BlockSpec((1, D_RNN), lambda n, c: (0, 0))
    blk = lambda: pl.BlockSpec((RNN_BLOCKS, RNN_BW, RNN_BW), lambda n, c: (0, 0, 0))
    return pl.pallas_call(
        _rnn_prompt_body,
        grid=(BATCH, nc),
        in_specs=[
            pl.BlockSpec((RNN_TC, D_RNN), lambda n, c: (n * nc + c, C_UGATE // D_RNN)),
            pl.BlockSpec((RNN_TC, D_RNN), lambda n, c: (n * nc + c, C_UX // D_RNN)),
            pl.BlockSpec((CONV_W, D_RNN), lambda n, c: (0, 0)),
            vec(), blk(), vec(), blk(), vec(), vec(),
        ],
        out_specs=[
            pl.BlockSpec((RNN_TC, D_RNN), lambda n, c: (n * nc + c, 0)),
            pl.BlockSpec((1, 8, D_RNN), lambda n, c: (n, 0, 0)),
        ],
        out_shape=[jax.ShapeDtypeStruct((M_PROMPT, D_RNN), BF16),
                   jax.ShapeDtypeStruct((BATCH, 8, D_RNN), F32)],
        scratch_shapes=[pltpu.VMEM((8, D_RNN), F32), pltpu.VMEM((8, D_RNN), F32)],
        compiler_params=_cparams(2),
        name="rnn_prompt",
    )(z, z, cw, cb, wa, ba, wi, bi, lam)


RNN_SC = 512


def _rnn_sample_body(ug_ref, ux_ref, buf_ref, h0_ref, cw_ref, cb_ref, wa_ref, ba_ref, wi_ref, bi_ref, lam_ref,
                     g_ref, h_ref):
    full = [buf_ref[j] for j in range(CONV_W - 1)] + [ux_ref[t] for t in range(DEC_SEQ)]
    sp = _softplus(-lam_ref[...])
    h = h0_ref[...]
    for t in range(DEC_SEQ):
        xc = cb_ref[...]
        for k in range(CONV_W):
            xc = xc + full[t + k] * cw_ref[k:k + 1, :]
        a, bt = _lru_coeffs(xc, wa_ref, ba_ref[...], wi_ref, bi_ref[...], sp, RNN_SC // RNN_BW)
        h = a * h + bt
        g_ref[t] = (h * jax.nn.gelu(ug_ref[t])).astype(BF16)
    h_ref[...] = h


def _rnn_sample(ug_t, ux_t, buf_t, h0, cw, cb, wa, ba, wi, bi, lam):
    nb = RNN_SC // RNN_BW
    vec = lambda: pl.BlockSpec((1, RNN_SC), lambda c: (0, c))
    blk = lambda: pl.BlockSpec((nb, RNN_BW, RNN_BW), lambda c: (c, 0, 0))
    return pl.pallas_call(
        _rnn_sample_body,
        grid=(D_RNN // RNN_SC,),
        in_specs=[
            pl.BlockSpec((DEC_SEQ, DEC_BATCH, RNN_SC), lambda c: (0, 0, c)),
            pl.BlockSpec((DEC_SEQ, DEC_BATCH, RNN_SC), lambda c: (0, 0, c)),
            pl.BlockSpec((CONV_W - 1, DEC_BATCH, RNN_SC), lambda c: (0, 0, c)),
            pl.BlockSpec((DEC_BATCH, RNN_SC), lambda c: (0, c)),
            pl.BlockSpec((CONV_W, RNN_SC), lambda c: (0, c)),
            vec(), blk(), vec(), blk(), vec(), vec(),
        ],
        out_specs=[
            pl.BlockSpec((DEC_SEQ, DEC_BATCH, RNN_SC), lambda c: (0, 0, c)),
            pl.BlockSpec((DEC_BATCH, RNN_SC), lambda c: (0, c)),
        ],
        out_shape=[jax.ShapeDtypeStruct((DEC_SEQ, DEC_BATCH, D_RNN), BF16),
                   jax.ShapeDtypeStruct((DEC_BATCH, D_RNN), F32)],
        compiler_params=_cparams(1),
        name="rnn_sample",
    )(ug_t, ux_t, buf_t, h0, cw, cb, wa, ba, wi, bi, lam)


N_SEG = PAGE_SIZE // CMP_STRIDE
SEG_ROWS = N_PAGES * N_SEG
N_PAIR = CMP_STRIDE // 2


CMP_PG = 4
SLAB_PITCH = 24


def _compress_body(pt_ref, *refs, paged):
    n_in = CMP_PG * (2 if paged else 1)
    page_refs = refs[:n_in]
    w1_ref, pos_ref, b1_ref, w2_ref, ck_ref, cv_ref, stage_ref, pterm_ref, slab_ref = refs[n_in:]
    n = pl.program_id(0)
    q = pl.program_id(1)

    @pl.when((n == 0) & (q == 0))
    def _():
        for kind in range(2):
            acc = jnp.zeros((8, 2 * CMP_HID), F32)
            for pr in range(N_PAIR):
                acc = acc + _dot(pos_ref[:, pr * 256:(pr + 1) * 256].astype(BF16), w1_ref[kind, pr])
            pterm_ref[kind] = acc

    for j in range(CMP_PG):
        for kind in range(2):
            for k in range(N_KV):
                if paged:
                    slab = _kv_head_rows(page_refs[j * 2 + kind], k)
                else:
                    kk = kind * N_KV + k
                    slab = page_refs[j][:, kk * HEAD_DIM:(kk + 1) * HEAD_DIM]
                for s in range(N_SEG):
                    slab_ref[(j * 2 + kind) * N_KV + k, pl.ds(s * SLAB_PITCH, CMP_STRIDE), :] = (
                        slab[s * CMP_STRIDE:(s + 1) * CMP_STRIDE])

    for j in range(CMP_PG):
        seg0 = (q * CMP_PG + j) * N_SEG
        for kind in range(2):
            for k in range(N_KV):
                for l in range(CMP_STRIDE):
                    piece = slab_ref[(j * 2 + kind) * N_KV + k, pl.ds(l, N_SEG, stride=SLAB_PITCH), :]
                    stage_ref[kind, l // 2, pl.ds(k * SEG_ROWS + seg0, N_SEG),
                              pl.ds((l % 2) * HEAD_DIM, HEAD_DIM)] = piece

    @pl.when(q == N_PAGES // CMP_PG - 1)
    def _():
        for kind, out_ref in ((0, ck_ref), (1, cv_ref)):
            acc = jnp.zeros((N_KV * SEG_ROWS, 2 * CMP_HID), F32)
            for pr in range(N_PAIR):
                acc = acc + _dot(stage_ref[kind, pr].astype(BF16), w1_ref[kind, pr])
            nxt = pltpu.roll(acc[:, CMP_HID:], N_KV * SEG_ROWS - 1, axis=0)
            pt = pterm_ref[kind]
            posterm = pt[0:1, :CMP_HID] + pt[1:2, CMP_HID:] + b1_ref[kind]
            hid = acc[:, :CMP_HID] + nxt + posterm
            out_ref[0] = _dot(jax.nn.gelu(hid).astype(BF16), w2_ref[kind]).astype(BF16)


def _compress(src, pt_flat, col_blk, n_seq, w1, pos, b1, w2, name):
    paged = col_blk is None
    page_of = lambda n, p, pt, j: pt[n * N_PAGES + p * CMP_PG + j]
    out_spec = pl.BlockSpec((1, N_KV * SEG_ROWS, HEAD_DIM), lambda n, p, pt: (n, 0, 0))
    if paged:
        page_specs = [pl.BlockSpec((None, None, PAGE_SIZE, None, N_KV, HEAD_DIM),
                                   lambda n, p, pt, j=j, kind=kind: (0, page_of(n, p, pt, j), 0, kind, 0, 0))
                      for j in range(CMP_PG) for kind in range(2)]
    else:
        page_specs = [pl.BlockSpec((None, PAGE_SIZE, 2 * N_KV * HEAD_DIM),
                                   lambda n, p, pt, j=j: (page_of(n, p, pt, j), 0, col_blk))
                      for j in range(CMP_PG)]
    grid_spec = pltpu.PrefetchScalarGridSpec(
        num_scalar_prefetch=1,
        grid=(n_seq, N_PAGES // CMP_PG),
        in_specs=page_specs + [
            pl.BlockSpec((2, N_PAIR, 256, 2 * CMP_HID), lambda n, p, pt: (0, 0, 0, 0)),
            pl.BlockSpec((8, CMP_STRIDE * HEAD_DIM), lambda n, p, pt: (0, 0)),
            pl.BlockSpec((2, 1, CMP_HID), lambda n, p, pt: (0, 0, 0)),
            pl.BlockSpec((2, CMP_HID, HEAD_DIM), lambda n, p, pt: (0, 0, 0)),
        ],
        out_specs=[out_spec, out_spec],
        scratch_shapes=[pltpu.VMEM((2, N_PAIR, N_KV * SEG_ROWS, 256), F32),
                        pltpu.VMEM((2, 8, 2 * CMP_HID), F32),
                        pltpu.VMEM((CMP_PG * 2 * N_KV, N_SEG * SLAB_PITCH, HEAD_DIM), F32)],
    )
    shp = jax.ShapeDtypeStruct((n_seq, N_KV * SEG_ROWS, HEAD_DIM), BF16)
    return pl.pallas_call(
        functools.partial(_compress_body, paged=paged),
        grid_spec=grid_spec,
        out_shape=[shp, shp],
        compiler_params=_cparams(2),
        name=name,
    )(pt_flat, *([src] * len(page_specs)), w1, pos, b1, w2)


def _select_blocks(score, cur, n_blk):
    jj = lax.broadcasted_iota(jnp.int32, score.shape, 1)
    forced = (jj == 0) | (jj == cur) | (jj == cur - 1)
    sc = jnp.where(forced, BIG, jnp.where(jj <= cur, score, NEG))
    rank = jnp.zeros(score.shape, F32)
    for i in range(n_blk):
        si = sc[:, i:i + 1]
        beats = (si > sc) | ((si == sc) & (jj > i))
        rank = rank + jnp.where(beats, 1.0, 0.0)
    sel = (rank < float(min(N_SEL, n_blk))) & (jj <= cur) & (jj < n_blk)
    return jnp.where(sel, 1.0, 0.0).astype(BF16)


def _select_blocks_t(score, cur):
    n_blk = score.shape[0]
    jj = lax.broadcasted_iota(jnp.int32, score.shape, 0)
    forced = (jj == 0) | (jj == cur) | (jj == cur - 1)
    sc = jnp.where(forced, BIG, jnp.where(jj <= cur, score, NEG))
    rank = jnp.zeros(score.shape, F32)
    for i in range(n_blk):
        si = sc[i:i + 1, :]
        beats = (si > sc) | ((si == sc) & (jj > i))
        rank = rank + jnp.where(beats, 1.0, 0.0)
    sel = (rank < float(min(N_SEL, n_blk))) & (jj <= cur)
    return jnp.where(sel, 1.0, 0.0)


def _softmax_rows(logits):
    m = jnp.maximum(jnp.max(logits, axis=-1, keepdims=True), M_FLOOR)
    e = jnp.exp(logits - m)
    s = jnp.sum(e, axis=-1, keepdims=True)
    return e / jnp.where(s > 0.0, s, 1.0)


MASKED_TILE = WINDOW // 128 + 1
ATT_HP = 2


def _attn_prompt_body(zq_ref, zg_ref, ck_ref, cv_ref, ks_ref, vs_ref, kw_ref, vw_ref,
                      bc_ref, tz_ref, ovt_ref, e_ref, o_ref,
                      ksb, vsb, kwb, vwb, selm_ref, s_ref, mel_ref, lel_ref, acc_ref):
    qt = pl.program_id(2)
    rows = GROUP * 128

    @pl.when(qt == 0)
    def _():
        ksb[...] = ks_ref[...].astype(BF16)
        vsb[...] = vs_ref[...].astype(BF16)
        kwb[...] = kw_ref[...].astype(BF16)
        vwb[...] = vw_ref[...].astype(BF16)

    heads = range(ATT_HP)
    lanes = lambda h: slice(h * HEAD_DIM, (h + 1) * HEAD_DIM)
    groups = lambda h: slice(h * GROUP, (h + 1) * GROUP)
    n_blk = SEQ // SEL_BLK
    cur = jnp.right_shift(qt * 128 + lax.broadcasted_iota(jnp.int32, (n_blk, 128), 1), 6)
    qq, o_c = [], []
    for h in heads:
        q = zq_ref[:, h * GROUP * HEAD_DIM:(h + 1) * GROUP * HEAD_DIM] * Q_SCALE
        qh = jnp.concatenate([q[:, g * HEAD_DIM:(g + 1) * HEAD_DIM] for g in range(GROUP)], axis=0).astype(BF16)
        qq.append(qh)
        pc = _softmax_rows(_dot_nt(qh, ck_ref[h]) + bc_ref[0, groups(h)].reshape(rows, 128))
        o_c.append(_dot(pc.astype(BF16), cv_ref[h]))
        ps = pc[0:128] + pc[128:256] + pc[256:384] + pc[384:512]
        sel_t = _select_blocks_t(_dot_nt_split3(ovt_ref[...], ps), cur)
        sel = jnp.concatenate([sel_t, jnp.zeros((128 - n_blk, 128), F32)], axis=0).T.astype(BF16)
        for j in range(SEQ // 256):
            selm_ref[h, j] = (_dot(sel, e_ref[j]) - 1.0) * BIG

    def attend(k_ref, v_ref, lo, tile_of_delta, use_sel):
        hi = jnp.right_shift(qt, 1) + 1
        mel_ref[...] = jnp.full(mel_ref.shape, M_FLOOR, F32)

        def logits_pass(j, carry):
            off = pl.multiple_of(j * 256, 256)
            d0 = qt - 2 * j
            i0, i1 = tile_of_delta(d0), tile_of_delta(d0 - 1)
            for h in heads:
                s = _dot_nt(qq[h], k_ref[pl.ds(off, 256), lanes(h)]).reshape(GROUP, 128, 256)
                s = s + jnp.concatenate([tz_ref[i0, groups(h)], tz_ref[i1, groups(h)]], axis=-1)
                if use_sel:
                    s = s + selm_ref[h, j][None]
                s = s.reshape(rows, 256)
                s_ref[h, j] = s
                mel_ref[h] = jnp.maximum(mel_ref[h], jnp.maximum(s[:, :128], s[:, 128:]))
            return carry

        lax.fori_loop(lo, hi, logits_pass, 0)
        m = [jnp.max(mel_ref[h], axis=-1, keepdims=True) for h in heads]
        lel_ref[...] = jnp.zeros_like(lel_ref)
        acc_ref[...] = jnp.zeros_like(acc_ref)

        def value_pass(j, carry):
            off = pl.multiple_of(j * 256, 256)
            for h in heads:
                pe = jnp.exp(s_ref[h, j] - m[h])
                lel_ref[h] += pe[:, :128] + pe[:, 128:]
                acc_ref[h] += _dot(pe.astype(BF16), v_ref[pl.ds(off, 256), lanes(h)])
            return carry

        lax.fori_loop(lo, hi, value_pass, 0)
        outs = []
        for h in heads:
            l = jnp.sum(lel_ref[h], axis=-1, keepdims=True)
            outs.append(acc_ref[h] / jnp.where(l > 0.0, l, 1.0))
        return outs

    n_win = WINDOW // 128
    o_s = attend(ksb, vsb, 0, lambda d: jnp.where(d < 0, MASKED_TILE, jnp.minimum(d, 2)), True)
    o_w = attend(kwb, vwb, jnp.right_shift(jnp.maximum(qt - n_win, 0), 1),
                 lambda d: jnp.where((d < 0) | (d > n_win), MASKED_TILE, d), False)

    gates = jax.nn.sigmoid(zg_ref[...])
    outs = []
    for h in heads:
        for g in range(GROUP):
            r = slice(g * 128, (g + 1) * 128)
            gate = lambda branch: gates[:, h * 128 + branch * GROUP + g:h * 128 + branch * GROUP + g + 1]
            outs.append(gate(0) * o_c[h][r] + gate(1) * o_s[h][r] + gate(2) * o_w[h][r])
    o_ref[...] = jnp.concatenate(outs, axis=1).astype(BF16)


def _attn_prompt(z, ck, cv, bias_cmp, tz, ov, emat):
    nq = SEQ // 128
    nhp = N_KV // ATT_HP
    kw = ATT_HP * HEAD_DIM
    qw = ATT_HP * GROUP * HEAD_DIM
    kv_col = lambda base, kind: (lambda n, k, t: (n, (base + kind * N_KV * HEAD_DIM) // kw + k))
    kvspec = lambda base, kind: pl.BlockSpec((SEQ, kw), kv_col(base, kind))
    rows = GROUP * 128
    return pl.pallas_call(
        _attn_prompt_body,
        grid=(BATCH, nhp, nq),
        in_specs=[
            pl.BlockSpec((128, qw), lambda n, k, t: (n * nq + t, C_Q // qw + k)),
            pl.BlockSpec((128, ATT_HP * 128), lambda n, k, t: (n * nq + t, C_GNSA // (ATT_HP * 128) + k)),
            pl.BlockSpec((ATT_HP, SEG_ROWS, HEAD_DIM), lambda n, k, t: (n * nhp + k, 0, 0)),
            pl.BlockSpec((ATT_HP, SEG_ROWS, HEAD_DIM), lambda n, k, t: (n * nhp + k, 0, 0)),
            kvspec(C_PAG, 2), kvspec(C_PAG, 3), kvspec(C_WIN, 0), kvspec(C_WIN, 1),
            pl.BlockSpec((1, ATT_HP * GROUP, 128, 128), lambda n, k, t: (t, k, 0, 0)),
            pl.BlockSpec((MASKED_TILE + 1, ATT_HP * GROUP, 128, 128), lambda n, k, t: (0, k, 0, 0)),
            pl.BlockSpec((SEQ // SEL_BLK, 128), lambda n, k, t: (0, 0)),
            pl.BlockSpec((SEQ // 256, 128, 256), lambda n, k, t: (0, 0, 0)),
        ],
        out_specs=pl.BlockSpec((128, qw), lambda n, k, t: (n * nq + t, k)),
        out_shape=jax.ShapeDtypeStruct((M_PROMPT, N_HEADS * HEAD_DIM), BF16),
        scratch_shapes=[pltpu.VMEM((SEQ, kw), BF16)] * 4 + [
            pltpu.VMEM((ATT_HP, SEQ // 256, 128, 256), F32),
            pltpu.VMEM((ATT_HP, SEQ // 256, rows, 256), F32),
            pltpu.VMEM((ATT_HP, rows, 128), F32), pltpu.VMEM((ATT_HP, rows, 128), F32),
            pltpu.VMEM((ATT_HP, rows, HEAD_DIM), F32)],
        compiler_params=_cparams(3),
        name="attn_prompt",
    )(z, z, ck, cv, z, z, z, z, bias_cmp, tz, ov, emat)


S_ROWS = GROUP * N_KV * DEC_SEQ


ATT_SS = 2


def _attn_sample_body(pt_ref, q_ref, gs_ref, ck_ref, cv_ref, *rest):
    n_pg = ATT_SS * 2 * N_PAGES
    page_refs = rest[:n_pg]
    win_refs = rest[n_pg:n_pg + 2 * ATT_SS]
    (nkv_ref, nwin_ref, bcmp_ref, bsel_ref, bnew_ref, bwin_ref, ov_ref, e_ref,
     o_ref, s_ref, nk_ref, nw_ref) = rest[n_pg + 2 * ATT_SS:]
    kv_of_row = jnp.bitwise_and(jnp.right_shift(lax.broadcasted_iota(jnp.int32, (S_ROWS, 1), 0), 2), N_KV - 1)
    col = lambda k, half: pl.ds(half * N_KV * HEAD_DIM + k * HEAD_DIM, HEAD_DIM)
    head_rows = _kv_head_rows
    rowmax = lambda s: jnp.max(s, axis=-1, keepdims=True)
    rowsum = lambda s: jnp.sum(s, axis=-1, keepdims=True)

    @pl.when(pl.program_id(0) == 0)
    def _():
        nk_ref[...] = jnp.zeros_like(nk_ref)
        nw_ref[...] = jnp.zeros_like(nw_ref)

    def one_sequence(i):
        pages = page_refs[i * 2 * N_PAGES:(i + 1) * 2 * N_PAGES]
        kwin_ref, vwin_ref = win_refs[2 * i], win_refs[2 * i + 1]
        qq = (q_ref[i] * Q_SCALE).astype(BF16)

        def logits(get_k):
            out = None
            for k in range(N_KV):
                s = jnp.where(kv_of_row == k, _dot_nt(qq, get_k(k).astype(BF16)), 0.0)
                out = s if out is None else out + s
            return out

        def weighted(pe, get_v):
            out = None
            for k in range(N_KV):
                o = _dot(jnp.where(kv_of_row == k, pe, 0.0).astype(BF16), get_v(k).astype(BF16))
                out = o if out is None else out + o
            return out

        nk_ref[i, 0:8, :] = nkv_ref[i]
        nw_ref[i, 0:8, :] = nwin_ref[i]

        pc = _softmax_rows(logits(lambda k: ck_ref[i, pl.ds(k * SEG_ROWS, SEG_ROWS), :]) + bcmp_ref[...])
        o_c = weighted(pc, lambda k: cv_ref[i, pl.ds(k * SEG_ROWS, SEG_ROWS), :])
        ps = pc + pltpu.roll(pc, 16, axis=0) + pltpu.roll(pc, 32, axis=0) + pltpu.roll(pc, 48, axis=0)
        score = _dot_split3(ps, ov_ref[...])
        n_blk = -(-(PAST_LEN + DEC_SEQ) // SEL_BLK)
        cur = jnp.full((S_ROWS, 128), PAST_LEN // SEL_BLK, jnp.int32)
        sel = _select_blocks(score, cur, n_blk)
        key_mask = (_dot(sel, e_ref[...]) - 1.0) * BIG

        m = jnp.full((S_ROWS, 1), M_FLOOR, F32)
        for p in range(N_PAGES):
            s = (logits(lambda k: head_rows(pages[2 * p], k)) + bsel_ref[p]
                 + key_mask[:, p * PAGE_SIZE:(p + 1) * PAGE_SIZE])
            s_ref[i, p] = s
            m = jnp.maximum(m, rowmax(s))
        sn = logits(lambda k: nk_ref[i, :, col(k, 0)]) + bnew_ref[...]
        m = jnp.maximum(m, rowmax(sn))
        pn = jnp.exp(sn - m)
        l = rowsum(pn)
        acc = weighted(pn, lambda k: nk_ref[i, :, col(k, 1)])
        for p in range(N_PAGES):
            pe = jnp.exp(s_ref[i, p] - m)
            l = l + rowsum(pe)
            acc = acc + weighted(pe, lambda k: head_rows(pages[2 * p + 1], k))
        o_s = acc / jnp.where(l > 0.0, l, 1.0)

        sw = logits(lambda k: head_rows(kwin_ref, k)) + bwin_ref[...]
        sn = logits(lambda k: nw_ref[i, :, col(k, 0)]) + bnew_ref[...]
        m = jnp.maximum(jnp.maximum(rowmax(sw), rowmax(sn)), M_FLOOR)
        pw = jnp.exp(sw - m)
        pn = jnp.exp(sn - m)
        l = rowsum(pw) + rowsum(pn)
        o_w = weighted(pw, lambda k: head_rows(vwin_ref, k)) + weighted(pn, lambda k: nw_ref[i, :, col(k, 1)])
        o_w = o_w / jnp.where(l > 0.0, l, 1.0)

        gates = jax.nn.sigmoid(gs_ref[i])
        o_ref[i] = gates[:, 0:1] * o_c + gates[:, 1:2] * o_s + gates[:, 2:3] * o_w

    for i in range(ATT_SS):
        one_sequence(i)


def _attn_sample(pt_flat, q_s, g_s, ck, cv, cache_kv, nkv, state_win, nwin, bcmp, bsel, bnew, bwin, ov, emat):
    const2 = lambda shape: pl.BlockSpec(shape, lambda n, pt: (0, 0))
    per_seq = lambda rows, width: pl.BlockSpec((ATT_SS, rows, width), lambda n, pt: (n, 0, 0))
    page_specs = [pl.BlockSpec((None, None, PAGE_SIZE, None, N_KV, HEAD_DIM),
                               lambda n, pt, i=i, p=p, kind=kind:
                               (0, pt[(n * ATT_SS + i) * N_PAGES + p], 0, kind, 0, 0))
                  for i in range(ATT_SS) for p in range(N_PAGES) for kind in (2, 3)]
    win_specs = [pl.BlockSpec((None, None, WINDOW, None, N_KV, HEAD_DIM),
                              lambda n, pt, i=i, kind=kind: (0, n * ATT_SS + i, 0, kind, 0, 0))
                 for i in range(ATT_SS) for kind in (0, 1)]
    grid_spec = pltpu.PrefetchScalarGridSpec(
        num_scalar_prefetch=1,
        grid=(DEC_BATCH // ATT_SS,),
        in_specs=[
            per_seq(S_ROWS, HEAD_DIM), per_seq(S_ROWS, 128),
            per_seq(N_KV * SEG_ROWS, HEAD_DIM), per_seq(N_KV * SEG_ROWS, HEAD_DIM),
        ] + page_specs + win_specs + [
            per_seq(8, 1024), per_seq(8, 1024),
            const2((S_ROWS, 128)),
            pl.BlockSpec((N_PAGES, S_ROWS, 128), lambda n, pt: (0, 0, 0)),
            const2((S_ROWS, 128)),
            const2((S_ROWS, WINDOW)),
            const2((128, 128)),
            const2((128, PAST_LEN)),
        ],
        out_specs=per_seq(S_ROWS, HEAD_DIM),
        scratch_shapes=[
            pltpu.VMEM((ATT_SS, N_PAGES, S_ROWS, 128), F32),
            pltpu.VMEM((ATT_SS, 128, 1024), F32), pltpu.VMEM((ATT_SS, 128, 1024), F32),
        ],
    )
    return pl.pallas_call(
        _attn_sample_body,
        grid_spec=grid_spec,
        out_shape=jax.ShapeDtypeStruct((DEC_BATCH, S_ROWS, HEAD_DIM), F32),
        compiler_params=_cparams(1),
        name="attn_sample",
    )(pt_flat, q_s, g_s, ck, cv, *([cache_kv] * len(page_specs)), *([state_win] * len(win_specs)), nkv, nwin,
      bcmp, bsel, bnew, bwin, ov, emat)


def _t5_bucket(dist):
    d = jnp.maximum(dist, 0)
    df = jnp.maximum(d, 1).astype(F32)
    large = MAX_EXACT + (jnp.log(df / MAX_EXACT) / math.log(MAX_DIST / MAX_EXACT)
                         * (N_BUCKETS - MAX_EXACT)).astype(jnp.int32)
    large = jnp.minimum(large, N_BUCKETS - 1)
    return jnp.where(d < MAX_EXACT, d, large)


def _bias_lookup_body(rb_ref, idx_ref, o_ref):
    idx = idx_ref[0]
    for h in range(N_HEADS):
        acc = jnp.full(idx.shape, NEG, F32)
        for b in range(N_BUCKETS):
            acc = jnp.where(idx == b, rb_ref[b * N_HEADS + h], acc)
        o_ref[0, h] = acc


def _bias_table(rel_bias, dist, valid, name):
    p, r, _ = dist.shape
    idx = jnp.where(jnp.asarray(valid), _t5_bucket(jnp.asarray(dist, jnp.int32)), -1)
    return pl.pallas_call(
        _bias_lookup_body,
        grid=(p,),
        in_specs=[pl.BlockSpec(memory_space=pltpu.SMEM), pl.BlockSpec((1, r, 128), lambda i: (i, 0, 0))],
        out_specs=pl.BlockSpec((1, N_HEADS, r, 128), lambda i: (i, 0, 0, 0)),
        out_shape=jax.ShapeDtypeStruct((p, N_HEADS, r, 128), F32),
        compiler_params=_cparams(1),
        name=name,
    )(rel_bias.astype(F32).reshape(-1), idx)


def _overlap(nc, nb):
    cs = np.arange(nc)[:, None] * CMP_STRIDE
    js = np.arange(nb)[None, :] * SEL_BLK
    ov = np.clip(np.minimum(cs + CMP_LEN, js + SEL_BLK) - np.maximum(cs, js), 0, None) / CMP_LEN
    out = np.zeros((128, 128), np.float32)
    out[:nc, :nb] = ov
    return jnp.asarray(out, BF16)


def _position_tables(rel_bias):
    nc = SEG_ROWS - 1
    t = np.arange(128)[None, :, None]
    c = np.arange(128)[None, None, :]
    cend = c * CMP_STRIDE + CMP_LEN - 1
    d = np.arange(MASKED_TILE + 1)[:, None, None] * 128 + t - c
    tz = _bias_table(rel_bias, d, (d >= 0) & (d < WINDOW), "bias_tiles")
    d = np.arange(SEQ // 128)[:, None, None] * 128 + t - cend
    bias_cmp = _bias_table(rel_bias, d, (d >= 0) & (c < nc), "bias_cmp")
    ts = np.arange(8)[None, :, None]
    qpos = PAST_LEN + ts
    live = ts < DEC_SEQ
    d_cmp = qpos - cend
    d_sel = qpos - (np.arange(N_PAGES)[:, None, None] * PAGE_SIZE + c)
    d_new = ts - c
    d_win = qpos - (PAST_LEN - WINDOW + np.arange(WINDOW // 128)[:, None, None] * 128 + c)
    d = np.concatenate([d_cmp, d_sel, d_new, d_win], axis=0)
    valid = np.concatenate([(d_cmp >= 0) & (c < nc), d_sel >= 0, (d_new >= 0) & (c < DEC_SEQ),
                            (d_win >= 0) & (d_win < WINDOW)], axis=0) & live
    o = _bias_table(rel_bias, d, valid, "bias_sample")[:, :, :DEC_SEQ]
    o = jnp.transpose(o.reshape(-1, N_KV, GROUP, DEC_SEQ, 128), (0, 2, 1, 3, 4)).reshape(-1, S_ROWS, 128)
    bcmp, bsel, bnew = o[0], o[1:1 + N_PAGES], o[1 + N_PAGES]
    bwin = jnp.transpose(o[2 + N_PAGES:], (1, 0, 2)).reshape(S_ROWS, WINDOW)
    keys = np.arange(SEQ)
    e_all = (np.arange(128)[:, None] == (keys // SEL_BLK)[None, :]).astype(np.float32)
    e_tiles = jnp.asarray(e_all.reshape(128, SEQ // 256, 256).transpose(1, 0, 2), BF16)
    ovt_p = jnp.transpose(_overlap(nc, SEQ // SEL_BLK))[:SEQ // SEL_BLK]
    return dict(tz=tz, bias_cmp=bias_cmp, bcmp=bcmp, bsel=bsel, bnew=bnew, bwin=bwin,
                ovt_p=ovt_p, ov_s=_overlap(nc, -(-(PAST_LEN + DEC_SEQ) // SEL_BLK)),
                e_tiles=e_tiles, e_all=jnp.asarray(e_all, BF16))


W_TC = 512
W_TR = 2048
N_GNSA = 3 * N_HEADS


def _permute_w_in_body(a_ref, b_ref, g_ref, s_ref, o_ref):
    j = pl.program_id(1)
    first, last = C_GRNN // W_TC, C_GNSA // W_TC

    @pl.when(j < first)
    def _():
        o_ref[...] = a_ref[...].astype(BF16)

    @pl.when((j >= first) & (j < last))
    def _():
        lane = lax.broadcasted_iota(jnp.int32, b_ref.shape, 1)
        b = jnp.where(lane < N_GNSA, b_ref[...], 0.0).astype(BF16)
        o_ref[...] = (_dot(a_ref[...].astype(BF16), s_ref[0]) + _dot(b, s_ref[1])).astype(BF16)

    @pl.when(j == last)
    def _():
        o_ref[...] = g_ref[...]


def _permute_w_in(w_in):
    c0 = C_GRNN
    g_nsa = w_in[0, :, c0:c0 + N_GNSA].reshape(D_MODEL, N_KV, GROUP, 3)
    g_nsa = jnp.transpose(g_nsa, (0, 1, 3, 2)).reshape(D_MODEL, N_KV, 3 * GROUP)
    g_nsa = jnp.pad(g_nsa, ((0, 0), (0, 0), (0, 128 - 3 * GROUP))).reshape(D_MODEL, N_KV * 128).astype(BF16)
    r = np.arange(W_TC)[:, None]
    c = np.arange(W_TC)[None, :]
    shift = jnp.asarray(np.stack([r == c + N_GNSA, r == c + N_GNSA - W_TC]).astype(np.float32), BF16)
    first, last = C_GRNN // W_TC, C_GNSA // W_TC
    return pl.pallas_call(
        _permute_w_in_body,
        grid=(D_MODEL // W_TR, D_Z // W_TC),
        in_specs=[
            pl.BlockSpec((None, W_TR, W_TC), lambda i, j: (0, i, jnp.minimum(j, last - 1))),
            pl.BlockSpec((None, W_TR, W_TC), lambda i, j: (0, i, jnp.clip(j + 1, first + 1, last))),
            pl.BlockSpec((W_TR, W_TC), lambda i, j: (i, 0)),
            pl.BlockSpec((2, W_TC, W_TC), lambda i, j: (0, 0, 0)),
        ],
        out_specs=pl.BlockSpec((W_TR, W_TC), lambda i, j: (i, j)),
        out_shape=jax.ShapeDtypeStruct((D_MODEL, D_Z), BF16),
        compiler_params=_cparams(2),
        name="w_in_layout",
    )(w_in, w_in, g_nsa, shift)


def _cmp_weights(w1_k, w1_v, b1_k, b1_v, w2_k, w2_v, pos):
    def cat(w1):
        w = w1.reshape(2, CMP_STRIDE * HEAD_DIM, CMP_HID)
        return jnp.concatenate([w[0], w[1]], axis=1).reshape(N_PAIR, 256, 2 * CMP_HID)
    w1 = jnp.stack([cat(w1_k), cat(w1_v)]).astype(BF16)
    posm = jnp.pad(pos.reshape(2, CMP_STRIDE * HEAD_DIM), ((0, 6), (0, 0)))
    b1 = jnp.stack([b1_k, b1_v]).reshape(2, 1, CMP_HID)
    w2 = jnp.stack([w2_k, w2_v]).astype(BF16)
    return w1, posm, b1, w2


def kernel(x_prompt, x_sample, cache_kv, page_table, state_win, state_conv, state_h, rel_bias, ln_final, ln_ffn1, w_ffn1_gate, w_ffn1_up, w_ffn1_down, ln_mix, w_in, conv_w, conv_b, rg_wa, rg_ba, rg_wi, rg_bi, rg_lambda, cmp_pos, cmp_k_w1, cmp_k_b1, cmp_k_w2, cmp_v_w1, cmp_v_b1, cmp_v_w2, w_br_rnn, w_br_attn, w_out, ln_ffn2, w_ffn2_gate, w_ffn2_up, w_ffn2_down):
    tabs = _position_tables(rel_bias)
    x = jnp.concatenate([x_prompt.reshape(M_PROMPT, D_MODEL), x_sample.reshape(M_SAMPLE, D_MODEL)], axis=0)

    x = _ffn(x, ln_ffn1[0], w_ffn1_gate[0].astype(BF16), w_ffn1_up[0].astype(BF16), w_ffn1_down[0].astype(BF16),
             ln_final, False)
    z = _in_proj(x, ln_mix[0], _permute_w_in(w_in))
    z_s = z[M_PROMPT:]

    vec = lambda v: v.reshape(1, D_RNN)
    rnn_w = (conv_w[0], vec(conv_b[0]), rg_wa[0].astype(BF16), vec(rg_ba[0]), rg_wi[0].astype(BF16),
             vec(rg_bi[0]), vec(rg_lambda[0]))
    g_p, h_p = _rnn_prompt(z, *rnn_w)
    tmajor = lambda a: jnp.transpose(a.reshape(DEC_BATCH, -1, D_RNN), (1, 0, 2))
    g_s, h_s = _rnn_sample(tmajor(z_s[:, C_UGATE:C_UGATE + D_RNN]), tmajor(z_s[:, C_UX:C_UX + D_RNN]),
                           tmajor(state_conv[0]), state_h[0], *rnn_w)
    grnn = jnp.concatenate([g_p, jnp.transpose(g_s, (1, 0, 2)).reshape(M_SAMPLE, D_RNN)], axis=0)

    cw = _cmp_weights(cmp_k_w1[0], cmp_v_w1[0], cmp_k_b1[0], cmp_v_b1[0], cmp_k_w2[0], cmp_v_w2[0], cmp_pos[0])
    pt_prompt = jnp.arange(BATCH * N_PAGES, dtype=jnp.int32)
    pt_sample = page_table.reshape(-1).astype(jnp.int32)
    ck_p, cv_p = _compress(z.reshape(M_TOK // PAGE_SIZE, PAGE_SIZE, D_Z), pt_prompt, C_PAG // 1024, BATCH, *cw,
                           name="compress_prompt")
    ck_s, cv_s = _compress(cache_kv, pt_sample, None, DEC_BATCH, *cw, name="compress_sample")
    o_p = _attn_prompt(z, ck_p.reshape(BATCH * N_KV, SEG_ROWS, HEAD_DIM), cv_p.reshape(BATCH * N_KV, SEG_ROWS, HEAD_DIM),
                       tabs["bias_cmp"], tabs["tz"], tabs["ovt_p"], tabs["e_tiles"])

    def rows_gkt(a, width):
        a = a.reshape(DEC_BATCH, DEC_SEQ, N_KV, GROUP, width)
        return jnp.transpose(a, (0, 3, 2, 1, 4)).reshape(DEC_BATCH, S_ROWS, width)

    q_s = rows_gkt(z_s[:, C_Q:C_Q + N_HEADS * HEAD_DIM], HEAD_DIM)
    gn = z_s[:, C_GNSA:].reshape(M_SAMPLE, N_KV, 128)[:, :, :3 * GROUP].reshape(M_SAMPLE, N_KV, 3, GROUP)
    g_s3 = jnp.pad(rows_gkt(jnp.transpose(gn, (0, 1, 3, 2)), 3), ((0, 0), (0, 0), (0, 125)))
    pad8 = lambda a: jnp.pad(a.reshape(DEC_BATCH, DEC_SEQ, -1), ((0, 0), (0, 8 - DEC_SEQ), (0, 0)))
    nkv = pad8(z_s[:, C_PAG + 2 * N_KV * HEAD_DIM:C_PAG + 4 * N_KV * HEAD_DIM])
    nwin = pad8(z_s[:, C_WIN:C_WIN + 2 * N_KV * HEAD_DIM])
    o_s = _attn_sample(pt_sample, q_s, g_s3, ck_s, cv_s, cache_kv, nkv, state_win, nwin,
                       tabs["bcmp"], tabs["bsel"], tabs["bnew"], tabs["bwin"], tabs["ov_s"], tabs["e_all"])
    o_s = jnp.transpose(o_s.reshape(DEC_BATCH, GROUP, N_KV, DEC_SEQ, HEAD_DIM), (0, 3, 2, 1, 4))
    oattn = jnp.concatenate([o_p, o_s.reshape(M_SAMPLE, N_HEADS * HEAD_DIM).astype(BF16)], axis=0)

    merged = _merge(z, grnn, oattn, w_br_rnn[0].astype(BF16), w_br_attn[0].astype(BF16))
    x = _out_proj(x, merged, w_out[0].astype(BF16))
    ffn2 = functools.partial(_ffn, x, ln_ffn2[0], w_ffn2_gate[0].astype(BF16), w_ffn2_up[0].astype(BF16),
                             w_ffn2_down[0].astype(BF16), ln_final, True)
    y_p = ffn2(tile0=0, n_tiles=M_PROMPT // TM)
    y_s = ffn2(tile0=M_PROMPT // TM, n_tiles=M_SAMPLE // TM)

    kv = z[:, C_PAG:C_PAG + 4 * N_KV * HEAD_DIM]
    wn = z[:, C_WIN:C_WIN + 2 * N_KV * HEAD_DIM]
    keep = CONV_W - 1
    conv_p = jnp.stack([lax.slice(z, ((n + 1) * SEQ - keep, C_UX), ((n + 1) * SEQ, C_UX + D_RNN))
                        for n in range(BATCH)])
    conv_s = z_s[:, C_UX:C_UX + D_RNN].reshape(DEC_BATCH, DEC_SEQ, D_RNN)[:, DEC_SEQ - keep:]
    win_p = wn[:M_PROMPT].reshape(BATCH, SEQ, 2, N_KV, HEAD_DIM)[:, SEQ - WINDOW:]
    win_s = jnp.concatenate([state_win.reshape(DEC_BATCH, WINDOW, 2, N_KV, HEAD_DIM),
                             wn[M_PROMPT:].reshape(DEC_BATCH, DEC_SEQ, 2, N_KV, HEAD_DIM)], axis=1)
    return (
        y_p.reshape(BATCH, SEQ, D_MODEL),
        y_s.reshape(DEC_BATCH, DEC_SEQ, D_MODEL),
        kv[:M_PROMPT].reshape(1, BATCH, SEQ, 4, N_KV, HEAD_DIM),
        kv[M_PROMPT:].reshape(1, DEC_BATCH, DEC_SEQ, 4, N_KV, HEAD_DIM),
        win_p[None],
        win_s[None, :, DEC_SEQ:],
        conv_p[None],
        conv_s[None],
        h_p[None, :, 7],
        h_s[None],
    )
```

```python
import functools
import math

import numpy as np
import jax
import jax.numpy as jnp
from jax import lax
from jax.experimental import pallas as pl
from jax.experimental.pallas import tpu as pltpu

F32 = jnp.float32
BF16 = jnp.bfloat16

D_MODEL = 4096
BATCH = 4
SEQ = 2048
DEC_BATCH = 128
DEC_SEQ = 4
PAST_LEN = 2048
PAGE_SIZE = 128
N_PAGES = PAST_LEN // PAGE_SIZE
D_RNN = D_MODEL // 2
RNN_BLOCKS = 16
RNN_BW = D_RNN // RNN_BLOCKS
CONV_W = 4
LRU_C = 8.0
N_HEADS = 16
HEAD_DIM = 128
N_KV = 4
GROUP = N_HEADS // N_KV
CMP_LEN = 32
CMP_STRIDE = 16
CMP_HID = 2 * HEAD_DIM
SEL_BLK = 64
N_SEL = 8
WINDOW = 512
N_BUCKETS = 32
MAX_EXACT = 16
MAX_DIST = 128
D_FF = ((8 * D_MODEL // 3 + 255) // 256) * 256
EPS = 1e-6
NEG = -1e30
BIG = 1e30
M_FLOOR = -1e29
Q_SCALE = HEAD_DIM ** -0.5

M_PROMPT = BATCH * SEQ
M_SAMPLE = DEC_BATCH * DEC_SEQ
M_TOK = M_PROMPT + M_SAMPLE

C_UGATE = 0
C_UX = C_UGATE + D_RNN
C_Q = C_UX + D_RNN
C_PAG = C_Q + N_HEADS * HEAD_DIM
C_WIN = C_PAG + 4 * N_KV * HEAD_DIM
C_GRNN = C_WIN + 2 * N_KV * HEAD_DIM
C_GATTN = C_GRNN + D_MODEL
C_GNSA = C_GATTN + D_MODEL
D_Z = C_GNSA + N_KV * 128

TM = 512
TF = 256
TN_IN = 1280
TN_MM = 1024
VMEM_LIMIT = 56 * 2 ** 20


def _cparams(n_axes, vmem=VMEM_LIMIT):
    return pltpu.CompilerParams(dimension_semantics=("arbitrary",) * n_axes, vmem_limit_bytes=vmem)


def _dot(a, b):
    return jnp.dot(a, b, preferred_element_type=F32)


def _dot_nt(a, b):
    return lax.dot_general(a, b, (((1,), (1,)), ((), ())), preferred_element_type=F32)


def _dot_split3(a, b):
    a1 = a.astype(BF16)
    r1 = a - a1.astype(F32)
    a2 = r1.astype(BF16)
    a3 = (r1 - a2.astype(F32)).astype(BF16)
    return _dot(a1, b) + _dot(a2, b) + _dot(a3, b)


def _dot_nt_split3(a, b):
    b1 = b.astype(BF16)
    r1 = b - b1.astype(F32)
    b2 = r1.astype(BF16)
    b3 = (r1 - b2.astype(F32)).astype(BF16)
    return _dot_nt(a, b1) + _dot_nt(a, b2) + _dot_nt(a, b3)


def _kv_head_rows(ref, k):
    n_rows = ref.shape[0]
    return ref.reshape(n_rows * N_KV, HEAD_DIM)[pl.ds(k, n_rows, stride=N_KV), :]


def _rms(x, g):
    return x * lax.rsqrt(jnp.mean(x * x, axis=-1, keepdims=True) + EPS) * g


def _ffn_body(x_ref, ln_ref, wg_ref, wu_ref, wd_ref, lnf_ref, o_ref, xn_ref, *, n_f, final_norm):
    f = pl.program_id(1)

    @pl.when(f == 0)
    def _():
        x = x_ref[...]
        xn_ref[...] = _rms(x, ln_ref[...]).astype(BF16)
        o_ref[...] = 2.0 * x

    xn = xn_ref[...]
    g = _dot(xn, wg_ref[...])
    u = _dot(xn, wu_ref[...])
    h = (g * jax.nn.sigmoid(g) * u).astype(BF16)
    o_ref[...] += _dot(h, wd_ref[...])

    @pl.when(f == n_f - 1)
    def _():
        y = 0.5 * o_ref[...]
        if final_norm:
            y = _rms(y, lnf_ref[...])
        o_ref[...] = y


def _ffn(x, ln, wg, wu, wd, lnf, final_norm, tile0=0, n_tiles=None):
    m = (x.shape[0] // TM if n_tiles is None else n_tiles) * TM
    n_f = D_FF // TF
    return pl.pallas_call(
        functools.partial(_ffn_body, n_f=n_f, final_norm=final_norm),
        grid=(m // TM, n_f),
        in_specs=[
            pl.BlockSpec((TM, D_MODEL), lambda i, f: (tile0 + i, 0), pipeline_mode=pl.Buffered(1)),
            pl.BlockSpec((1, D_MODEL), lambda i, f: (0, 0)),
            pl.BlockSpec((D_MODEL, TF), lambda i, f: (0, f)),
            pl.BlockSpec((D_MODEL, TF), lambda i, f: (0, f)),
            pl.BlockSpec((TF, D_MODEL), lambda i, f: (f, 0)),
            pl.BlockSpec((1, D_MODEL), lambda i, f: (0, 0)),
        ],
        out_specs=pl.BlockSpec((TM, D_MODEL), lambda i, f: (i, 0)),
        out_shape=jax.ShapeDtypeStruct((m, D_MODEL), F32),
        scratch_shapes=[pltpu.VMEM((TM, D_MODEL), BF16)],
        compiler_params=_cparams(2),
        name="ffn",
    )(x, ln.reshape(1, D_MODEL), wg, wu, wd, lnf.reshape(1, D_MODEL))


def _in_proj_body(x_ref, ln_ref, w_ref, o_ref, xn_ref):
    @pl.when(pl.program_id(1) == 0)
    def _():
        xn_ref[...] = _rms(x_ref[...], ln_ref[...]).astype(BF16)

    o_ref[...] = _dot_nt(xn_ref[...], w_ref[...])


def _in_proj(x, ln, w):
    m = x.shape[0]
    return pl.pallas_call(
        _in_proj_body,
        grid=(m // TM, D_Z // TN_IN),
        in_specs=[
            pl.BlockSpec((TM, D_MODEL), lambda i, j: (i, 0), pipeline_mode=pl.Buffered(1)),
            pl.BlockSpec((1, D_MODEL), lambda i, j: (0, 0)),
            pl.BlockSpec((TN_IN, D_MODEL), lambda i, j: (j, 0)),
        ],
        out_specs=pl.BlockSpec((TM, TN_IN), lambda i, j: (i, j)),
        out_shape=jax.ShapeDtypeStruct((m, D_Z), F32),
        scratch_shapes=[pltpu.VMEM((TM, D_MODEL), BF16)],
        compiler_params=_cparams(2),
        name="in_proj",
    )(x, ln.reshape(1, D_MODEL), w)


def _merge_body(gr_ref, oa_ref, wr_ref, wa_ref, zr_ref, za_ref, o_ref):
    y_rnn = _dot(gr_ref[...], wr_ref[...])
    y_attn = _dot(oa_ref[...], wa_ref[...])
    o_ref[...] = (jax.nn.sigmoid(zr_ref[...]) * y_rnn + jax.nn.sigmoid(za_ref[...]) * y_attn).astype(BF16)


def _merge(z, grnn, oattn, w_rnn, w_attn):
    m = z.shape[0]
    cr, ca = C_GRNN // TN_MM, C_GATTN // TN_MM
    return pl.pallas_call(
        _merge_body,
        grid=(m // TM, D_MODEL // TN_MM),
        in_specs=[
            pl.BlockSpec((TM, D_RNN), lambda i, j: (i, 0)),
            pl.BlockSpec((TM, N_HEADS * HEAD_DIM), lambda i, j: (i, 0)),
            pl.BlockSpec((D_RNN, TN_MM), lambda i, j: (0, j)),
            pl.BlockSpec((N_HEADS * HEAD_DIM, TN_MM), lambda i, j: (0, j)),
            pl.BlockSpec((TM, TN_MM), lambda i, j: (i, cr + j)),
            pl.BlockSpec((TM, TN_MM), lambda i, j: (i, ca + j)),
        ],
        out_specs=pl.BlockSpec((TM, TN_MM), lambda i, j: (i, j)),
        out_shape=jax.ShapeDtypeStruct((m, D_MODEL), BF16),
        compiler_params=_cparams(2),
        name="merge",
    )(grnn, oattn, w_rnn, w_attn, z, z)


def _out_proj_body(a_ref, w_ref, x_ref, o_ref):
    o_ref[...] = x_ref[...] + _dot(a_ref[...], w_ref[...])


def _out_proj(x, a, w):
    m = x.shape[0]
    return pl.pallas_call(
        _out_proj_body,
        grid=(m // TM, D_MODEL // TN_MM),
        in_specs=[
            pl.BlockSpec((TM, D_MODEL), lambda i, j: (i, 0)),
            pl.BlockSpec((D_MODEL, TN_MM), lambda i, j: (0, j)),
            pl.BlockSpec((TM, TN_MM), lambda i, j: (i, j)),
        ],
        out_specs=pl.BlockSpec((TM, TN_MM), lambda i, j: (i, j)),
        out_shape=jax.ShapeDtypeStruct((m, D_MODEL), F32),
        compiler_params=_cparams(2),
        name="out_proj",
    )(a, w, x)


def _softplus(v):
    return jnp.maximum(v, 0.0) + jnp.log1p(jnp.exp(-jnp.abs(v)))


def _lru_coeffs(xc, wa_ref, ba, wi_ref, bi, sp, n_blk):
    xb = xc.astype(BF16)
    ra = jnp.concatenate([_dot(xb[:, b * RNN_BW:(b + 1) * RNN_BW], wa_ref[b]) for b in range(n_blk)], axis=1)
    ia = jnp.concatenate([_dot(xb[:, b * RNN_BW:(b + 1) * RNN_BW], wi_ref[b]) for b in range(n_blk)], axis=1)
    r = jax.nn.sigmoid(ra + ba)
    i = jax.nn.sigmoid(ia + bi)
    log_a = -LRU_C * r * sp
    a = jnp.exp(log_a)
    bt = jnp.sqrt(-jnp.tanh(log_a) * (a * a + 1.0)) * (i * xc)
    return a, bt


RNN_TC = 256


def _rnn_prompt_body(ug_ref, ux_ref, cw_ref, cb_ref, wa_ref, ba_ref, wi_ref, bi_ref, lam_ref,
                     g_ref, h_ref, tail_ref, hc_ref):
    c = pl.program_id(1)
    tc = RNN_TC

    @pl.when(c == 0)
    def _():
        tail_ref[...] = jnp.zeros_like(tail_ref)
        hc_ref[...] = jnp.zeros_like(hc_ref)

    u = ux_ref[...]
    tail = tail_ref[...]
    row8 = lax.broadcasted_iota(jnp.int32, (8, D_RNN), 0)
    xc = cb_ref[...] + cw_ref[CONV_W - 1:CONV_W, :] * u
    for j in range(1, CONV_W):
        r = pltpu.roll(u, j, axis=0)
        first = jnp.where(row8 >= j, r[0:8], pltpu.roll(tail, j, axis=0))
        shifted = jnp.concatenate([first, r[8:]], axis=0)
        xc = xc + cw_ref[CONV_W - 1 - j:CONV_W - j, :] * shifted
    tail_ref[...] = u[tc - 8:tc]

    a, bt = _lru_coeffs(xc, wa_ref, ba_ref[...], wi_ref, bi_ref[...], _softplus(-lam_ref[...]), RNN_BLOCKS)

    row = lax.broadcasted_iota(jnp.int32, (tc, D_RNN), 0)
    s = 1
    while s < tc:
        keep = row >= s
        a_sh = jnp.where(keep, pltpu.roll(a, s, axis=0), 1.0)
        b_sh = jnp.where(keep, pltpu.roll(bt, s, axis=0), 0.0)
        bt = a * b_sh + bt
        a = a * a_sh
        s *= 2
    h = bt + a * hc_ref[7:8, :]
    hc_ref[...] = h[tc - 8:tc]
    g_ref[...] = (h * jax.nn.gelu(ug_ref[...])).astype(BF16)

    @pl.when(c == pl.num_programs(1) - 1)
    def _():
        h_ref[0] = h[tc - 8:tc]


def _rnn_prompt(z, cw, cb, wa, ba, wi, bi, lam):
    nc = SEQ // RNN_TC
    vec = lambda: pl.BlockSpec((1, D_RNN), lambda n, c: (0, 0))
    blk = lambda: pl.BlockSpec((RNN_BLOCKS, RNN_BW, RNN_BW), lambda n, c: (0, 0, 0))
    return pl.pallas_call(
        _rnn_prompt_body,
        grid=(BATCH, nc),
        in_specs=[
            pl.BlockSpec((RNN_TC, D_RNN), lambda n, c: (n * nc + c, C_UGATE // D_RNN)),
            pl.BlockSpec((RNN_TC, D_RNN), lambda n, c: (n * nc + c, C_UX // D_RNN)),
            pl.BlockSpec((CONV_W, D_RNN), lambda n, c: (0, 0)),
            vec(), blk(), vec(), blk(), vec(), vec(),
        ],
        out_specs=[
            pl.BlockSpec((RNN_TC, D_RNN), lambda n, c: (n * nc + c, 0)),
            pl.BlockSpec((1, 8, D_RNN), lambda n, c: (n, 0, 0)),
        ],
        out_shape=[jax.ShapeDtypeStruct((M_PROMPT, D_RNN), BF16),
                   jax.ShapeDtypeStruct((BATCH, 8, D_RNN), F32)],
        scratch_shapes=[pltpu.VMEM((8, D_RNN), F32), pltpu.VMEM((8, D_RNN), F32)],
        compiler_params=_cparams(2),
        name="rnn_prompt",
    )(z, z, cw, cb, wa, ba, wi, bi, lam)


RNN_SC = 512


def _rnn_sample_body(ug_ref, ux_ref, buf_ref, h0_ref, cw_ref, cb_ref, wa_ref, ba_ref, wi_ref, bi_ref, lam_ref,
                     g_ref, h_ref):
    full = [buf_ref[j] for j in range(CONV_W - 1)] + [ux_ref[t] for t in range(DEC_SEQ)]
    sp = _softplus(-lam_ref[...])
    h = h0_ref[...]
    for t in range(DEC_SEQ):
        xc = cb_ref[...]
        for k in range(CONV_W):
            xc = xc + full[t + k] * cw_ref[k:k + 1, :]
        a, bt = _lru_coeffs(xc, wa_ref, ba_ref[...], wi_ref, bi_ref[...], sp, RNN_SC // RNN_BW)
        h = a * h + bt
        g_ref[t] = (h * jax.nn.gelu(ug_ref[t])).astype(BF16)
    h_ref[...] = h


def _rnn_sample(ug_t, ux_t, buf_t, h0, cw, cb, wa, ba, wi, bi, lam):
    nb = RNN_SC // RNN_BW
    vec = lambda: pl.BlockSpec((1, RNN_SC), lambda c: (0, c))
    blk = lambda: pl.BlockSpec((nb, RNN_BW, RNN_BW), lambda c: (c, 0, 0))
    return pl.pallas_call(
        _rnn_sample_body,
        grid=(D_RNN // RNN_SC,),
        in_specs=[
            pl.BlockSpec((DEC_SEQ, DEC_BATCH, RNN_SC), lambda c: (0, 0, c)),
            pl.BlockSpec((DEC_SEQ, DEC_BATCH, RNN_SC), lambda c: (0, 0, c)),
            pl.BlockSpec((CONV_W - 1, DEC_BATCH, RNN_SC), lambda c: (0, 0, c)),
            pl.BlockSpec((DEC_BATCH, RNN_SC), lambda c: (0, c)),
            pl.BlockSpec((CONV_W, RNN_SC), lambda c: (0, c)),
            vec(), blk(), vec(), blk(), vec(), vec(),
        ],
        out_specs=[
            pl.BlockSpec((DEC_SEQ, DEC_BATCH, RNN_SC), lambda c: (0, 0, c)),
            pl.BlockSpec((DEC_BATCH, RNN_SC), lambda c: (0, c)),
        ],
        out_shape=[jax.ShapeDtypeStruct((DEC_SEQ, DEC_BATCH, D_RNN), BF16),
                   jax.ShapeDtypeStruct((DEC_BATCH, D_RNN), F32)],
        compiler_params=_cparams(1),
        name="rnn_sample",
    )(ug_t, ux_t, buf_t, h0, cw, cb, wa, ba, wi, bi, lam)


N_SEG = PAGE_SIZE // CMP_STRIDE
SEG_ROWS = N_PAGES * N_SEG
N_PAIR = CMP_STRIDE // 2


CMP_PG = 4
SLAB_PITCH = 24


def _compress_body(pt_ref, *refs, paged):
    n_in = CMP_PG * (2 if paged else 1)
    page_refs = refs[:n_in]
    w1_ref, pos_ref, b1_ref, w2_ref, ck_ref, cv_ref, stage_ref, pterm_ref, slab_ref = refs[n_in:]
    n = pl.program_id(0)
    q = pl.program_id(1)

    @pl.when((n == 0) & (q == 0))
    def _():
        for kind in range(2):
            acc = jnp.zeros((8, 2 * CMP_HID), F32)
            for pr in range(N_PAIR):
                acc = acc + _dot(pos_ref[:, pr * 256:(pr + 1) * 256].astype(BF16), w1_ref[kind, pr])
            pterm_ref[kind] = acc

    for j in range(CMP_PG):
        for kind in range(2):
            for k in range(N_KV):
                if paged:
                    slab = _kv_head_rows(page_refs[j * 2 + kind], k)
                else:
                    kk = kind * N_KV + k
                    slab = page_refs[j][:, kk * HEAD_DIM:(kk + 1) * HEAD_DIM]
                for s in range(N_SEG):
                    slab_ref[(j * 2 + kind) * N_KV + k, pl.ds(s * SLAB_PITCH, CMP_STRIDE), :] = (
                        slab[s * CMP_STRIDE:(s + 1) * CMP_STRIDE])

    for j in range(CMP_PG):
        seg0 = (q * CMP_PG + j) * N_SEG
        for kind in range(2):
            for k in range(N_KV):
                for l in range(CMP_STRIDE):
                    piece = slab_ref[(j * 2 + kind) * N_KV + k, pl.ds(l, N_SEG, stride=SLAB_PITCH), :]
                    stage_ref[kind, l // 2, pl.ds(k * SEG_ROWS + seg0, N_SEG),
                              pl.ds((l % 2) * HEAD_DIM, HEAD_DIM)] = piece

    @pl.when(q == N_PAGES // CMP_PG - 1)
    def _():
        for kind, out_ref in ((0, ck_ref), (1, cv_ref)):
            acc = jnp.zeros((N_KV * SEG_ROWS, 2 * CMP_HID), F32)
            for pr in range(N_PAIR):
                acc = acc + _dot(stage_ref[kind, pr].astype(BF16), w1_ref[kind, pr])
            nxt = pltpu.roll(acc[:, CMP_HID:], N_KV * SEG_ROWS - 1, axis=0)
            pt = pterm_ref[kind]
            posterm = pt[0:1, :CMP_HID] + pt[1:2, CMP_HID:] + b1_ref[kind]
            hid = acc[:, :CMP_HID] + nxt + posterm
            out_ref[0] = _dot(jax.nn.gelu(hid).astype(BF16), w2_ref[kind]).astype(BF16)


def _compress(src, pt_flat, col_blk, n_seq, w1, pos, b1, w2, name):
    paged = col_blk is None
    page_of = lambda n, p, pt, j: pt[n * N_PAGES + p * CMP_PG + j]
    out_spec = pl.BlockSpec((1, N_KV * SEG_ROWS, HEAD_DIM), lambda n, p, pt: (n, 0, 0))
    if paged:
        page_specs = [pl.BlockSpec((None, None, PAGE_SIZE, None, N_KV, HEAD_DIM),
                                   lambda n, p, pt, j=j, kind=kind: (0, page_of(n, p, pt, j), 0, kind, 0, 0))
                      for j in range(CMP_PG) for kind in range(2)]
    else:
        page_specs = [pl.BlockSpec((None, PAGE_SIZE, 2 * N_KV * HEAD_DIM),
                                   lambda n, p, pt, j=j: (page_of(n, p, pt, j), 0, col_blk))
                      for j in range(CMP_PG)]
    grid_spec = pltpu.PrefetchScalarGridSpec(
        num_scalar_prefetch=1,
        grid=(n_seq, N_PAGES // CMP_PG),
        in_specs=page_specs + [
            pl.BlockSpec((2, N_PAIR, 256, 2 * CMP_HID), lambda n, p, pt: (0, 0, 0, 0)),
            pl.BlockSpec((8, CMP_STRIDE * HEAD_DIM), lambda n, p, pt: (0, 0)),
            pl.BlockSpec((2, 1, CMP_HID), lambda n, p, pt: (0, 0, 0)),
            pl.BlockSpec((2, CMP_HID, HEAD_DIM), lambda n, p, pt: (0, 0, 0)),
        ],
        out_specs=[out_spec, out_spec],
        scratch_shapes=[pltpu.VMEM((2, N_PAIR, N_KV * SEG_ROWS, 256), F32),
                        pltpu.VMEM((2, 8, 2 * CMP_HID), F32),
                        pltpu.VMEM((CMP_PG * 2 * N_KV, N_SEG * SLAB_PITCH, HEAD_DIM), F32)],
    )
    shp = jax.ShapeDtypeStruct((n_seq, N_KV * SEG_ROWS, HEAD_DIM), BF16)
    return pl.pallas_call(
        functools.partial(_compress_body, paged=paged),
        grid_spec=grid_spec,
        out_shape=[shp, shp],
        compiler_params=_cparams(2),
        name=name,
    )(pt_flat, *([src] * len(page_specs)), w1, pos, b1, w2)


def _select_blocks(score, cur, n_blk):
    jj = lax.broadcasted_iota(jnp.int32, score.shape, 1)
    forced = (jj == 0) | (jj == cur) | (jj == cur - 1)
    sc = jnp.where(forced, BIG, jnp.where(jj <= cur, score, NEG))
    rank = jnp.zeros(score.shape, F32)
    for i in range(n_blk):
        si = sc[:, i:i + 1]
        beats = (si > sc) | ((si == sc) & (jj > i))
        rank = rank + jnp.where(beats, 1.0, 0.0)
    sel = (rank < float(min(N_SEL, n_blk))) & (jj <= cur) & (jj < n_blk)
    return jnp.where(sel, 1.0, 0.0).astype(BF16)


def _select_blocks_t(score, cur):
    n_blk = score.shape[0]
    jj = lax.broadcasted_iota(jnp.int32, score.shape, 0)
    forced = (jj == 0) | (jj == cur) | (jj == cur - 1)
    sc = jnp.where(forced, BIG, jnp.where(jj <= cur, score, NEG))
    rank = jnp.zeros(score.shape, F32)
    for i in range(n_blk):
        si = sc[i:i + 1, :]
        beats = (si > sc) | ((si == sc) & (jj > i))
        rank = rank + jnp.where(beats, 1.0, 0.0)
    sel = (rank < float(min(N_SEL, n_blk))) & (jj <= cur)
    return jnp.where(sel, 1.0, 0.0)


def _softmax_rows(logits):
    m = jnp.maximum(jnp.max(logits, axis=-1, keepdims=True), M_FLOOR)
    e = jnp.exp(logits - m)
    s = jnp.sum(e, axis=-1, keepdims=True)
    return e / jnp.where(s > 0.0, s, 1.0)


MASKED_TILE = WINDOW // 128 + 1
ATT_HP = 2


def _attn_prompt_body(zq_ref, zg_ref, ck_ref, cv_ref, ks_ref, vs_ref, kw_ref, vw_ref,
                      bc_ref, tz_ref, ovt_ref, e_ref, o_ref,
                      ksb, vsb, kwb, vwb, selm_ref, s_ref, mel_ref, lel_ref, acc_ref):
    qt = pl.program_id(2)
    rows = GROUP * 128

    @pl.when(qt == 0)
    def _():
        ksb[...] = ks_ref[...].astype(BF16)
        vsb[...] = vs_ref[...].astype(BF16)
        kwb[...] = kw_ref[...].astype(BF16)
        vwb[...] = vw_ref[...].astype(BF16)

    heads = range(ATT_HP)
    lanes = lambda h: slice(h * HEAD_DIM, (h + 1) * HEAD_DIM)
    groups = lambda h: slice(h * GROUP, (h + 1) * GROUP)
    n_blk = SEQ // SEL_BLK
    cur = jnp.right_shift(qt * 128 + lax.broadcasted_iota(jnp.int32, (n_blk, 128), 1), 6)
    qq, o_c = [], []
    for h in heads:
        q = zq_ref[:, h * GROUP * HEAD_DIM:(h + 1) * GROUP * HEAD_DIM] * Q_SCALE
        qh = jnp.concatenate([q[:, g * HEAD_DIM:(g + 1) * HEAD_DIM] for g in range(GROUP)], axis=0).astype(BF16)
        qq.append(qh)
        pc = _softmax_rows(_dot_nt(qh, ck_ref[h]) + bc_ref[0, groups(h)].reshape(rows, 128))
        o_c.append(_dot(pc.astype(BF16), cv_ref[h]))
        ps = pc[0:128] + pc[128:256] + pc[256:384] + pc[384:512]
        sel_t = _select_blocks_t(_dot_nt_split3(ovt_ref[...], ps), cur)
        sel = jnp.concatenate([sel_t, jnp.zeros((128 - n_blk, 128), F32)], axis=0).T.astype(BF16)
        for j in range(SEQ // 256):
            selm_ref[h, j] = (_dot(sel, e_ref[j]) - 1.0) * BIG

    def attend(k_ref, v_ref, lo, tile_of_delta, use_sel):
        hi = jnp.right_shift(qt, 1) + 1
        mel_ref[...] = jnp.full(mel_ref.shape, M_FLOOR, F32)

        def logits_pass(j, carry):
            off = pl.multiple_of(j * 256, 256)
            d0 = qt - 2 * j
            i0, i1 = tile_of_delta(d0), tile_of_delta(d0 - 1)
            for h in heads:
                s = _dot_nt(qq[h], k_ref[pl.ds(off, 256), lanes(h)]).reshape(GROUP, 128, 256)
                s = s + jnp.concatenate([tz_ref[i0, groups(h)], tz_ref[i1, groups(h)]], axis=-1)
                if use_sel:
                    s = s + selm_ref[h, j][None]
                s = s.reshape(rows, 256)
                s_ref[h, j] = s
                mel_ref[h] = jnp.maximum(mel_ref[h], jnp.maximum(s[:, :128], s[:, 128:]))
            return carry

        lax.fori_loop(lo, hi, logits_pass, 0)
        m = [jnp.max(mel_ref[h], axis=-1, keepdims=True) for h in heads]
        lel_ref[...] = jnp.zeros_like(lel_ref)
        acc_ref[...] = jnp.zeros_like(acc_ref)

        def value_pass(j, carry):
            off = pl.multiple_of(j * 256, 256)
            for h in heads:
                pe = jnp.exp(s_ref[h, j] - m[h])
                lel_ref[h] += pe[:, :128] + pe[:, 128:]
                acc_ref[h] += _dot(pe.astype(BF16), v_ref[pl.ds(off, 256), lanes(h)])
            return carry

        lax.fori_loop(lo, hi, value_pass, 0)
        outs = []
        for h in heads:
            l = jnp.sum(lel_ref[h], axis=-1, keepdims=True)
            outs.append(acc_ref[h] / jnp.where(l > 0.0, l, 1.0))
        return outs

    n_win = WINDOW // 128
    o_s = attend(ksb, vsb, 0, lambda d: jnp.where(d < 0, MASKED_TILE, jnp.minimum(d, 2)), True)
    o_w = attend(kwb, vwb, jnp.right_shift(jnp.maximum(qt - n_win, 0), 1),
                 lambda d: jnp.where((d < 0) | (d > n_win), MASKED_TILE, d), False)

    gates = jax.nn.sigmoid(zg_ref[...])
    outs = []
    for h in heads:
        for g in range(GROUP):
            r = slice(g * 128, (g + 1) * 128)
            gate = lambda branch: gates[:, h * 128 + branch * GROUP + g:h * 128 + branch * GROUP + g + 1]
            outs.append(gate(0) * o_c[h][r] + gate(1) * o_s[h][r] + gate(2) * o_w[h][r])
    o_ref[...] = jnp.concatenate(outs, axis=1).astype(BF16)


def _attn_prompt(z, ck, cv, bias_cmp, tz, ov, emat):
    nq = SEQ // 128
    nhp = N_KV // ATT_HP
    kw = ATT_HP * HEAD_DIM
    qw = ATT_HP * GROUP * HEAD_DIM
    kv_col = lambda base, kind: (lambda n, k, t: (n, (base + kind * N_KV * HEAD_DIM) // kw + k))
    kvspec = lambda base, kind: pl.BlockSpec((SEQ, kw), kv_col(base, kind))
    rows = GROUP * 128
    return pl.pallas_call(
        _attn_prompt_body,
        grid=(BATCH, nhp, nq),
        in_specs=[
            pl.BlockSpec((128, qw), lambda n, k, t: (n * nq + t, C_Q // qw + k)),
            pl.BlockSpec((128, ATT_HP * 128), lambda n, k, t: (n * nq + t, C_GNSA // (ATT_HP * 128) + k)),
            pl.BlockSpec((ATT_HP, SEG_ROWS, HEAD_DIM), lambda n, k, t: (n * nhp + k, 0, 0)),
            pl.BlockSpec((ATT_HP, SEG_ROWS, HEAD_DIM), lambda n, k, t: (n * nhp + k, 0, 0)),
            kvspec(C_PAG, 2), kvspec(C_PAG, 3), kvspec(C_WIN, 0), kvspec(C_WIN, 1),
            pl.BlockSpec((1, ATT_HP * GROUP, 128, 128), lambda n, k, t: (t, k, 0, 0)),
            pl.BlockSpec((MASKED_TILE + 1, ATT_HP * GROUP, 128, 128), lambda n, k, t: (0, k, 0, 0)),
            pl.BlockSpec((SEQ // SEL_BLK, 128), lambda n, k, t: (0, 0)),
            pl.BlockSpec((SEQ // 256, 128, 256), lambda n, k, t: (0, 0, 0)),
        ],
        out_specs=pl.BlockSpec((128, qw), lambda n, k, t: (n * nq + t, k)),
        out_shape=jax.ShapeDtypeStruct((M_PROMPT, N_HEADS * HEAD_DIM), BF16),
        scratch_shapes=[pltpu.VMEM((SEQ, kw), BF16)] * 4 + [
            pltpu.VMEM((ATT_HP, SEQ // 256, 128, 256), F32),
            pltpu.VMEM((ATT_HP, SEQ // 256, rows, 256), F32),
            pltpu.VMEM((ATT_HP, rows, 128), F32), pltpu.VMEM((ATT_HP, rows, 128), F32),
            pltpu.VMEM((ATT_HP, rows, HEAD_DIM), F32)],
        compiler_params=_cparams(3),
        name="attn_prompt",
    )(z, z, ck, cv, z, z, z, z, bias_cmp, tz, ov, emat)


S_ROWS = GROUP * N_KV * DEC_SEQ


ATT_SS = 2


def _attn_sample_body(pt_ref, q_ref, gs_ref, ck_ref, cv_ref, *rest):
    n_pg = ATT_SS * 2 * N_PAGES
    page_refs = rest[:n_pg]
    win_refs = rest[n_pg:n_pg + 2 * ATT_SS]
    (nkv_ref, nwin_ref, bcmp_ref, bsel_ref, bnew_ref, bwin_ref, ov_ref, e_ref,
     o_ref, s_ref, nk_ref, nw_ref) = rest[n_pg + 2 * ATT_SS:]
    kv_of_row = jnp.bitwise_and(jnp.right_shift(lax.broadcasted_iota(jnp.int32, (S_ROWS, 1), 0), 2), N_KV - 1)
    col = lambda k, half: pl.ds(half * N_KV * HEAD_DIM + k * HEAD_DIM, HEAD_DIM)
    head_rows = _kv_head_rows
    rowmax = lambda s: jnp.max(s, axis=-1, keepdims=True)
    rowsum = lambda s: jnp.sum(s, axis=-1, keepdims=True)

    @pl.when(pl.program_id(0) == 0)
    def _():
        nk_ref[...] = jnp.zeros_like(nk_ref)
        nw_ref[...] = jnp.zeros_like(nw_ref)

    def one_sequence(i):
        pages = page_refs[i * 2 * N_PAGES:(i + 1) * 2 * N_PAGES]
        kwin_ref, vwin_ref = win_refs[2 * i], win_refs[2 * i + 1]
        qq = (q_ref[i] * Q_SCALE).astype(BF16)

        def logits(get_k):
            out = None
            for k in range(N_KV):
                s = jnp.where(kv_of_row == k, _dot_nt(qq, get_k(k).astype(BF16)), 0.0)
                out = s if out is None else out + s
            return out

        def weighted(pe, get_v):
            out = None
            for k in range(N_KV):
                o = _dot(jnp.where(kv_of_row == k, pe, 0.0).astype(BF16), get_v(k).astype(BF16))
                out = o if out is None else out + o
            return out

        nk_ref[i, 0:8, :] = nkv_ref[i]
        nw_ref[i, 0:8, :] = nwin_ref[i]

        pc = _softmax_rows(logits(lambda k: ck_ref[i, pl.ds(k * SEG_ROWS, SEG_ROWS), :]) + bcmp_ref[...])
        o_c = weighted(pc, lambda k: cv_ref[i, pl.ds(k * SEG_ROWS, SEG_ROWS), :])
        ps = pc + pltpu.roll(pc, 16, axis=0) + pltpu.roll(pc, 32, axis=0) + pltpu.roll(pc, 48, axis=0)
        score = _dot_split3(ps, ov_ref[...])
        n_blk = -(-(PAST_LEN + DEC_SEQ) // SEL_BLK)
        cur = jnp.full((S_ROWS, 128), PAST_LEN // SEL_BLK, jnp.int32)
        sel = _select_blocks(score, cur, n_blk)
        key_mask = (_dot(sel, e_ref[...]) - 1.0) * BIG

        m = jnp.full((S_ROWS, 1), M_FLOOR, F32)
        for p in range(N_PAGES):
            s = (logits(lambda k: head_rows(pages[2 * p], k)) + bsel_ref[p]
                 + key_mask[:, p * PAGE_SIZE:(p + 1) * PAGE_SIZE])
            s_ref[i, p] = s
            m = jnp.maximum(m, rowmax(s))
        sn = logits(lambda k: nk_ref[i, :, col(k, 0)]) + bnew_ref[...]
        m = jnp.maximum(m, rowmax(sn))
        pn = jnp.exp(sn - m)
        l = rowsum(pn)
        acc = weighted(pn, lambda k: nk_ref[i, :, col(k, 1)])
        for p in range(N_PAGES):
            pe = jnp.exp(s_ref[i, p] - m)
            l = l + rowsum(pe)
            acc = acc + weighted(pe, lambda k: head_rows(pages[2 * p + 1], k))
        o_s = acc / jnp.where(l > 0.0, l, 1.0)

        sw = logits(lambda k: head_rows(kwin_ref, k)) + bwin_ref[...]
        sn = logits(lambda k: nw_ref[i, :, col(k, 0)]) + bnew_ref[...]
        m = jnp.maximum(jnp.maximum(rowmax(sw), rowmax(sn)), M_FLOOR)
        pw = jnp.exp(sw - m)
        pn = jnp.exp(sn - m)
        l = rowsum(pw) + rowsum(pn)
        o_w = weighted(pw, lambda k: head_rows(vwin_ref, k)) + weighted(pn, lambda k: nw_ref[i, :, col(k, 1)])
        o_w = o_w / jnp.where(l > 0.0, l, 1.0)

        gates = jax.nn.sigmoid(gs_ref[i])
        o_ref[i] = gates[:, 0:1] * o_c + gates[:, 1:2] * o_s + gates[:, 2:3] * o_w

    for i in range(ATT_SS):
        one_sequence(i)


def _attn_sample(pt_flat, q_s, g_s, ck, cv, cache_kv, nkv, state_win, nwin, bcmp, bsel, bnew, bwin, ov, emat):
    const2 = lambda shape: pl.BlockSpec(shape, lambda n, pt: (0, 0))
    per_seq = lambda rows, width: pl.BlockSpec((ATT_SS, rows, width), lambda n, pt: (n, 0, 0))
    page_specs = [pl.BlockSpec((None, None, PAGE_SIZE, None, N_KV, HEAD_DIM),
                               lambda n, pt, i=i, p=p, kind=kind:
                               (0, pt[(n * ATT_SS + i) * N_PAGES + p], 0, kind, 0, 0))
                  for i in range(ATT_SS) for p in range(N_PAGES) for kind in (2, 3)]
    win_specs = [pl.BlockSpec((None, None, WINDOW, None, N_KV, HEAD_DIM),
                              lambda n, pt, i=i, kind=kind: (0, n * ATT_SS + i, 0, kind, 0, 0))
                 for i in range(ATT_SS) for kind in (0, 1)]
    grid_spec = pltpu.PrefetchScalarGridSpec(
        num_scalar_prefetch=1,
        grid=(DEC_BATCH // ATT_SS,),
        in_specs=[
            per_seq(S_ROWS, HEAD_DIM), per_seq(S_ROWS, 128),
            per_seq(N_KV * SEG_ROWS, HEAD_DIM), per_seq(N_KV * SEG_ROWS, HEAD_DIM),
        ] + page_specs + win_specs + [
            per_seq(8, 1024), per_seq(8, 1024),
            const2((S_ROWS, 128)),
            pl.BlockSpec((N_PAGES, S_ROWS, 128), lambda n, pt: (0, 0, 0)),
            const2((S_ROWS, 128)),
            const2((S_ROWS, WINDOW)),
            const2((128, 128)),
            const2((128, PAST_LEN)),
        ],
        out_specs=per_seq(S_ROWS, HEAD_DIM),
        scratch_shapes=[
            pltpu.VMEM((ATT_SS, N_PAGES, S_ROWS, 128), F32),
            pltpu.VMEM((ATT_SS, 128, 1024), F32), pltpu.VMEM((ATT_SS, 128, 1024), F32),
        ],
    )
    return pl.pallas_call(
        _attn_sample_body,
        grid_spec=grid_spec,
        out_shape=jax.ShapeDtypeStruct((DEC_BATCH, S_ROWS, HEAD_DIM), F32),
        compiler_params=_cparams(1),
        name="attn_sample",
    )(pt_flat, q_s, g_s, ck, cv, *([cache_kv] * len(page_specs)), *([state_win] * len(win_specs)), nkv, nwin,
      bcmp, bsel, bnew, bwin, ov, emat)


def _t5_bucket(dist):
    d = jnp.maximum(dist, 0)
    df = jnp.maximum(d, 1).astype(F32)
    large = MAX_EXACT + (jnp.log(df / MAX_EXACT) / math.log(MAX_DIST / MAX_EXACT)
                         * (N_BUCKETS - MAX_EXACT)).astype(jnp.int32)
    large = jnp.minimum(large, N_BUCKETS - 1)
    return jnp.where(d < MAX_EXACT, d, large)


def _bias_lookup_body(rb_ref, idx_ref, o_ref):
    idx = idx_ref[0]
    for h in range(N_HEADS):
        acc = jnp.full(idx.shape, NEG, F32)
        for b in range(N_BUCKETS):
            acc = jnp.where(idx == b, rb_ref[b * N_HEADS + h], acc)
        o_ref[0, h] = acc


def _bias_table(rel_bias, dist, valid, name):
    p, r, _ = dist.shape
    idx = jnp.where(jnp.asarray(valid), _t5_bucket(jnp.asarray(dist, jnp.int32)), -1)
    return pl.pallas_call(
        _bias_lookup_body,
        grid=(p,),
        in_specs=[pl.BlockSpec(memory_space=pltpu.SMEM), pl.BlockSpec((1, r, 128), lambda i: (i, 0, 0))],
        out_specs=pl.BlockSpec((1, N_HEADS, r, 128), lambda i: (i, 0, 0, 0)),
        out_shape=jax.ShapeDtypeStruct((p, N_HEADS, r, 128), F32),
        compiler_params=_cparams(1),
        name=name,
    )(rel_bias.astype(F32).reshape(-1), idx)


def _overlap(nc, nb):
    cs = np.arange(nc)[:, None] * CMP_STRIDE
    js = np.arange(nb)[None, :] * SEL_BLK
    ov = np.clip(np.minimum(cs + CMP_LEN, js + SEL_BLK) - np.maximum(cs, js), 0, None) / CMP_LEN
    out = np.zeros((128, 128), np.float32)
    out[:nc, :nb] = ov
    return jnp.asarray(out, BF16)


def _position_tables(rel_bias):
    nc = SEG_ROWS - 1
    t = np.arange(128)[None, :, None]
    c = np.arange(128)[None, None, :]
    cend = c * CMP_STRIDE + CMP_LEN - 1
    d = np.arange(MASKED_TILE + 1)[:, None, None] * 128 + t - c
    tz = _bias_table(rel_bias, d, (d >= 0) & (d < WINDOW), "bias_tiles")
    d = np.arange(SEQ // 128)[:, None, None] * 128 + t - cend
    bias_cmp = _bias_table(rel_bias, d, (d >= 0) & (c < nc), "bias_cmp")
    ts = np.arange(8)[None, :, None]
    qpos = PAST_LEN + ts
    live = ts < DEC_SEQ
    d_cmp = qpos - cend
    d_sel = qpos - (np.arange(N_PAGES)[:, None, None] * PAGE_SIZE + c)
    d_new = ts - c
    d_win = qpos - (PAST_LEN - WINDOW + np.arange(WINDOW // 128)[:, None, None] * 128 + c)
    d = np.concatenate([d_cmp, d_sel, d_new, d_win], axis=0)
    valid = np.concatenate([(d_cmp >= 0) & (c < nc), d_sel >= 0, (d_new >= 0) & (c < DEC_SEQ),
                            (d_win >= 0) & (d_win < WINDOW)], axis=0) & live
    o = _bias_table(rel_bias, d, valid, "bias_sample")[:, :, :DEC_SEQ]
    o = jnp.transpose(o.reshape(-1, N_KV, GROUP, DEC_SEQ, 128), (0, 2, 1, 3, 4)).reshape(-1, S_ROWS, 128)
    bcmp, bsel, bnew = o[0], o[1:1 + N_PAGES], o[1 + N_PAGES]
    bwin = jnp.transpose(o[2 + N_PAGES:], (1, 0, 2)).reshape(S_ROWS, WINDOW)
    keys = np.arange(SEQ)
    e_all = (np.arange(128)[:, None] == (keys // SEL_BLK)[None, :]).astype(np.float32)
    e_tiles = jnp.asarray(e_all.reshape(128, SEQ // 256, 256).transpose(1, 0, 2), BF16)
    ovt_p = jnp.transpose(_overlap(nc, SEQ // SEL_BLK))[:SEQ // SEL_BLK]
    return dict(tz=tz, bias_cmp=bias_cmp, bcmp=bcmp, bsel=bsel, bnew=bnew, bwin=bwin,
                ovt_p=ovt_p, ov_s=_overlap(nc, -(-(PAST_LEN + DEC_SEQ) // SEL_BLK)),
                e_tiles=e_tiles, e_all=jnp.asarray(e_all, BF16))


W_TR = 512
W_TC = 2048
N_GNSA = 3 * N_HEADS


def _permute_w_in_body(a_ref, b_ref, g_ref, o_ref):
    j = pl.program_id(0)
    first, last = C_GRNN // W_TR, C_GNSA // W_TR

    @pl.when(j < first)
    def _():
        o_ref[...] = a_ref[...].astype(BF16)

    @pl.when((j >= first) & (j < last))
    def _():
        o_ref[0:W_TR - N_GNSA, :] = a_ref[N_GNSA:W_TR, :].astype(BF16)
        o_ref[W_TR - N_GNSA:W_TR, :] = b_ref[0:N_GNSA, :].astype(BF16)

    @pl.when(j == last)
    def _():
        o_ref[...] = g_ref[...]


def _permute_w_in(w_in):
    w_t = jnp.swapaxes(w_in, 1, 2)
    g_nsa = w_t[0, C_GRNN:C_GRNN + N_GNSA].reshape(N_KV, GROUP, 3, D_MODEL)
    g_nsa = jnp.transpose(g_nsa, (0, 2, 1, 3)).reshape(N_KV, 3 * GROUP, D_MODEL)
    g_nsa = jnp.pad(g_nsa, ((0, 0), (0, 128 - 3 * GROUP), (0, 0))).reshape(N_KV * 128, D_MODEL).astype(BF16)
    first, last = C_GRNN // W_TR, C_GNSA // W_TR
    return pl.pallas_call(
        _permute_w_in_body,
        grid=(D_Z // W_TR, D_MODEL // W_TC),
        in_specs=[
            pl.BlockSpec((None, W_TR, W_TC), lambda j, c: (0, jnp.minimum(j, last - 1), c)),
            pl.BlockSpec((None, W_TR, W_TC), lambda j, c: (0, jnp.clip(j + 1, first + 1, last), c)),
            pl.BlockSpec((W_TR, W_TC), lambda j, c: (0, c)),
        ],
        out_specs=pl.BlockSpec((W_TR, W_TC), lambda j, c: (j, c)),
        out_shape=jax.ShapeDtypeStruct((D_Z, D_MODEL), BF16),
        compiler_params=_cparams(2),
        name="w_in_layout",
    )(w_t, w_t, g_nsa)


def _cmp_weights(w1_k, w1_v, b1_k, b1_v, w2_k, w2_v, pos):
    def cat(w1):
        w = w1.reshape(2, CMP_STRIDE * HEAD_DIM, CMP_HID)
        return jnp.concatenate([w[0], w[1]], axis=1).reshape(N_PAIR, 256, 2 * CMP_HID)
    w1 = jnp.stack([cat(w1_k), cat(w1_v)]).astype(BF16)
    posm = jnp.pad(pos.reshape(2, CMP_STRIDE * HEAD_DIM), ((0, 6), (0, 0)))
    b1 = jnp.stack([b1_k, b1_v]).reshape(2, 1, CMP_HID)
    w2 = jnp.stack([w2_k, w2_v]).astype(BF16)
    return w1, posm, b1, w2


def kernel(x_prompt, x_sample, cache_kv, page_table, state_win, state_conv, state_h, rel_bias, ln_final, ln_ffn1, w_ffn1_gate, w_ffn1_up, w_ffn1_down, ln_mix, w_in, conv_w, conv_b, rg_wa, rg_ba, rg_wi, rg_bi, rg_lambda, cmp_pos, cmp_k_w1, cmp_k_b1, cmp_k_w2, cmp_v_w1, cmp_v_b1, cmp_v_w2, w_br_rnn, w_br_attn, w_out, ln_ffn2, w_ffn2_gate, w_ffn2_up, w_ffn2_down):
    tabs = _position_tables(rel_bias)
    x = jnp.concatenate([x_prompt.reshape(M_PROMPT, D_MODEL), x_sample.reshape(M_SAMPLE, D_MODEL)], axis=0)

    x = _ffn(x, ln_ffn1[0], w_ffn1_gate[0].astype(BF16), w_ffn1_up[0].astype(BF16), w_ffn1_down[0].astype(BF16),
             ln_final, False)
    z = _in_proj(x, ln_mix[0], _permute_w_in(w_in))
    z_s = z[M_PROMPT:]

    vec = lambda v: v.reshape(1, D_RNN)
    rnn_w = (conv_w[0], vec(conv_b[0]), rg_wa[0].astype(BF16), vec(rg_ba[0]), rg_wi[0].astype(BF16),
             vec(rg_bi[0]), vec(rg_lambda[0]))
    g_p, h_p = _rnn_prompt(z, *rnn_w)
    tmajor = lambda a: jnp.transpose(a.reshape(DEC_BATCH, -1, D_RNN), (1, 0, 2))
    g_s, h_s = _rnn_sample(tmajor(z_s[:, C_UGATE:C_UGATE + D_RNN]), tmajor(z_s[:, C_UX:C_UX + D_RNN]),
                           tmajor(state_conv[0]), state_h[0], *rnn_w)
    grnn = jnp.concatenate([g_p, jnp.transpose(g_s, (1, 0, 2)).reshape(M_SAMPLE, D_RNN)], axis=0)

    cw = _cmp_weights(cmp_k_w1[0], cmp_v_w1[0], cmp_k_b1[0], cmp_v_b1[0], cmp_k_w2[0], cmp_v_w2[0], cmp_pos[0])
    pt_prompt = jnp.arange(BATCH * N_PAGES, dtype=jnp.int32)
    pt_sample = page_table.reshape(-1).astype(jnp.int32)
    ck_p, cv_p = _compress(z.reshape(M_TOK // PAGE_SIZE, PAGE_SIZE, D_Z), pt_prompt, C_PAG // 1024, BATCH, *cw,
                           name="compress_prompt")
    ck_s, cv_s = _compress(cache_kv, pt_sample, None, DEC_BATCH, *cw, name="compress_sample")
    o_p = _attn_prompt(z, ck_p.reshape(BATCH * N_KV, SEG_ROWS, HEAD_DIM), cv_p.reshape(BATCH * N_KV, SEG_ROWS, HEAD_DIM),
                       tabs["bias_cmp"], tabs["tz"], tabs["ovt_p"], tabs["e_tiles"])

    def rows_gkt(a, width):
        a = a.reshape(DEC_BATCH, DEC_SEQ, N_KV, GROUP, width)
        return jnp.transpose(a, (0, 3, 2, 1, 4)).reshape(DEC_BATCH, S_ROWS, width)

    q_s = rows_gkt(z_s[:, C_Q:C_Q + N_HEADS * HEAD_DIM], HEAD_DIM)
    gn = z_s[:, C_GNSA:].reshape(M_SAMPLE, N_KV, 128)[:, :, :3 * GROUP].reshape(M_SAMPLE, N_KV, 3, GROUP)
    g_s3 = jnp.pad(rows_gkt(jnp.transpose(gn, (0, 1, 3, 2)), 3), ((0, 0), (0, 0), (0, 125)))
    pad8 = lambda a: jnp.pad(a.reshape(DEC_BATCH, DEC_SEQ, -1), ((0, 0), (0, 8 - DEC_SEQ), (0, 0)))
    nkv = pad8(z_s[:, C_PAG + 2 * N_KV * HEAD_DIM:C_PAG + 4 * N_KV * HEAD_DIM])
    nwin = pad8(z_s[:, C_WIN:C_WIN + 2 * N_KV * HEAD_DIM])
    o_s = _attn_sample(pt_sample, q_s, g_s3, ck_s, cv_s, cache_kv, nkv, state_win, nwin,
                       tabs["bcmp"], tabs["bsel"], tabs["bnew"], tabs["bwin"], tabs["ov_s"], tabs["e_all"])
    o_s = jnp.transpose(o_s.reshape(DEC_BATCH, GROUP, N_KV, DEC_SEQ, HEAD_DIM), (0, 3, 2, 1, 4))
    oattn = jnp.concatenate([o_p, o_s.reshape(M_SAMPLE, N_HEADS * HEAD_DIM).astype(BF16)], axis=0)

    merged = _merge(z, grnn, oattn, w_br_rnn[0].astype(BF16), w_br_attn[0].astype(BF16))
    x = _out_proj(x, merged, w_out[0].astype(BF16))
    ffn2 = functools.partial(_ffn, x, ln_ffn2[0], w_ffn2_gate[0].astype(BF16), w_ffn2_up[0].astype(BF16),
                             w_ffn2_down[0].astype(BF16), ln_final, True)
    y_p = ffn2(tile0=0, n_tiles=M_PROMPT // TM)
    y_s = ffn2(tile0=M_PROMPT // TM, n_tiles=M_SAMPLE // TM)

    kv = z[:, C_PAG:C_PAG + 4 * N_KV * HEAD_DIM]
    wn = z[:, C_WIN:C_WIN + 2 * N_KV * HEAD_DIM]
    keep = CONV_W - 1
    conv_p = jnp.stack([lax.slice(z, ((n + 1) * SEQ - keep, C_UX), ((n + 1) * SEQ, C_UX + D_RNN))
                        for n in range(BATCH)])
    conv_s = z_s[:, C_UX:C_UX + D_RNN].reshape(DEC_BATCH, DEC_SEQ, D_RNN)[:, DEC_SEQ - keep:]
    win_p = wn[:M_PROMPT].reshape(BATCH, SEQ, 2, N_KV, HEAD_DIM)[:, SEQ - WINDOW:]
    win_s = jnp.concatenate([state_win.reshape(DEC_BATCH, WINDOW, 2, N_KV, HEAD_DIM),
                             wn[M_PROMPT:].reshape(DEC_BATCH, DEC_SEQ, 2, N_KV, HEAD_DIM)], axis=1)
    return (
        y_p.reshape(BATCH, SEQ, D_MODEL),
        y_s.reshape(DEC_BATCH, DEC_SEQ, D_MODEL),
        kv[:M_PROMPT].reshape(1, BATCH, SEQ, 4, N_KV, HEAD_DIM),
        kv[M_PROMPT:].reshape(1, DEC_BATCH, DEC_SEQ, 4, N_KV, HEAD_DIM),
        win_p[None],
        win_s[None, :, DEC_SEQ:],
        conv_p[None],
        conv_s[None],
        h_p[None, :, 7],
        h_s[None],
    )
```

```python
import functools
import math

import numpy as np
import jax
import jax.numpy as jnp
from jax import lax
from jax.experimental import pallas as pl
from jax.experimental.pallas import tpu as pltpu

F32 = jnp.float32
BF16 = jnp.bfloat16

D_MODEL = 4096
BATCH = 4
SEQ = 2048
DEC_BATCH = 128
DEC_SEQ = 4
PAST_LEN = 2048
PAGE_SIZE = 128
N_PAGES = PAST_LEN // PAGE_SIZE
D_RNN = D_MODEL // 2
RNN_BLOCKS = 16
RNN_BW = D_RNN // RNN_BLOCKS
CONV_W = 4
LRU_C = 8.0
N_HEADS = 16
HEAD_DIM = 128
N_KV = 4
GROUP = N_HEADS // N_KV
CMP_LEN = 32
CMP_STRIDE = 16
CMP_HID = 2 * HEAD_DIM
SEL_BLK = 64
N_SEL = 8
WINDOW = 512
N_BUCKETS = 32
MAX_EXACT = 16
MAX_DIST = 128
D_FF = ((8 * D_MODEL // 3 + 255) // 256) * 256
EPS = 1e-6
NEG = -1e30
BIG = 1e30
M_FLOOR = -1e29
Q_SCALE = HEAD_DIM ** -0.5

M_PROMPT = BATCH * SEQ
M_SAMPLE = DEC_BATCH * DEC_SEQ
M_TOK = M_PROMPT + M_SAMPLE

C_UGATE = 0
C_UX = C_UGATE + D_RNN
C_Q = C_UX + D_RNN
C_PAG = C_Q + N_HEADS * HEAD_DIM
C_WIN = C_PAG + 4 * N_KV * HEAD_DIM
C_GRNN = C_WIN + 2 * N_KV * HEAD_DIM
C_GATTN = C_GRNN + D_MODEL
C_GNSA = C_GATTN + D_MODEL
D_Z = C_GNSA + N_KV * 128

TM = 512
TF = 256
TN_IN = 1280
TN_MM = 1024
VMEM_LIMIT = 56 * 2 ** 20


def _cparams(n_axes, vmem=VMEM_LIMIT):
    return pltpu.CompilerParams(dimension_semantics=("arbitrary",) * n_axes, vmem_limit_bytes=vmem)


def _dot(a, b):
    return jnp.dot(a, b, preferred_element_type=F32)


def _dot_nt(a, b):
    return lax.dot_general(a, b, (((1,), (1,)), ((), ())), preferred_element_type=F32)


def _dot_split3(a, b):
    a1 = a.astype(BF16)
    r1 = a - a1.astype(F32)
    a2 = r1.astype(BF16)
    a3 = (r1 - a2.astype(F32)).astype(BF16)
    return _dot(a1, b) + _dot(a2, b) + _dot(a3, b)


def _dot_nt_split3(a, b):
    b1 = b.astype(BF16)
    r1 = b - b1.astype(F32)
    b2 = r1.astype(BF16)
    b3 = (r1 - b2.astype(F32)).astype(BF16)
    return _dot_nt(a, b1) + _dot_nt(a, b2) + _dot_nt(a, b3)


def _kv_head_rows(ref, k):
    n_rows = ref.shape[0]
    return ref.reshape(n_rows * N_KV, HEAD_DIM)[pl.ds(k, n_rows, stride=N_KV), :]


def _rms(x, g):
    return x * lax.rsqrt(jnp.mean(x * x, axis=-1, keepdims=True) + EPS) * g


def _ffn_body(x_ref, ln_ref, wg_ref, wu_ref, wd_ref, lnf_ref, o_ref, xn_ref, *, n_f, final_norm):
    f = pl.program_id(1)

    @pl.when(f == 0)
    def _():
        x = x_ref[...]
        xn_ref[...] = _rms(x, ln_ref[...]).astype(BF16)
        o_ref[...] = 2.0 * x

    xn = xn_ref[...]
    g = _dot(xn, wg_ref[...])
    u = _dot(xn, wu_ref[...])
    h = (g * jax.nn.sigmoid(g) * u).astype(BF16)
    o_ref[...] += _dot(h, wd_ref[...])

    @pl.when(f == n_f - 1)
    def _():
        y = 0.5 * o_ref[...]
        if final_norm:
            y = _rms(y, lnf_ref[...])
        o_ref[...] = y


def _ffn(x, ln, wg, wu, wd, lnf, final_norm, tile0=0, n_tiles=None):
    m = (x.shape[0] // TM if n_tiles is None else n_tiles) * TM
    n_f = D_FF // TF
    return pl.pallas_call(
        functools.partial(_ffn_body, n_f=n_f, final_norm=final_norm),
        grid=(m // TM, n_f),
        in_specs=[
            pl.BlockSpec((TM, D_MODEL), lambda i, f: (tile0 + i, 0), pipeline_mode=pl.Buffered(1)),
            pl.BlockSpec((1, D_MODEL), lambda i, f: (0, 0)),
            pl.BlockSpec((D_MODEL, TF), lambda i, f: (0, f)),
            pl.BlockSpec((D_MODEL, TF), lambda i, f: (0, f)),
            pl.BlockSpec((TF, D_MODEL), lambda i, f: (f, 0)),
            pl.BlockSpec((1, D_MODEL), lambda i, f: (0, 0)),
        ],
        out_specs=pl.BlockSpec((TM, D_MODEL), lambda i, f: (i, 0)),
        out_shape=jax.ShapeDtypeStruct((m, D_MODEL), F32),
        scratch_shapes=[pltpu.VMEM((TM, D_MODEL), BF16)],
        compiler_params=_cparams(2),
        name="ffn",
    )(x, ln.reshape(1, D_MODEL), wg, wu, wd, lnf.reshape(1, D_MODEL))


def _in_proj_body(x_ref, ln_ref, w_ref, o_ref, xn_ref):
    @pl.when(pl.program_id(1) == 0)
    def _():
        xn_ref[...] = _rms(x_ref[...], ln_ref[...]).astype(BF16)

    o_ref[...] = _dot_nt(xn_ref[...], w_ref[...])


def _in_proj(x, ln, w):
    m = x.shape[0]
    return pl.pallas_call(
        _in_proj_body,
        grid=(m // TM, D_Z // TN_IN),
        in_specs=[
            pl.BlockSpec((TM, D_MODEL), lambda i, j: (i, 0), pipeline_mode=pl.Buffered(1)),
            pl.BlockSpec((1, D_MODEL), lambda i, j: (0, 0)),
            pl.BlockSpec((TN_IN, D_MODEL), lambda i, j: (j, 0)),
        ],
        out_specs=pl.BlockSpec((TM, TN_IN), lambda i, j: (i, j)),
        out_shape=jax.ShapeDtypeStruct((m, D_Z), F32),
        scratch_shapes=[pltpu.VMEM((TM, D_MODEL), BF16)],
        compiler_params=_cparams(2),
        name="in_proj",
    )(x, ln.reshape(1, D_MODEL), w)


def _merge_body(gr_ref, oa_ref, wr_ref, wa_ref, zr_ref, za_ref, o_ref):
    y_rnn = _dot(gr_ref[...], wr_ref[...])
    y_attn = _dot(oa_ref[...], wa_ref[...])
    o_ref[...] = (jax.nn.sigmoid(zr_ref[...]) * y_rnn + jax.nn.sigmoid(za_ref[...]) * y_attn).astype(BF16)


def _merge(z, grnn, oattn, w_rnn, w_attn):
    m = z.shape[0]
    cr, ca = C_GRNN // TN_MM, C_GATTN // TN_MM
    return pl.pallas_call(
        _merge_body,
        grid=(m // TM, D_MODEL // TN_MM),
        in_specs=[
            pl.BlockSpec((TM, D_RNN), lambda i, j: (i, 0)),
            pl.BlockSpec((TM, N_HEADS * HEAD_DIM), lambda i, j: (i, 0)),
            pl.BlockSpec((D_RNN, TN_MM), lambda i, j: (0, j)),
            pl.BlockSpec((N_HEADS * HEAD_DIM, TN_MM), lambda i, j: (0, j)),
            pl.BlockSpec((TM, TN_MM), lambda i, j: (i, cr + j)),
            pl.BlockSpec((TM, TN_MM), lambda i, j: (i, ca + j)),
        ],
        out_specs=pl.BlockSpec((TM, TN_MM), lambda i, j: (i, j)),
        out_shape=jax.ShapeDtypeStruct((m, D_MODEL), BF16),
        compiler_params=_cparams(2),
        name="merge",
    )(grnn, oattn, w_rnn, w_attn, z, z)


def _out_proj_body(a_ref, w_ref, x_ref, o_ref):
    o_ref[...] = x_ref[...] + _dot(a_ref[...], w_ref[...])


def _out_proj(x, a, w):
    m = x.shape[0]
    return pl.pallas_call(
        _out_proj_body,
        grid=(m // TM, D_MODEL // TN_MM),
        in_specs=[
            pl.BlockSpec((TM, D_MODEL), lambda i, j: (i, 0)),
            pl.BlockSpec((D_MODEL, TN_MM), lambda i, j: (0, j)),
            pl.BlockSpec((TM, TN_MM), lambda i, j: (i, j)),
        ],
        out_specs=pl.BlockSpec((TM, TN_MM), lambda i, j: (i, j)),
        out_shape=jax.ShapeDtypeStruct((m, D_MODEL), F32),
        compiler_params=_cparams(2),
        name="out_proj",
    )(a, w, x)


def _softplus(v):
    return jnp.maximum(v, 0.0) + jnp.log1p(jnp.exp(-jnp.abs(v)))


def _lru_coeffs(xc, wa_ref, ba, wi_ref, bi, sp, n_blk):
    xb = xc.astype(BF16)
    ra = jnp.concatenate([_dot(xb[:, b * RNN_BW:(b + 1) * RNN_BW], wa_ref[b]) for b in range(n_blk)], axis=1)
    ia = jnp.concatenate([_dot(xb[:, b * RNN_BW:(b + 1) * RNN_BW], wi_ref[b]) for b in range(n_blk)], axis=1)
    r = jax.nn.sigmoid(ra + ba)
    i = jax.nn.sigmoid(ia + bi)
    log_a = -LRU_C * r * sp
    a = jnp.exp(log_a)
    bt = jnp.sqrt(-jnp.tanh(log_a) * (a * a + 1.0)) * (i * xc)
    return a, bt


RNN_TC = 256


def _rnn_prompt_body(ug_ref, ux_ref, cw_ref, cb_ref, wa_ref, ba_ref, wi_ref, bi_ref, lam_ref,
                     g_ref, h_ref, tail_ref, hc_ref):
    c = pl.program_id(1)
    tc = RNN_TC

    @pl.when(c == 0)
    def _():
        tail_ref[...] = jnp.zeros_like(tail_ref)
        hc_ref[...] = jnp.zeros_like(hc_ref)

    u = ux_ref[...]
    tail = tail_ref[...]
    row8 = lax.broadcasted_iota(jnp.int32, (8, D_RNN), 0)
    xc = cb_ref[...] + cw_ref[CONV_W - 1:CONV_W, :] * u
    for j in range(1, CONV_W):
        r = pltpu.roll(u, j, axis=0)
        first = jnp.where(row8 >= j, r[0:8], pltpu.roll(tail, j, axis=0))
        shifted = jnp.concatenate([first, r[8:]], axis=0)
        xc = xc + cw_ref[CONV_W - 1 - j:CONV_W - j, :] * shifted
    tail_ref[...] = u[tc - 8:tc]

    a, bt = _lru_coeffs(xc, wa_ref, ba_ref[...], wi_ref, bi_ref[...], _softplus(-lam_ref[...]), RNN_BLOCKS)

    row = lax.broadcasted_iota(jnp.int32, (tc, D_RNN), 0)
    s = 1
    while s < tc:
        keep = row >= s
        a_sh = jnp.where(keep, pltpu.roll(a, s, axis=0), 1.0)
        b_sh = jnp.where(keep, pltpu.roll(bt, s, axis=0), 0.0)
        bt = a * b_sh + bt
        a = a * a_sh
        s *= 2
    h = bt + a * hc_ref[7:8, :]
    hc_ref[...] = h[tc - 8:tc]
    g_ref[...] = (h * jax.nn.gelu(ug_ref[...])).astype(BF16)

    @pl.when(c == pl.num_programs(1) - 1)
    def _():
        h_ref[0] = h[tc - 8:tc]


def _rnn_prompt(z, cw, cb, wa, ba, wi, bi, lam):
    nc = SEQ // RNN_TC
    vec = lambda: pl.BlockSpec((1, D_RNN), lambda n, c: (0, 0))
    blk = lambda: pl.BlockSpec((RNN_BLOCKS, RNN_BW, RNN_BW), lambda n, c: (0, 0, 0))
    return pl.pallas_call(
        _rnn_prompt_body,
        grid=(BATCH, nc),
        in_specs=[
            pl.BlockSpec((RNN_TC, D_RNN), lambda n, c: (n * nc + c, C_UGATE // D_RNN)),
            pl.BlockSpec((RNN_TC, D_RNN), lambda n, c: (n * nc + c, C_UX // D_RNN)),
            pl.BlockSpec((CONV_W, D_RNN), lambda n, c: (0, 0)),
            vec(), blk(), vec(), blk(), vec(), vec(),
        ],
        out_specs=[
            pl.BlockSpec((RNN_TC, D_RNN), lambda n, c: (n * nc + c, 0)),
            pl.BlockSpec((1, 8, D_RNN), lambda n, c: (n, 0, 0)),
        ],
        out_shape=[jax.ShapeDtypeStruct((M_PROMPT, D_RNN), BF16),
                   jax.ShapeDtypeStruct((BATCH, 8, D_RNN), F32)],
        scratch_shapes=[pltpu.VMEM((8, D_RNN), F32), pltpu.VMEM((8, D_RNN), F32)],
        compiler_params=_cparams(2),
        name="rnn_prompt",
    )(z, z, cw, cb, wa, ba, wi, bi, lam)


RNN_SC = 512


def _rnn_sample_body(ug_ref, ux_ref, buf_ref, h0_ref, cw_ref, cb_ref, wa_ref, ba_ref, wi_ref, bi_ref, lam_ref,
                     g_ref, h_ref):
    full = [buf_ref[j] for j in range(CONV_W - 1)] + [ux_ref[t] for t in range(DEC_SEQ)]
    sp = _softplus(-lam_ref[...])
    h = h0_ref[...]
    for t in range(DEC_SEQ):
        xc = cb_ref[...]
        for k in range(CONV_W):
            xc = xc + full[t + k] * cw_ref[k:k + 1, :]
        a, bt = _lru_coeffs(xc, wa_ref, ba_ref[...], wi_ref, bi_ref[...], sp, RNN_SC // RNN_BW)
        h = a * h + bt
        g_ref[t] = (h * jax.nn.gelu(ug_ref[t])).astype(BF16)
    h_ref[...] = h


def _rnn_sample(ug_t, ux_t, buf_t, h0, cw, cb, wa, ba, wi, bi, lam):
    nb = RNN_SC // RNN_BW
    vec = lambda: pl.BlockSpec((1, RNN_SC), lambda c: (0, c))
    blk = lambda: pl.BlockSpec((nb, RNN_BW, RNN_BW), lambda c: (c, 0, 0))
    return pl.pallas_call(
        _rnn_sample_body,
        grid=(D_RNN // RNN_SC,),
        in_specs=[
            pl.BlockSpec((DEC_SEQ, DEC_BATCH, RNN_SC), lambda c: (0, 0, c)),
            pl.BlockSpec((DEC_SEQ, DEC_BATCH, RNN_SC), lambda c: (0, 0, c)),
            pl.BlockSpec((CONV_W - 1, DEC_BATCH, RNN_SC), lambda c: (0, 0, c)),
            pl.BlockSpec((DEC_BATCH, RNN_SC), lambda c: (0, c)),
            pl.BlockSpec((CONV_W, RNN_SC), lambda c: (0, c)),
            vec(), blk(), vec(), blk(), vec(), vec(),
        ],
        out_specs=[
            pl.BlockSpec((DEC_SEQ, DEC_BATCH, RNN_SC), lambda c: (0, 0, c)),
            pl.BlockSpec((DEC_BATCH, RNN_SC), lambda c: (0, c)),
        ],
        out_shape=[jax.ShapeDtypeStruct((DEC_SEQ, DEC_BATCH, D_RNN), BF16),
                   jax.ShapeDtypeStruct((DEC_BATCH, D_RNN), F32)],
        compiler_params=_cparams(1),
        name="rnn_sample",
    )(ug_t, ux_t, buf_t, h0, cw, cb, wa, ba, wi, bi, lam)


N_SEG = PAGE_SIZE // CMP_STRIDE
SEG_ROWS = N_PAGES * N_SEG
N_PAIR = CMP_STRIDE // 2


CMP_PG = 4
SLAB_PITCH = 24


def _compress_body(pt_ref, *refs, paged):
    n_in = CMP_PG * (2 if paged else 1)
    page_refs = refs[:n_in]
    w1_ref, pos_ref, b1_ref, w2_ref, ck_ref, cv_ref, stage_ref, pterm_ref, slab_ref = refs[n_in:]
    n = pl.program_id(0)
    q = pl.program_id(1)

    @pl.when((n == 0) & (q == 0))
    def _():
        for kind in range(2):
            acc = jnp.zeros((8, 2 * CMP_HID), F32)
            for pr in range(N_PAIR):
                acc = acc + _dot(pos_ref[:, pr * 256:(pr + 1) * 256].astype(BF16), w1_ref[kind, pr])
            pterm_ref[kind] = acc

    for j in range(CMP_PG):
        for kind in range(2):
            for k in range(N_KV):
                if paged:
                    slab = _kv_head_rows(page_refs[j * 2 + kind], k)
                else:
                    kk = kind * N_KV + k
                    slab = page_refs[j][:, kk * HEAD_DIM:(kk + 1) * HEAD_DIM]
                for s in range(N_SEG):
                    slab_ref[(j * 2 + kind) * N_KV + k, pl.ds(s * SLAB_PITCH, CMP_STRIDE), :] = (
                        slab[s * CMP_STRIDE:(s + 1) * CMP_STRIDE])

    for j in range(CMP_PG):
        seg0 = (q * CMP_PG + j) * N_SEG
        for kind in range(2):
            for k in range(N_KV):
                for l in range(CMP_STRIDE):
                    piece = slab_ref[(j * 2 + kind) * N_KV + k, pl.ds(l, N_SEG, stride=SLAB_PITCH), :]
                    stage_ref[kind, l // 2, pl.ds(k * SEG_ROWS + seg0, N_SEG),
                              pl.ds((l % 2) * HEAD_DIM, HEAD_DIM)] = piece

    @pl.when(q == N_PAGES // CMP_PG - 1)
    def _():
        for kind, out_ref in ((0, ck_ref), (1, cv_ref)):
            acc = jnp.zeros((N_KV * SEG_ROWS, 2 * CMP_HID), F32)
            for pr in range(N_PAIR):
                acc = acc + _dot(stage_ref[kind, pr].astype(BF16), w1_ref[kind, pr])
            nxt = pltpu.roll(acc[:, CMP_HID:], N_KV * SEG_ROWS - 1, axis=0)
            pt = pterm_ref[kind]
            posterm = pt[0:1, :CMP_HID] + pt[1:2, CMP_HID:] + b1_ref[kind]
            hid = acc[:, :CMP_HID] + nxt + posterm
            out_ref[0] = _dot(jax.nn.gelu(hid).astype(BF16), w2_ref[kind]).astype(BF16)


def _compress(src, pt_flat, col_blk, n_seq, w1, pos, b1, w2, name):
    paged = col_blk is None
    page_of = lambda n, p, pt, j: pt[n * N_PAGES + p * CMP_PG + j]
    out_spec = pl.BlockSpec((1, N_KV * SEG_ROWS, HEAD_DIM), lambda n, p, pt: (n, 0, 0))
    if paged:
        page_specs = [pl.BlockSpec((None, None, PAGE_SIZE, None, N_KV, HEAD_DIM),
                                   lambda n, p, pt, j=j, kind=kind: (0, page_of(n, p, pt, j), 0, kind, 0, 0))
                      for j in range(CMP_PG) for kind in range(2)]
    else:
        page_specs = [pl.BlockSpec((None, PAGE_SIZE, 2 * N_KV * HEAD_DIM),
                                   lambda n, p, pt, j=j: (page_of(n, p, pt, j), 0, col_blk))
                      for j in range(CMP_PG)]
    grid_spec = pltpu.PrefetchScalarGridSpec(
        num_scalar_prefetch=1,
        grid=(n_seq, N_PAGES // CMP_PG),
        in_specs=page_specs + [
            pl.BlockSpec((2, N_PAIR, 256, 2 * CMP_HID), lambda n, p, pt: (0, 0, 0, 0)),
            pl.BlockSpec((8, CMP_STRIDE * HEAD_DIM), lambda n, p, pt: (0, 0)),
            pl.BlockSpec((2, 1, CMP_HID), lambda n, p, pt: (0, 0, 0)),
            pl.BlockSpec((2, CMP_HID, HEAD_DIM), lambda n, p, pt: (0, 0, 0)),
        ],
        out_specs=[out_spec, out_spec],
        scratch_shapes=[pltpu.VMEM((2, N_PAIR, N_KV * SEG_ROWS, 256), F32),
                        pltpu.VMEM((2, 8, 2 * CMP_HID), F32),
                        pltpu.VMEM((CMP_PG * 2 * N_KV, N_SEG * SLAB_PITCH, HEAD_DIM), F32)],
    )
    shp = jax.ShapeDtypeStruct((n_seq, N_KV * SEG_ROWS, HEAD_DIM), BF16)
    return pl.pallas_call(
        functools.partial(_compress_body, paged=paged),
        grid_spec=grid_spec,
        out_shape=[shp, shp],
        compiler_params=_cparams(2),
        name=name,
    )(pt_flat, *([src] * len(page_specs)), w1, pos, b1, w2)


def _select_blocks(score, cur, n_blk):
    jj = lax.broadcasted_iota(jnp.int32, score.shape, 1)
    forced = (jj == 0) | (jj == cur) | (jj == cur - 1)
    sc = jnp.where(forced, BIG, jnp.where(jj <= cur, score, NEG))
    rank = jnp.zeros(score.shape, F32)
    for i in range(n_blk):
        si = sc[:, i:i + 1]
        beats = (si > sc) | ((si == sc) & (jj > i))
        rank = rank + jnp.where(beats, 1.0, 0.0)
    sel = (rank < float(min(N_SEL, n_blk))) & (jj <= cur) & (jj < n_blk)
    return jnp.where(sel, 1.0, 0.0).astype(BF16)


def _select_blocks_t(score, cur):
    n_blk = score.shape[0]
    jj = lax.broadcasted_iota(jnp.int32, score.shape, 0)
    forced = (jj == 0) | (jj == cur) | (jj == cur - 1)
    sc = jnp.where(forced, BIG, jnp.where(jj <= cur, score, NEG))
    rank = jnp.zeros(score.shape, F32)
    for i in range(n_blk):
        si = sc[i:i + 1, :]
        beats = (si > sc) | ((si == sc) & (jj > i))
        rank = rank + jnp.where(beats, 1.0, 0.0)
    sel = (rank < float(min(N_SEL, n_blk))) & (jj <= cur)
    return jnp.where(sel, 1.0, 0.0)


def _softmax_rows(logits):
    m = jnp.maximum(jnp.max(logits, axis=-1, keepdims=True), M_FLOOR)
    e = jnp.exp(logits - m)
    s = jnp.sum(e, axis=-1, keepdims=True)
    return e / jnp.where(s > 0.0, s, 1.0)


MASKED_TILE = WINDOW // 128 + 1
ATT_HP = 2


def _attn_prompt_body(zq_ref, zg_ref, ck_ref, cv_ref, ks_ref, vs_ref, kw_ref, vw_ref,
                      bc_ref, tz_ref, ovt_ref, e_ref, o_ref,
                      ksb, vsb, kwb, vwb, selm_ref, s_ref, mel_ref, lel_ref, acc_ref):
    qt = pl.program_id(2)
    rows = GROUP * 128

    @pl.when(qt == 0)
    def _():
        ksb[...] = ks_ref[...].astype(BF16)
        vsb[...] = vs_ref[...].astype(BF16)
        kwb[...] = kw_ref[...].astype(BF16)
        vwb[...] = vw_ref[...].astype(BF16)

    heads = range(ATT_HP)
    lanes = lambda h: slice(h * HEAD_DIM, (h + 1) * HEAD_DIM)
    groups = lambda h: slice(h * GROUP, (h + 1) * GROUP)
    n_blk = SEQ // SEL_BLK
    cur = jnp.right_shift(qt * 128 + lax.broadcasted_iota(jnp.int32, (n_blk, 128), 1), 6)
    qq, o_c = [], []
    for h in heads:
        q = zq_ref[:, h * GROUP * HEAD_DIM:(h + 1) * GROUP * HEAD_DIM] * Q_SCALE
        qh = jnp.concatenate([q[:, g * HEAD_DIM:(g + 1) * HEAD_DIM] for g in range(GROUP)], axis=0).astype(BF16)
        qq.append(qh)
        pc = _softmax_rows(_dot_nt(qh, ck_ref[h]) + bc_ref[0, groups(h)].reshape(rows, 128))
        o_c.append(_dot(pc.astype(BF16), cv_ref[h]))
        ps = pc[0:128] + pc[128:256] + pc[256:384] + pc[384:512]
        sel_t = _select_blocks_t(_dot_nt_split3(ovt_ref[...], ps), cur)
        sel = jnp.concatenate([sel_t, jnp.zeros((128 - n_blk, 128), F32)], axis=0).T.astype(BF16)
        for j in range(SEQ // 256):
            selm_ref[h, j] = (_dot(sel, e_ref[j]) - 1.0) * BIG

    def attend(k_ref, v_ref, lo, tile_of_delta, use_sel):
        hi = jnp.right_shift(qt, 1) + 1
        mel_ref[...] = jnp.full(mel_ref.shape, M_FLOOR, F32)

        def logits_pass(j, carry):
            off = pl.multiple_of(j * 256, 256)
            d0 = qt - 2 * j
            i0, i1 = tile_of_delta(d0), tile_of_delta(d0 - 1)
            for h in heads:
                s = _dot_nt(qq[h], k_ref[pl.ds(off, 256), lanes(h)]).reshape(GROUP, 128, 256)
                s = s + jnp.concatenate([tz_ref[i0, groups(h)], tz_ref[i1, groups(h)]], axis=-1)
                if use_sel:
                    s = s + selm_ref[h, j][None]
                s = s.reshape(rows, 256)
                s_ref[h, j] = s
                mel_ref[h] = jnp.maximum(mel_ref[h], jnp.maximum(s[:, :128], s[:, 128:]))
            return carry

        lax.fori_loop(lo, hi, logits_pass, 0)
        m = [jnp.max(mel_ref[h], axis=-1, keepdims=True) for h in heads]
        lel_ref[...] = jnp.zeros_like(lel_ref)
        acc_ref[...] = jnp.zeros_like(acc_ref)

        def value_pass(j, carry):
            off = pl.multiple_of(j * 256, 256)
            for h in heads:
                pe = jnp.exp(s_ref[h, j] - m[h])
                lel_ref[h] += pe[:, :128] + pe[:, 128:]
                acc_ref[h] += _dot(pe.astype(BF16), v_ref[pl.ds(off, 256), lanes(h)])
            return carry

        lax.fori_loop(lo, hi, value_pass, 0)
        outs = []
        for h in heads:
            l = jnp.sum(lel_ref[h], axis=-1, keepdims=True)
            outs.append(acc_ref[h] / jnp.where(l > 0.0, l, 1.0))
        return outs

    n_win = WINDOW // 128
    o_s = attend(ksb, vsb, 0, lambda d: jnp.where(d < 0, MASKED_TILE, jnp.minimum(d, 2)), True)
    o_w = attend(kwb, vwb, jnp.right_shift(jnp.maximum(qt - n_win, 0), 1),
                 lambda d: jnp.where((d < 0) | (d > n_win), MASKED_TILE, d), False)

    gates = jax.nn.sigmoid(zg_ref[...])
    outs = []
    for h in heads:
        for g in range(GROUP):
            r = slice(g * 128, (g + 1) * 128)
            gate = lambda branch: gates[:, h * 128 + branch * GROUP + g:h * 128 + branch * GROUP + g + 1]
            outs.append(gate(0) * o_c[h][r] + gate(1) * o_s[h][r] + gate(2) * o_w[h][r])
    o_ref[...] = jnp.concatenate(outs, axis=1).astype(BF16)


def _attn_prompt(z, ck, cv, bias_cmp, tz, ov, emat):
    nq = SEQ // 128
    nhp = N_KV // ATT_HP
    kw = ATT_HP * HEAD_DIM
    qw = ATT_HP * GROUP * HEAD_DIM
    kv_col = lambda base, kind: (lambda n, k, t: (n, (base + kind * N_KV * HEAD_DIM) // kw + k))
    kvspec = lambda base, kind: pl.BlockSpec((SEQ, kw), kv_col(base, kind))
    rows = GROUP * 128
    return pl.pallas_call(
        _attn_prompt_body,
        grid=(BATCH, nhp, nq),
        in_specs=[
            pl.BlockSpec((128, qw), lambda n, k, t: (n * nq + t, C_Q // qw + k)),
            pl.BlockSpec((128, ATT_HP * 128), lambda n, k, t: (n * nq + t, C_GNSA // (ATT_HP * 128) + k)),
            pl.BlockSpec((ATT_HP, SEG_ROWS, HEAD_DIM), lambda n, k, t: (n * nhp + k, 0, 0)),
            pl.BlockSpec((ATT_HP, SEG_ROWS, HEAD_DIM), lambda n, k, t: (n * nhp + k, 0, 0)),
            kvspec(C_PAG, 2), kvspec(C_PAG, 3), kvspec(C_WIN, 0), kvspec(C_WIN, 1),
            pl.BlockSpec((1, ATT_HP * GROUP, 128, 128), lambda n, k, t: (t, k, 0, 0)),
            pl.BlockSpec((MASKED_TILE + 1, ATT_HP * GROUP, 128, 128), lambda n, k, t: (0, k, 0, 0)),
            pl.BlockSpec((SEQ // SEL_BLK, 128), lambda n, k, t: (0, 0)),
            pl.BlockSpec((SEQ // 256, 128, 256), lambda n, k, t: (0, 0, 0)),
        ],
        out_specs=pl.BlockSpec((128, qw), lambda n, k, t: (n * nq + t, k)),
        out_shape=jax.ShapeDtypeStruct((M_PROMPT, N_HEADS * HEAD_DIM), BF16),
        scratch_shapes=[pltpu.VMEM((SEQ, kw), BF16)] * 4 + [
            pltpu.VMEM((ATT_HP, SEQ // 256, 128, 256), F32),
            pltpu.VMEM((ATT_HP, SEQ // 256, rows, 256), F32),
            pltpu.VMEM((ATT_HP, rows, 128), F32), pltpu.VMEM((ATT_HP, rows, 128), F32),
            pltpu.VMEM((ATT_HP, rows, HEAD_DIM), F32)],
        compiler_params=_cparams(3),
        name="attn_prompt",
    )(z, z, ck, cv, z, z, z, z, bias_cmp, tz, ov, emat)


S_ROWS = GROUP * N_KV * DEC_SEQ


ATT_SS = 2


def _attn_sample_body(pt_ref, q_ref, gs_ref, ck_ref, cv_ref, *rest):
    n_pg = ATT_SS * 2 * N_PAGES
    page_refs = rest[:n_pg]
    win_refs = rest[n_pg:n_pg + 2 * ATT_SS]
    (nkv_ref, nwin_ref, bcmp_ref, bsel_ref, bnew_ref, bwin_ref, ov_ref, e_ref,
     o_ref, s_ref, nk_ref, nw_ref) = rest[n_pg + 2 * ATT_SS:]
    kv_of_row = jnp.bitwise_and(jnp.right_shift(lax.broadcasted_iota(jnp.int32, (S_ROWS, 1), 0), 2), N_KV - 1)
    col = lambda k, half: pl.ds(half * N_KV * HEAD_DIM + k * HEAD_DIM, HEAD_DIM)
    head_rows = _kv_head_rows
    rowmax = lambda s: jnp.max(s, axis=-1, keepdims=True)
    rowsum = lambda s: jnp.sum(s, axis=-1, keepdims=True)

    @pl.when(pl.program_id(0) == 0)
    def _():
        nk_ref[...] = jnp.zeros_like(nk_ref)
        nw_ref[...] = jnp.zeros_like(nw_ref)

    def one_sequence(i):
        pages = page_refs[i * 2 * N_PAGES:(i + 1) * 2 * N_PAGES]
        kwin_ref, vwin_ref = win_refs[2 * i], win_refs[2 * i + 1]
        qq = (q_ref[i] * Q_SCALE).astype(BF16)

        def logits(get_k):
            out = None
            for k in range(N_KV):
                s = jnp.where(kv_of_row == k, _dot_nt(qq, get_k(k).astype(BF16)), 0.0)
                out = s if out is None else out + s
            return out

        def weighted(pe, get_v):
            out = None
            for k in range(N_KV):
                o = _dot(jnp.where(kv_of_row == k, pe, 0.0).astype(BF16), get_v(k).astype(BF16))
                out = o if out is None else out + o
            return out

        nk_ref[i, 0:8, :] = nkv_ref[i]
        nw_ref[i, 0:8, :] = nwin_ref[i]

        pc = _softmax_rows(logits(lambda k: ck_ref[i, pl.ds(k * SEG_ROWS, SEG_ROWS), :]) + bcmp_ref[...])
        o_c = weighted(pc, lambda k: cv_ref[i, pl.ds(k * SEG_ROWS, SEG_ROWS), :])
        ps = pc + pltpu.roll(pc, 16, axis=0) + pltpu.roll(pc, 32, axis=0) + pltpu.roll(pc, 48, axis=0)
        score = _dot_split3(ps, ov_ref[...])
        n_blk = -(-(PAST_LEN + DEC_SEQ) // SEL_BLK)
        cur = jnp.full((S_ROWS, 128), PAST_LEN // SEL_BLK, jnp.int32)
        sel = _select_blocks(score, cur, n_blk)
        key_mask = (_dot(sel, e_ref[...]) - 1.0) * BIG

        m = jnp.full((S_ROWS, 1), M_FLOOR, F32)
        for p in range(N_PAGES):
            s = (logits(lambda k: head_rows(pages[2 * p], k)) + bsel_ref[p]
                 + key_mask[:, p * PAGE_SIZE:(p + 1) * PAGE_SIZE])
            s_ref[i, p] = s
            m = jnp.maximum(m, rowmax(s))
        sn = logits(lambda k: nk_ref[i, :, col(k, 0)]) + bnew_ref[...]
        m = jnp.maximum(m, rowmax(sn))
        pn = jnp.exp(sn - m)
        l = rowsum(pn)
        acc = weighted(pn, lambda k: nk_ref[i, :, col(k, 1)])
        for p in range(N_PAGES):
            pe = jnp.exp(s_ref[i, p] - m)
            l = l + rowsum(pe)
            acc = acc + weighted(pe, lambda k: head_rows(pages[2 * p + 1], k))
        o_s = acc / jnp.where(l > 0.0, l, 1.0)

        sw = logits(lambda k: head_rows(kwin_ref, k)) + bwin_ref[...]
        sn = logits(lambda k: nw_ref[i, :, col(k, 0)]) + bnew_ref[...]
        m = jnp.maximum(jnp.maximum(rowmax(sw), rowmax(sn)), M_FLOOR)
        pw = jnp.exp(sw - m)
        pn = jnp.exp(sn - m)
        l = rowsum(pw) + rowsum(pn)
        o_w = weighted(pw, lambda k: head_rows(vwin_ref, k)) + weighted(pn, lambda k: nw_ref[i, :, col(k, 1)])
        o_w = o_w / jnp.where(l > 0.0, l, 1.0)

        gates = jax.nn.sigmoid(gs_ref[i])
        o_ref[i] = gates[:, 0:1] * o_c + gates[:, 1:2] * o_s + gates[:, 2:3] * o_w

    for i in range(ATT_SS):
        one_sequence(i)


def _attn_sample(pt_flat, q_s, g_s, ck, cv, cache_kv, nkv, state_win, nwin, bcmp, bsel, bnew, bwin, ov, emat):
    const2 = lambda shape: pl.BlockSpec(shape, lambda n, pt: (0, 0))
    per_seq = lambda rows, width: pl.BlockSpec((ATT_SS, rows, width), lambda n, pt: (n, 0, 0))
    page_specs = [pl.BlockSpec((None, None, PAGE_SIZE, None, N_KV, HEAD_DIM),
                               lambda n, pt, i=i, p=p, kind=kind:
                               (0, pt[(n * ATT_SS + i) * N_PAGES + p], 0, kind, 0, 0))
                  for i in range(ATT_SS) for p in range(N_PAGES) for kind in (2, 3)]
    win_specs = [pl.BlockSpec((None, None, WINDOW, None, N_KV, HEAD_DIM),
                              lambda n, pt, i=i, kind=kind: (0, n * ATT_SS + i, 0, kind, 0, 0))
                 for i in range(ATT_SS) for kind in (0, 1)]
    grid_spec = pltpu.PrefetchScalarGridSpec(
        num_scalar_prefetch=1,
        grid=(DEC_BATCH // ATT_SS,),
        in_specs=[
            per_seq(S_ROWS, HEAD_DIM), per_seq(S_ROWS, 128),
            per_seq(N_KV * SEG_ROWS, HEAD_DIM), per_seq(N_KV * SEG_ROWS, HEAD_DIM),
        ] + page_specs + win_specs + [
            per_seq(8, 1024), per_seq(8, 1024),
            const2((S_ROWS, 128)),
            pl.BlockSpec((N_PAGES, S_ROWS, 128), lambda n, pt: (0, 0, 0)),
            const2((S_ROWS, 128)),
            const2((S_ROWS, WINDOW)),
            const2((128, 128)),
            const2((128, PAST_LEN)),
        ],
        out_specs=per_seq(S_ROWS, HEAD_DIM),
        scratch_shapes=[
            pltpu.VMEM((ATT_SS, N_PAGES, S_ROWS, 128), F32),
            pltpu.VMEM((ATT_SS, 128, 1024), F32), pltpu.VMEM((ATT_SS, 128, 1024), F32),
        ],
    )
    return pl.pallas_call(
        _attn_sample_body,
        grid_spec=grid_spec,
        out_shape=jax.ShapeDtypeStruct((DEC_BATCH, S_ROWS, HEAD_DIM), F32),
        compiler_params=_cparams(1),
        name="attn_sample",
    )(pt_flat, q_s, g_s, ck, cv, *([cache_kv] * len(page_specs)), *([state_win] * len(win_specs)), nkv, nwin,
      bcmp, bsel, bnew, bwin, ov, emat)


def _t5_bucket(dist):
    d = jnp.maximum(dist, 0)
    df = jnp.maximum(d, 1).astype(F32)
    large = MAX_EXACT + (jnp.log(df / MAX_EXACT) / math.log(MAX_DIST / MAX_EXACT)
                         * (N_BUCKETS - MAX_EXACT)).astype(jnp.int32)
    large = jnp.minimum(large, N_BUCKETS - 1)
    return jnp.where(d < MAX_EXACT, d, large)


def _bias_lookup_body(rb_ref, idx_ref, o_ref):
    idx = idx_ref[0]
    for h in range(N_HEADS):
        acc = jnp.full(idx.shape, NEG, F32)
        for b in range(N_BUCKETS):
            acc = jnp.where(idx == b, rb_ref[b * N_HEADS + h], acc)
        o_ref[0, h] = acc


def _bias_table(rel_bias, dist, valid, name):
    p, r, _ = dist.shape
    idx = jnp.where(jnp.asarray(valid), _t5_bucket(jnp.asarray(dist, jnp.int32)), -1)
    return pl.pallas_call(
        _bias_lookup_body,
        grid=(p,),
        in_specs=[pl.BlockSpec(memory_space=pltpu.SMEM), pl.BlockSpec((1, r, 128), lambda i: (i, 0, 0))],
        out_specs=pl.BlockSpec((1, N_HEADS, r, 128), lambda i: (i, 0, 0, 0)),
        out_shape=jax.ShapeDtypeStruct((p, N_HEADS, r, 128), F32),
        compiler_params=_cparams(1),
        name=name,
    )(rel_bias.astype(F32).reshape(-1), idx)


def _overlap(nc, nb):
    cs = np.arange(nc)[:, None] * CMP_STRIDE
    js = np.arange(nb)[None, :] * SEL_BLK
    ov = np.clip(np.minimum(cs + CMP_LEN, js + SEL_BLK) - np.maximum(cs, js), 0, None) / CMP_LEN
    out = np.zeros((128, 128), np.float32)
    out[:nc, :nb] = ov
    return jnp.asarray(out, BF16)


def _position_tables(rel_bias):
    nc = SEG_ROWS - 1
    t = np.arange(128)[None, :, None]
    c = np.arange(128)[None, None, :]
    cend = c * CMP_STRIDE + CMP_LEN - 1
    d = np.arange(MASKED_TILE + 1)[:, None, None] * 128 + t - c
    tz = _bias_table(rel_bias, d, (d >= 0) & (d < WINDOW), "bias_tiles")
    d = np.arange(SEQ // 128)[:, None, None] * 128 + t - cend
    bias_cmp = _bias_table(rel_bias, d, (d >= 0) & (c < nc), "bias_cmp")
    ts = np.arange(8)[None, :, None]
    qpos = PAST_LEN + ts
    live = ts < DEC_SEQ
    d_cmp = qpos - cend
    d_sel = qpos - (np.arange(N_PAGES)[:, None, None] * PAGE_SIZE + c)
    d_new = ts - c
    d_win = qpos - (PAST_LEN - WINDOW + np.arange(WINDOW // 128)[:, None, None] * 128 + c)
    d = np.concatenate([d_cmp, d_sel, d_new, d_win], axis=0)
    valid = np.concatenate([(d_cmp >= 0) & (c < nc), d_sel >= 0, (d_new >= 0) & (c < DEC_SEQ),
                            (d_win >= 0) & (d_win < WINDOW)], axis=0) & live
    o = _bias_table(rel_bias, d, valid, "bias_sample")[:, :, :DEC_SEQ]
    o = jnp.transpose(o.reshape(-1, N_KV, GROUP, DEC_SEQ, 128), (0, 2, 1, 3, 4)).reshape(-1, S_ROWS, 128)
    bcmp, bsel, bnew = o[0], o[1:1 + N_PAGES], o[1 + N_PAGES]
    bwin = jnp.transpose(o[2 + N_PAGES:], (1, 0, 2)).reshape(S_ROWS, WINDOW)
    keys = np.arange(SEQ)
    e_all = (np.arange(128)[:, None] == (keys // SEL_BLK)[None, :]).astype(np.float32)
    e_tiles = jnp.asarray(e_all.reshape(128, SEQ // 256, 256).transpose(1, 0, 2), BF16)
    ovt_p = jnp.transpose(_overlap(nc, SEQ // SEL_BLK))[:SEQ // SEL_BLK]
    return dict(tz=tz, bias_cmp=bias_cmp, bcmp=bcmp, bsel=bsel, bnew=bnew, bwin=bwin,
                ovt_p=ovt_p, ov_s=_overlap(nc, -(-(PAST_LEN + DEC_SEQ) // SEL_BLK)),
                e_tiles=e_tiles, e_all=jnp.asarray(e_all, BF16))


W_TR = 512
W_TC = 2048
N_GNSA = 3 * N_HEADS


W_TB = 16


def _permute_w_in_body(a_ref, *refs):
    b_refs = refs[:N_GNSA // W_TB]
    g_ref, o_ref = refs[N_GNSA // W_TB:]
    j = pl.program_id(0)
    first, last = C_GRNN // W_TR, C_GNSA // W_TR

    @pl.when(j < first)
    def _():
        o_ref[...] = a_ref[...].astype(BF16)

    @pl.when((j >= first) & (j < last))
    def _():
        o_ref[0:W_TR - N_GNSA, :] = a_ref[N_GNSA:W_TR, :].astype(BF16)
        for t, b_ref in enumerate(b_refs):
            r0 = W_TR - N_GNSA + t * W_TB
            o_ref[r0:r0 + W_TB, :] = b_ref[...].astype(BF16)

    @pl.when(j == last)
    def _():
        o_ref[...] = g_ref[...]


def _permute_w_in(w_in):
    w_t = jnp.swapaxes(w_in, 1, 2)
    g_nsa = w_t[0, C_GRNN:C_GRNN + N_GNSA].reshape(N_KV, GROUP, 3, D_MODEL)
    g_nsa = jnp.transpose(g_nsa, (0, 2, 1, 3)).reshape(N_KV, 3 * GROUP, D_MODEL)
    g_nsa = jnp.pad(g_nsa, ((0, 0), (0, 128 - 3 * GROUP), (0, 0))).reshape(N_KV * 128, D_MODEL).astype(BF16)
    first, last = C_GRNN // W_TR, C_GNSA // W_TR
    n_b = N_GNSA // W_TB
    next_rows = lambda t: pl.BlockSpec(
        (None, W_TB, W_TC), lambda j, c: (0, jnp.clip(j + 1, first + 1, last) * (W_TR // W_TB) + t, c))
    return pl.pallas_call(
        _permute_w_in_body,
        grid=(D_Z // W_TR, D_MODEL // W_TC),
        in_specs=[pl.BlockSpec((None, W_TR, W_TC), lambda j, c: (0, jnp.minimum(j, last - 1), c))]
        + [next_rows(t) for t in range(n_b)]
        + [pl.BlockSpec((W_TR, W_TC), lambda j, c: (0, c))],
        out_specs=pl.BlockSpec((W_TR, W_TC), lambda j, c: (j, c)),
        out_shape=jax.ShapeDtypeStruct((D_Z, D_MODEL), BF16),
        compiler_params=_cparams(2),
        name="w_in_layout",
    )(w_t, *([w_t] * n_b), g_nsa)


def _cmp_weights(w1_k, w1_v, b1_k, b1_v, w2_k, w2_v, pos):
    def cat(w1):
        w = w1.reshape(2, CMP_STRIDE * HEAD_DIM, CMP_HID)
        return jnp.concatenate([w[0], w[1]], axis=1).reshape(N_PAIR, 256, 2 * CMP_HID)
    w1 = jnp.stack([cat(w1_k), cat(w1_v)]).astype(BF16)
    posm = jnp.pad(pos.reshape(2, CMP_STRIDE * HEAD_DIM), ((0, 6), (0, 0)))
    b1 = jnp.stack([b1_k, b1_v]).reshape(2, 1, CMP_HID)
    w2 = jnp.stack([w2_k, w2_v]).astype(BF16)
    return w1, posm, b1, w2


def kernel(x_prompt, x_sample, cache_kv, page_table, state_win, state_conv, state_h, rel_bias, ln_final, ln_ffn1, w_ffn1_gate, w_ffn1_up, w_ffn1_down, ln_mix, w_in, conv_w, conv_b, rg_wa, rg_ba, rg_wi, rg_bi, rg_lambda, cmp_pos, cmp_k_w1, cmp_k_b1, cmp_k_w2, cmp_v_w1, cmp_v_b1, cmp_v_w2, w_br_rnn, w_br_attn, w_out, ln_ffn2, w_ffn2_gate, w_ffn2_up, w_ffn2_down):
    tabs = _position_tables(rel_bias)
    x = jnp.concatenate([x_prompt.reshape(M_PROMPT, D_MODEL), x_sample.reshape(M_SAMPLE, D_MODEL)], axis=0)

    x = _ffn(x, ln_ffn1[0], w_ffn1_gate[0].astype(BF16), w_ffn1_up[0].astype(BF16), w_ffn1_down[0].astype(BF16),
             ln_final, False)
    z = _in_proj(x, ln_mix[0], _permute_w_in(w_in))
    z_s = z[M_PROMPT:]

    vec = lambda v: v.reshape(1, D_RNN)
    rnn_w = (conv_w[0], vec(conv_b[0]), rg_wa[0].astype(BF16), vec(rg_ba[0]), rg_wi[0].astype(BF16),
             vec(rg_bi[0]), vec(rg_lambda[0]))
    g_p, h_p = _rnn_prompt(z, *rnn_w)
    tmajor = lambda a: jnp.transpose(a.reshape(DEC_BATCH, -1, D_RNN), (1, 0, 2))
    g_s, h_s = _rnn_sample(tmajor(z_s[:, C_UGATE:C_UGATE + D_RNN]), tmajor(z_s[:, C_UX:C_UX + D_RNN]),
                           tmajor(state_conv[0]), state_h[0], *rnn_w)
    grnn = jnp.concatenate([g_p, jnp.transpose(g_s, (1, 0, 2)).reshape(M_SAMPLE, D_RNN)], axis=0)

    cw = _cmp_weights(cmp_k_w1[0], cmp_v_w1[0], cmp_k_b1[0], cmp_v_b1[0], cmp_k_w2[0], cmp_v_w2[0], cmp_pos[0])
    pt_prompt = jnp.arange(BATCH * N_PAGES, dtype=jnp.int32)
    pt_sample = page_table.reshape(-1).astype(jnp.int32)
    ck_p, cv_p = _compress(z.reshape(M_TOK // PAGE_SIZE, PAGE_SIZE, D_Z), pt_prompt, C_PAG // 1024, BATCH, *cw,
                           name="compress_prompt")
    ck_s, cv_s = _compress(cache_kv, pt_sample, None, DEC_BATCH, *cw, name="compress_sample")
    o_p = _attn_prompt(z, ck_p.reshape(BATCH * N_KV, SEG_ROWS, HEAD_DIM), cv_p.reshape(BATCH * N_KV, SEG_ROWS, HEAD_DIM),
                       tabs["bias_cmp"], tabs["tz"], tabs["ovt_p"], tabs["e_tiles"])

    def rows_gkt(a, width):
        a = a.reshape(DEC_BATCH, DEC_SEQ, N_KV, GROUP, width)
        return jnp.transpose(a, (0, 3, 2, 1, 4)).reshape(DEC_BATCH, S_ROWS, width)

    q_s = rows_gkt(z_s[:, C_Q:C_Q + N_HEADS * HEAD_DIM], HEAD_DIM)
    gn = z_s[:, C_GNSA:].reshape(M_SAMPLE, N_KV, 128)[:, :, :3 * GROUP].reshape(M_SAMPLE, N_KV, 3, GROUP)
    g_s3 = jnp.pad(rows_gkt(jnp.transpose(gn, (0, 1, 3, 2)), 3), ((0, 0), (0, 0), (0, 125)))
    pad8 = lambda a: jnp.pad(a.reshape(DEC_BATCH, DEC_SEQ, -1), ((0, 0), (0, 8 - DEC_SEQ), (0, 0)))
    nkv = pad8(z_s[:, C_PAG + 2 * N_KV * HEAD_DIM:C_PAG + 4 * N_KV * HEAD_DIM])
    nwin = pad8(z_s[:, C_WIN:C_WIN + 2 * N_KV * HEAD_DIM])
    o_s = _attn_sample(pt_sample, q_s, g_s3, ck_s, cv_s, cache_kv, nkv, state_win, nwin,
                       tabs["bcmp"], tabs["bsel"], tabs["bnew"], tabs["bwin"], tabs["ov_s"], tabs["e_all"])
    o_s = jnp.transpose(o_s.reshape(DEC_BATCH, GROUP, N_KV, DEC_SEQ, HEAD_DIM), (0, 3, 2, 1, 4))
    oattn = jnp.concatenate([o_p, o_s.reshape(M_SAMPLE, N_HEADS * HEAD_DIM).astype(BF16)], axis=0)

    merged = _merge(z, grnn, oattn, w_br_rnn[0].astype(BF16), w_br_attn[0].astype(BF16))
    x = _out_proj(x, merged, w_out[0].astype(BF16))
    ffn2 = functools.partial(_ffn, x, ln_ffn2[0], w_ffn2_gate[0].astype(BF16), w_ffn2_up[0].astype(BF16),
                             w_ffn2_down[0].astype(BF16), ln_final, True)
    y_p = ffn2(tile0=0, n_tiles=M_PROMPT // TM)
    y_s = ffn2(tile0=M_PROMPT // TM, n_tiles=M_SAMPLE // TM)

    kv = z[:, C_PAG:C_PAG + 4 * N_KV * HEAD_DIM]
    wn = z[:, C_WIN:C_WIN + 2 * N_KV * HEAD_DIM]
    keep = CONV_W - 1
    conv_p = jnp.stack([lax.slice(z, ((n + 1) * SEQ - keep, C_UX), ((n + 1) * SEQ, C_UX + D_RNN))
                        for n in range(BATCH)])
    conv_s = z_s[:, C_UX:C_UX + D_RNN].reshape(DEC_BATCH, DEC_SEQ, D_RNN)[:, DEC_SEQ - keep:]
    win_p = wn[:M_PROMPT].reshape(BATCH, SEQ, 2, N_KV, HEAD_DIM)[:, SEQ - WINDOW:]
    win_s = jnp.concatenate([state_win.reshape(DEC_BATCH, WINDOW, 2, N_KV, HEAD_DIM),
                             wn[M_PROMPT:].reshape(DEC_BATCH, DEC_SEQ, 2, N_KV, HEAD_DIM)], axis=1)
    return (
        y_p.reshape(BATCH, SEQ, D_MODEL),
        y_s.reshape(DEC_BATCH, DEC_SEQ, D_MODEL),
        kv[:M_PROMPT].reshape(1, BATCH, SEQ, 4, N_KV, HEAD_DIM),
        kv[M_PROMPT:].reshape(1, DEC_BATCH, DEC_SEQ, 4, N_KV, HEAD_DIM),
        win_p[None],
        win_s[None, :, DEC_SEQ:],
        conv_p[None],
        conv_s[None],
        h_p[None, :, 7],
        h_s[None],
    )
```

```python
import functools
import math

import numpy as np
import jax
import jax.numpy as jnp
from jax import lax
from jax.experimental import pallas as pl
from jax.experimental.pallas import tpu as pltpu

F32 = jnp.float32
BF16 = jnp.bfloat16

D_MODEL = 4096
BATCH = 4
SEQ = 2048
DEC_BATCH = 128
DEC_SEQ = 4
PAST_LEN = 2048
PAGE_SIZE = 128
N_PAGES = PAST_LEN // PAGE_SIZE
D_RNN = D_MODEL // 2
RNN_BLOCKS = 16
RNN_BW = D_RNN // RNN_BLOCKS
CONV_W = 4
LRU_C = 8.0
N_HEADS = 16
HEAD_DIM = 128
N_KV = 4
GROUP = N_HEADS // N_KV
CMP_LEN = 32
CMP_STRIDE = 16
CMP_HID = 2 * HEAD_DIM
SEL_BLK = 64
N_SEL = 8
WINDOW = 512
N_BUCKETS = 32
MAX_EXACT = 16
MAX_DIST = 128
D_FF = ((8 * D_MODEL // 3 + 255) // 256) * 256
EPS = 1e-6
NEG = -1e30
BIG = 1e30
M_FLOOR = -1e29
Q_SCALE = HEAD_DIM ** -0.5

M_PROMPT = BATCH * SEQ
M_SAMPLE = DEC_BATCH * DEC_SEQ
M_TOK = M_PROMPT + M_SAMPLE

C_UGATE = 0
C_UX = C_UGATE + D_RNN
C_Q = C_UX + D_RNN
C_PAG = C_Q + N_HEADS * HEAD_DIM
C_WIN = C_PAG + 4 * N_KV * HEAD_DIM
C_GRNN = C_WIN + 2 * N_KV * HEAD_DIM
C_GATTN = C_GRNN + D_MODEL
C_GNSA = C_GATTN + D_MODEL
D_Z = C_GNSA + N_KV * 128

TM = 512
TF = 256
TN_IN = 1280
TN_MM = 1024
VMEM_LIMIT = 56 * 2 ** 20


def _cparams(n_axes, vmem=VMEM_LIMIT):
    return pltpu.CompilerParams(dimension_semantics=("arbitrary",) * n_axes, vmem_limit_bytes=vmem)


def _dot(a, b):
    return jnp.dot(a, b, preferred_element_type=F32)


def _dot_nt(a, b):
    return lax.dot_general(a, b, (((1,), (1,)), ((), ())), preferred_element_type=F32)


def _dot_split3(a, b):
    a1 = a.astype(BF16)
    r1 = a - a1.astype(F32)
    a2 = r1.astype(BF16)
    a3 = (r1 - a2.astype(F32)).astype(BF16)
    return _dot(a1, b) + _dot(a2, b) + _dot(a3, b)


def _dot_nt_split3(a, b):
    b1 = b.astype(BF16)
    r1 = b - b1.astype(F32)
    b2 = r1.astype(BF16)
    b3 = (r1 - b2.astype(F32)).astype(BF16)
    return _dot_nt(a, b1) + _dot_nt(a, b2) + _dot_nt(a, b3)


def _kv_head_rows(ref, k):
    n_rows = ref.shape[0]
    return ref.reshape(n_rows * N_KV, HEAD_DIM)[pl.ds(k, n_rows, stride=N_KV), :]


def _rms(x, g):
    return x * lax.rsqrt(jnp.mean(x * x, axis=-1, keepdims=True) + EPS) * g


def _ffn_body(x_ref, ln_ref, wg_ref, wu_ref, wd_ref, lnf_ref, o_ref, xn_ref, *, n_f, final_norm):
    f = pl.program_id(1)

    @pl.when(f == 0)
    def _():
        x = x_ref[...]
        xn_ref[...] = _rms(x, ln_ref[...]).astype(BF16)
        o_ref[...] = 2.0 * x

    xn = xn_ref[...]
    g = _dot(xn, wg_ref[...])
    u = _dot(xn, wu_ref[...])
    h = (g * jax.nn.sigmoid(g) * u).astype(BF16)
    o_ref[...] += _dot(h, wd_ref[...])

    @pl.when(f == n_f - 1)
    def _():
        y = 0.5 * o_ref[...]
        if final_norm:
            y = _rms(y, lnf_ref[...])
        o_ref[...] = y


def _ffn(x, ln, wg, wu, wd, lnf, final_norm, tile0=0, n_tiles=None):
    m = (x.shape[0] // TM if n_tiles is None else n_tiles) * TM
    n_f = D_FF // TF
    return pl.pallas_call(
        functools.partial(_ffn_body, n_f=n_f, final_norm=final_norm),
        grid=(m // TM, n_f),
        in_specs=[
            pl.BlockSpec((TM, D_MODEL), lambda i, f: (tile0 + i, 0), pipeline_mode=pl.Buffered(1)),
            pl.BlockSpec((1, D_MODEL), lambda i, f: (0, 0)),
            pl.BlockSpec((D_MODEL, TF), lambda i, f: (0, f)),
            pl.BlockSpec((D_MODEL, TF), lambda i, f: (0, f)),
            pl.BlockSpec((TF, D_MODEL), lambda i, f: (f, 0)),
            pl.BlockSpec((1, D_MODEL), lambda i, f: (0, 0)),
        ],
        out_specs=pl.BlockSpec((TM, D_MODEL), lambda i, f: (i, 0)),
        out_shape=jax.ShapeDtypeStruct((m, D_MODEL), F32),
        scratch_shapes=[pltpu.VMEM((TM, D_MODEL), BF16)],
        compiler_params=_cparams(2),
        name="ffn",
    )(x, ln.reshape(1, D_MODEL), wg, wu, wd, lnf.reshape(1, D_MODEL))


def _in_proj_body(x_ref, ln_ref, w_ref, o_ref, xn_ref):
    @pl.when(pl.program_id(1) == 0)
    def _():
        xn_ref[...] = _rms(x_ref[...], ln_ref[...]).astype(BF16)

    o_ref[...] = _dot_nt(xn_ref[...], w_ref[...])


def _in_proj(x, ln, w):
    m = x.shape[0]
    return pl.pallas_call(
        _in_proj_body,
        grid=(m // TM, D_Z // TN_IN),
        in_specs=[
            pl.BlockSpec((TM, D_MODEL), lambda i, j: (i, 0), pipeline_mode=pl.Buffered(1)),
            pl.BlockSpec((1, D_MODEL), lambda i, j: (0, 0)),
            pl.BlockSpec((TN_IN, D_MODEL), lambda i, j: (j, 0)),
        ],
        out_specs=pl.BlockSpec((TM, TN_IN), lambda i, j: (i, j)),
        out_shape=jax.ShapeDtypeStruct((m, D_Z), F32),
        scratch_shapes=[pltpu.VMEM((TM, D_MODEL), BF16)],
        compiler_params=_cparams(2),
        name="in_proj",
    )(x, ln.reshape(1, D_MODEL), w)


def _merge_body(gr_ref, oa_ref, wr_ref, wa_ref, zr_ref, za_ref, o_ref):
    y_rnn = _dot(gr_ref[...], wr_ref[...])
    y_attn = _dot(oa_ref[...], wa_ref[...])
    o_ref[...] = (jax.nn.sigmoid(zr_ref[...]) * y_rnn + jax.nn.sigmoid(za_ref[...]) * y_attn).astype(BF16)


def _merge(z, grnn, oattn, w_rnn, w_attn):
    m = z.shape[0]
    cr, ca = C_GRNN // TN_MM, C_GATTN // TN_MM
    return pl.pallas_call(
        _merge_body,
        grid=(m // TM, D_MODEL // TN_MM),
        in_specs=[
            pl.BlockSpec((TM, D_RNN), lambda i, j: (i, 0)),
            pl.BlockSpec((TM, N_HEADS * HEAD_DIM), lambda i, j: (i, 0)),
            pl.BlockSpec((D_RNN, TN_MM), lambda i, j: (0, j)),
            pl.BlockSpec((N_HEADS * HEAD_DIM, TN_MM), lambda i, j: (0, j)),
            pl.BlockSpec((TM, TN_MM), lambda i, j: (i, cr + j)),
            pl.BlockSpec((TM, TN_MM), lambda i, j: (i, ca + j)),
        ],
        out_specs=pl.BlockSpec((TM, TN_MM), lambda i, j: (i, j)),
        out_shape=jax.ShapeDtypeStruct((m, D_MODEL), BF16),
        compiler_params=_cparams(2),
        name="merge",
    )(grnn, oattn, w_rnn, w_attn, z, z)


def _out_proj_body(a_ref, w_ref, x_ref, o_ref):
    o_ref[...] = x_ref[...] + _dot(a_ref[...], w_ref[...])


def _out_proj(x, a, w):
    m = x.shape[0]
    return pl.pallas_call(
        _out_proj_body,
        grid=(m // TM, D_MODEL // TN_MM),
        in_specs=[
            pl.BlockSpec((TM, D_MODEL), lambda i, j: (i, 0)),
            pl.BlockSpec((D_MODEL, TN_MM), lambda i, j: (0, j)),
            pl.BlockSpec((TM, TN_MM), lambda i, j: (i, j)),
        ],
        out_specs=pl.BlockSpec((TM, TN_MM), lambda i, j: (i, j)),
        out_shape=jax.ShapeDtypeStruct((m, D_MODEL), F32),
        compiler_params=_cparams(2),
        name="out_proj",
    )(a, w, x)


def _softplus(v):
    return jnp.maximum(v, 0.0) + jnp.log1p(jnp.exp(-jnp.abs(v)))


def _lru_coeffs(xc, wa_ref, ba, wi_ref, bi, sp, n_blk):
    xb = xc.astype(BF16)
    ra = jnp.concatenate([_dot(xb[:, b * RNN_BW:(b + 1) * RNN_BW], wa_ref[b]) for b in range(n_blk)], axis=1)
    ia = jnp.concatenate([_dot(xb[:, b * RNN_BW:(b + 1) * RNN_BW], wi_ref[b]) for b in range(n_blk)], axis=1)
    r = jax.nn.sigmoid(ra + ba)
    i = jax.nn.sigmoid(ia + bi)
    log_a = -LRU_C * r * sp
    a = jnp.exp(log_a)
    bt = jnp.sqrt(-jnp.tanh(log_a) * (a * a + 1.0)) * (i * xc)
    return a, bt


RNN_TC = 256


def _rnn_prompt_body(ug_ref, ux_ref, cw_ref, cb_ref, wa_ref, ba_ref, wi_ref, bi_ref, lam_ref,
                     g_ref, h_ref, tail_ref, hc_ref):
    c = pl.program_id(1)
    tc = RNN_TC

    @pl.when(c == 0)
    def _():
        tail_ref[...] = jnp.zeros_like(tail_ref)
        hc_ref[...] = jnp.zeros_like(hc_ref)

    u = ux_ref[...]
    tail = tail_ref[...]
    row8 = lax.broadcasted_iota(jnp.int32, (8, D_RNN), 0)
    xc = cb_ref[...] + cw_ref[CONV_W - 1:CONV_W, :] * u
    for j in range(1, CONV_W):
        r = pltpu.roll(u, j, axis=0)
        first = jnp.where(row8 >= j, r[0:8], pltpu.roll(tail, j, axis=0))
        shifted = jnp.concatenate([first, r[8:]], axis=0)
        xc = xc + cw_ref[CONV_W - 1 - j:CONV_W - j, :] * shifted
    tail_ref[...] = u[tc - 8:tc]

    a, bt = _lru_coeffs(xc, wa_ref, ba_ref[...], wi_ref, bi_ref[...], _softplus(-lam_ref[...]), RNN_BLOCKS)

    row = lax.broadcasted_iota(jnp.int32, (tc, D_RNN), 0)
    s = 1
    while s < tc:
        keep = row >= s
        a_sh = jnp.where(keep, pltpu.roll(a, s, axis=0), 1.0)
        b_sh = jnp.where(keep, pltpu.roll(bt, s, axis=0), 0.0)
        bt = a * b_sh + bt
        a = a * a_sh
        s *= 2
    h = bt + a * hc_ref[7:8, :]
    hc_ref[...] = h[tc - 8:tc]
    g_ref[...] = (h * jax.nn.gelu(ug_ref[...])).astype(BF16)

    @pl.when(c == pl.num_programs(1) - 1)
    def _():
        h_ref[0] = h[tc - 8:tc]


def _rnn_prompt(z, cw, cb, wa, ba, wi, bi, lam):
    nc = SEQ // RNN_TC
    vec = lambda: pl.BlockSpec((1, D_RNN), lambda n, c: (0, 0))
    blk = lambda: pl.BlockSpec((RNN_BLOCKS, RNN_BW, RNN_BW), lambda n, c: (0, 0, 0))
    return pl.pallas_call(
        _rnn_prompt_body,
        grid=(BATCH, nc),
        in_specs=[
            pl.BlockSpec((RNN_TC, D_RNN), lambda n, c: (n * nc + c, C_UGATE // D_RNN)),
            pl.BlockSpec((RNN_TC, D_RNN), lambda n, c: (n * nc + c, C_UX // D_RNN)),
            pl.BlockSpec((CONV_W, D_RNN), lambda n, c: (0, 0)),
            vec(), blk(), vec(), blk(), vec(), vec(),
        ],
        out_specs=[
            pl.BlockSpec((RNN_TC, D_RNN), lambda n, c: (n * nc + c, 0)),
            pl.BlockSpec((1, 8, D_RNN), lambda n, c: (n, 0, 0)),
        ],
        out_shape=[jax.ShapeDtypeStruct((M_PROMPT, D_RNN), BF16),
                   jax.ShapeDtypeStruct((BATCH, 8, D_RNN), F32)],
        scratch_shapes=[pltpu.VMEM((8, D_RNN), F32), pltpu.VMEM((8, D_RNN), F32)],
        compiler_params=_cparams(2),
        name="rnn_prompt",
    )(z, z, cw, cb, wa, ba, wi, bi, lam)


RNN_SC = 512


def _rnn_sample_body(ug_ref, ux_ref, buf_ref, h0_ref, cw_ref, cb_ref, wa_ref, ba_ref, wi_ref, bi_ref, lam_ref,
                     g_ref, h_ref):
    full = [buf_ref[j] for j in range(CONV_W - 1)] + [ux_ref[t] for t in range(DEC_SEQ)]
    sp = _softplus(-lam_ref[...])
    h = h0_ref[...]
    for t in range(DEC_SEQ):
        xc = cb_ref[...]
        for k in range(CONV_W):
            xc = xc + full[t + k] * cw_ref[k:k + 1, :]
        a, bt = _lru_coeffs(xc, wa_ref, ba_ref[...], wi_ref, bi_ref[...], sp, RNN_SC // RNN_BW)
        h = a * h + bt
        g_ref[t] = (h * jax.nn.gelu(ug_ref[t])).astype(BF16)
    h_ref[...] = h


def _rnn_sample(ug_t, ux_t, buf_t, h0, cw, cb, wa, ba, wi, bi, lam):
    nb = RNN_SC // RNN_BW
    vec = lambda: pl.BlockSpec((1, RNN_SC), lambda c: (0, c))
    blk = lambda: pl.BlockSpec((nb, RNN_BW, RNN_BW), lambda c: (c, 0, 0))
    return pl.pallas_call(
        _rnn_sample_body,
        grid=(D_RNN // RNN_SC,),
        in_specs=[
            pl.BlockSpec((DEC_SEQ, DEC_BATCH, RNN_SC), lambda c: (0, 0, c)),
            pl.BlockSpec((DEC_SEQ, DEC_BATCH, RNN_SC), lambda c: (0, 0, c)),
            pl.BlockSpec((CONV_W - 1, DEC_BATCH, RNN_SC), lambda c: (0, 0, c)),
            pl.BlockSpec((DEC_BATCH, RNN_SC), lambda c: (0, c)),
            pl.BlockSpec((CONV_W, RNN_SC), lambda c: (0, c)),
            vec(), blk(), vec(), blk(), vec(), vec(),
        ],
        out_specs=[
            pl.BlockSpec((DEC_SEQ, DEC_BATCH, RNN_SC), lambda c: (0, 0, c)),
            pl.BlockSpec((DEC_BATCH, RNN_SC), lambda c: (0, c)),
        ],
        out_shape=[jax.ShapeDtypeStruct((DEC_SEQ, DEC_BATCH, D_RNN), BF16),
                   jax.ShapeDtypeStruct((DEC_BATCH, D_RNN), F32)],
        compiler_params=_cparams(1),
        name="rnn_sample",
    )(ug_t, ux_t, buf_t, h0, cw, cb, wa, ba, wi, bi, lam)


N_SEG = PAGE_SIZE // CMP_STRIDE
SEG_ROWS = N_PAGES * N_SEG
N_PAIR = CMP_STRIDE // 2


CMP_PG = 4
SLAB_PITCH = 20


def _compress_body(pt_ref, *refs, paged):
    n_in = CMP_PG * (2 if paged else 1)
    page_refs = refs[:n_in]
    w1_ref, pos_ref, b1_ref, w2_ref, ck_ref, cv_ref, stage_ref, pterm_ref, slab_ref = refs[n_in:]
    n = pl.program_id(0)
    q = pl.program_id(1)

    @pl.when((n == 0) & (q == 0))
    def _():
        for kind in range(2):
            acc = jnp.zeros((8, 2 * CMP_HID), F32)
            for pr in range(N_PAIR):
                acc = acc + _dot(pos_ref[:, pr * 256:(pr + 1) * 256].astype(BF16), w1_ref[kind, pr])
            pterm_ref[kind] = acc

    for j in range(CMP_PG):
        for kind in range(2):
            for k in range(N_KV):
                if paged:
                    slab = _kv_head_rows(page_refs[j * 2 + kind], k)
                else:
                    kk = kind * N_KV + k
                    slab = page_refs[j][:, kk * HEAD_DIM:(kk + 1) * HEAD_DIM]
                for s in range(N_SEG):
                    slab_ref[(j * 2 + kind) * N_KV + k, pl.ds(s * SLAB_PITCH, CMP_STRIDE), :] = (
                        slab[s * CMP_STRIDE:(s + 1) * CMP_STRIDE])

    for j in range(CMP_PG):
        seg0 = (q * CMP_PG + j) * N_SEG
        for kind in range(2):
            for k in range(N_KV):
                for l in range(CMP_STRIDE):
                    piece = slab_ref[(j * 2 + kind) * N_KV + k, pl.ds(l, N_SEG, stride=SLAB_PITCH), :]
                    stage_ref[kind, l // 2, pl.ds(k * SEG_ROWS + seg0, N_SEG),
                              pl.ds((l % 2) * HEAD_DIM, HEAD_DIM)] = piece

    @pl.when(q == N_PAGES // CMP_PG - 1)
    def _():
        for kind, out_ref in ((0, ck_ref), (1, cv_ref)):
            acc = jnp.zeros((N_KV * SEG_ROWS, 2 * CMP_HID), F32)
            for pr in range(N_PAIR):
                acc = acc + _dot(stage_ref[kind, pr].astype(BF16), w1_ref[kind, pr])
            nxt = pltpu.roll(acc[:, CMP_HID:], N_KV * SEG_ROWS - 1, axis=0)
            pt = pterm_ref[kind]
            posterm = pt[0:1, :CMP_HID] + pt[1:2, CMP_HID:] + b1_ref[kind]
            hid = acc[:, :CMP_HID] + nxt + posterm
            out_ref[0] = _dot(jax.nn.gelu(hid).astype(BF16), w2_ref[kind]).astype(BF16)


def _compress(src, pt_flat, col_blk, n_seq, w1, pos, b1, w2, name):
    paged = col_blk is None
    page_of = lambda n, p, pt, j: pt[n * N_PAGES + p * CMP_PG + j]
    out_spec = pl.BlockSpec((1, N_KV * SEG_ROWS, HEAD_DIM), lambda n, p, pt: (n, 0, 0))
    if paged:
        page_specs = [pl.BlockSpec((None, None, PAGE_SIZE, None, N_KV, HEAD_DIM),
                                   lambda n, p, pt, j=j, kind=kind: (0, page_of(n, p, pt, j), 0, kind, 0, 0))
                      for j in range(CMP_PG) for kind in range(2)]
    else:
        page_specs = [pl.BlockSpec((None, PAGE_SIZE, 2 * N_KV * HEAD_DIM),
                                   lambda n, p, pt, j=j: (page_of(n, p, pt, j), 0, col_blk))
                      for j in range(CMP_PG)]
    grid_spec = pltpu.PrefetchScalarGridSpec(
        num_scalar_prefetch=1,
        grid=(n_seq, N_PAGES // CMP_PG),
        in_specs=page_specs + [
            pl.BlockSpec((2, N_PAIR, 256, 2 * CMP_HID), lambda n, p, pt: (0, 0, 0, 0)),
            pl.BlockSpec((8, CMP_STRIDE * HEAD_DIM), lambda n, p, pt: (0, 0)),
            pl.BlockSpec((2, 1, CMP_HID), lambda n, p, pt: (0, 0, 0)),
            pl.BlockSpec((2, CMP_HID, HEAD_DIM), lambda n, p, pt: (0, 0, 0)),
        ],
        out_specs=[out_spec, out_spec],
        scratch_shapes=[pltpu.VMEM((2, N_PAIR, N_KV * SEG_ROWS, 256), F32),
                        pltpu.VMEM((2, 8, 2 * CMP_HID), F32),
                        pltpu.VMEM((CMP_PG * 2 * N_KV, N_SEG * SLAB_PITCH, HEAD_DIM), F32)],
    )
    shp = jax.ShapeDtypeStruct((n_seq, N_KV * SEG_ROWS, HEAD_DIM), BF16)
    return pl.pallas_call(
        functools.partial(_compress_body, paged=paged),
        grid_spec=grid_spec,
        out_shape=[shp, shp],
        compiler_params=_cparams(2),
        name=name,
    )(pt_flat, *([src] * len(page_specs)), w1, pos, b1, w2)


def _select_blocks(score, cur, n_blk):
    jj = lax.broadcasted_iota(jnp.int32, score.shape, 1)
    forced = (jj == 0) | (jj == cur) | (jj == cur - 1)
    sc = jnp.where(forced, BIG, jnp.where(jj <= cur, score, NEG))
    rank = jnp.zeros(score.shape, F32)
    for i in range(n_blk):
        si = sc[:, i:i + 1]
        beats = (si > sc) | ((si == sc) & (jj > i))
        rank = rank + jnp.where(beats, 1.0, 0.0)
    sel = (rank < float(min(N_SEL, n_blk))) & (jj <= cur) & (jj < n_blk)
    return jnp.where(sel, 1.0, 0.0).astype(BF16)


def _select_blocks_t(score, cur):
    n_blk = score.shape[0]
    jj = lax.broadcasted_iota(jnp.int32, score.shape, 0)
    forced = (jj == 0) | (jj == cur) | (jj == cur - 1)
    sc = jnp.where(forced, BIG, jnp.where(jj <= cur, score, NEG))
    rank = jnp.zeros(score.shape, F32)
    for i in range(n_blk):
        si = sc[i:i + 1, :]
        beats = (si > sc) | ((si == sc) & (jj > i))
        rank = rank + jnp.where(beats, 1.0, 0.0)
    sel = (rank < float(min(N_SEL, n_blk))) & (jj <= cur)
    return jnp.where(sel, 1.0, 0.0)


def _softmax_rows(logits):
    m = jnp.maximum(jnp.max(logits, axis=-1, keepdims=True), M_FLOOR)
    e = jnp.exp(logits - m)
    s = jnp.sum(e, axis=-1, keepdims=True)
    return e / jnp.where(s > 0.0, s, 1.0)


MASKED_TILE = WINDOW // 128 + 1
ATT_HP = 2


def _attn_prompt_body(zq_ref, zg_ref, ck_ref, cv_ref, ks_ref, vs_ref, kw_ref, vw_ref,
                      bc_ref, tz_ref, ovt_ref, e_ref, o_ref,
                      ksb, vsb, kwb, vwb, selm_ref, s_ref, mel_ref, lel_ref, acc_ref):
    qt = pl.program_id(2)
    rows = GROUP * 128

    @pl.when(qt == 0)
    def _():
        ksb[...] = ks_ref[...].astype(BF16)
        vsb[...] = vs_ref[...].astype(BF16)
        kwb[...] = kw_ref[...].astype(BF16)
        vwb[...] = vw_ref[...].astype(BF16)

    heads = range(ATT_HP)
    lanes = lambda h: slice(h * HEAD_DIM, (h + 1) * HEAD_DIM)
    groups = lambda h: slice(h * GROUP, (h + 1) * GROUP)
    n_blk = SEQ // SEL_BLK
    cur = jnp.right_shift(qt * 128 + lax.broadcasted_iota(jnp.int32, (n_blk, 128), 1), 6)
    qq, o_c = [], []
    for h in heads:
        q = zq_ref[:, h * GROUP * HEAD_DIM:(h + 1) * GROUP * HEAD_DIM] * Q_SCALE
        qh = jnp.concatenate([q[:, g * HEAD_DIM:(g + 1) * HEAD_DIM] for g in range(GROUP)], axis=0).astype(BF16)
        qq.append(qh)
        pc = _softmax_rows(_dot_nt(qh, ck_ref[h]) + bc_ref[0, groups(h)].reshape(rows, 128))
        o_c.append(_dot(pc.astype(BF16), cv_ref[h]))
        ps = pc[0:128] + pc[128:256] + pc[256:384] + pc[384:512]
        sel_t = _select_blocks_t(_dot_nt_split3(ovt_ref[...], ps), cur)
        sel = jnp.concatenate([sel_t, jnp.zeros((128 - n_blk, 128), F32)], axis=0).T.astype(BF16)
        for j in range(SEQ // 256):
            selm_ref[h, j] = (_dot(sel, e_ref[j]) - 1.0) * BIG

    def attend(k_ref, v_ref, lo, tile_of_delta, use_sel):
        hi = jnp.right_shift(qt, 1) + 1
        mel_ref[...] = jnp.full(mel_ref.shape, M_FLOOR, F32)

        def logits_pass(j, carry):
            off = pl.multiple_of(j * 256, 256)
            d0 = qt - 2 * j
            i0, i1 = tile_of_delta(d0), tile_of_delta(d0 - 1)
            for h in heads:
                s = _dot_nt(qq[h], k_ref[pl.ds(off, 256), lanes(h)]).reshape(GROUP, 128, 256)
                s = s + jnp.concatenate([tz_ref[i0, groups(h)], tz_ref[i1, groups(h)]], axis=-1)
                if use_sel:
                    s = s + selm_ref[h, j][None]
                s = s.reshape(rows, 256)
                s_ref[h, j] = s
                mel_ref[h] = jnp.maximum(mel_ref[h], jnp.maximum(s[:, :128], s[:, 128:]))
            return carry

        lax.fori_loop(lo, hi, logits_pass, 0)
        m = [jnp.max(mel_ref[h], axis=-1, keepdims=True) for h in heads]
        lel_ref[...] = jnp.zeros_like(lel_ref)
        acc_ref[...] = jnp.zeros_like(acc_ref)

        def value_pass(j, carry):
            off = pl.multiple_of(j * 256, 256)
            for h in heads:
                pe = jnp.exp(s_ref[h, j] - m[h])
                lel_ref[h] += pe[:, :128] + pe[:, 128:]
                acc_ref[h] += _dot(pe.astype(BF16), v_ref[pl.ds(off, 256), lanes(h)])
            return carry

        lax.fori_loop(lo, hi, value_pass, 0)
        outs = []
        for h in heads:
            l = jnp.sum(lel_ref[h], axis=-1, keepdims=True)
            outs.append(acc_ref[h] / jnp.where(l > 0.0, l, 1.0))
        return outs

    n_win = WINDOW // 128
    o_s = attend(ksb, vsb, 0, lambda d: jnp.where(d < 0, MASKED_TILE, jnp.minimum(d, 2)), True)
    o_w = attend(kwb, vwb, jnp.right_shift(jnp.maximum(qt - n_win, 0), 1),
                 lambda d: jnp.where((d < 0) | (d > n_win), MASKED_TILE, d), False)

    gates = jax.nn.sigmoid(zg_ref[...])
    outs = []
    for h in heads:
        for g in range(GROUP):
            r = slice(g * 128, (g + 1) * 128)
            gate = lambda branch: gates[:, h * 128 + branch * GROUP + g:h * 128 + branch * GROUP + g + 1]
            outs.append(gate(0) * o_c[h][r] + gate(1) * o_s[h][r] + gate(2) * o_w[h][r])
    o_ref[...] = jnp.concatenate(outs, axis=1).astype(BF16)


def _attn_prompt(z, ck, cv, bias_cmp, tz, ov, emat):
    nq = SEQ // 128
    nhp = N_KV // ATT_HP
    kw = ATT_HP * HEAD_DIM
    qw = ATT_HP * GROUP * HEAD_DIM
    kv_col = lambda base, kind: (lambda n, k, t: (n, (base + kind * N_KV * HEAD_DIM) // kw + k))
    kvspec = lambda base, kind: pl.BlockSpec((SEQ, kw), kv_col(base, kind))
    rows = GROUP * 128
    return pl.pallas_call(
        _attn_prompt_body,
        grid=(BATCH, nhp, nq),
        in_specs=[
            pl.BlockSpec((128, qw), lambda n, k, t: (n * nq + t, C_Q // qw + k)),
            pl.BlockSpec((128, ATT_HP * 128), lambda n, k, t: (n * nq + t, C_GNSA // (ATT_HP * 128) + k)),
            pl.BlockSpec((ATT_HP, SEG_ROWS, HEAD_DIM), lambda n, k, t: (n * nhp + k, 0, 0)),
            pl.BlockSpec((ATT_HP, SEG_ROWS, HEAD_DIM), lambda n, k, t: (n * nhp + k, 0, 0)),
            kvspec(C_PAG, 2), kvspec(C_PAG, 3), kvspec(C_WIN, 0), kvspec(C_WIN, 1),
            pl.BlockSpec((1, ATT_HP * GROUP, 128, 128), lambda n, k, t: (t, k, 0, 0)),
            pl.BlockSpec((MASKED_TILE + 1, ATT_HP * GROUP, 128, 128), lambda n, k, t: (0, k, 0, 0)),
            pl.BlockSpec((SEQ // SEL_BLK, 128), lambda n, k, t: (0, 0)),
            pl.BlockSpec((SEQ // 256, 128, 256), lambda n, k, t: (0, 0, 0)),
        ],
        out_specs=pl.BlockSpec((128, qw), lambda n, k, t: (n * nq + t, k)),
        out_shape=jax.ShapeDtypeStruct((M_PROMPT, N_HEADS * HEAD_DIM), BF16),
        scratch_shapes=[pltpu.VMEM((SEQ, kw), BF16)] * 4 + [
            pltpu.VMEM((ATT_HP, SEQ // 256, 128, 256), F32),
            pltpu.VMEM((ATT_HP, SEQ // 256, rows, 256), F32),
            pltpu.VMEM((ATT_HP, rows, 128), F32), pltpu.VMEM((ATT_HP, rows, 128), F32),
            pltpu.VMEM((ATT_HP, rows, HEAD_DIM), F32)],
        compiler_params=_cparams(3),
        name="attn_prompt",
    )(z, z, ck, cv, z, z, z, z, bias_cmp, tz, ov, emat)


S_ROWS = GROUP * N_KV * DEC_SEQ


ATT_SS = 2


def _attn_sample_body(pt_ref, q_ref, gs_ref, ck_ref, cv_ref, *rest):
    n_pg = ATT_SS * 2 * N_PAGES
    page_refs = rest[:n_pg]
    win_refs = rest[n_pg:n_pg + 2 * ATT_SS]
    (nkv_ref, nwin_ref, bcmp_ref, bsel_ref, bnew_ref, bwin_ref, ov_ref, e_ref,
     o_ref, s_ref, nk_ref, nw_ref) = rest[n_pg + 2 * ATT_SS:]
    kv_of_row = jnp.bitwise_and(jnp.right_shift(lax.broadcasted_iota(jnp.int32, (S_ROWS, 1), 0), 2), N_KV - 1)
    col = lambda k, half: pl.ds(half * N_KV * HEAD_DIM + k * HEAD_DIM, HEAD_DIM)
    head_rows = _kv_head_rows
    rowmax = lambda s: jnp.max(s, axis=-1, keepdims=True)
    rowsum = lambda s: jnp.sum(s, axis=-1, keepdims=True)

    @pl.when(pl.program_id(0) == 0)
    def _():
        nk_ref[...] = jnp.zeros_like(nk_ref)
        nw_ref[...] = jnp.zeros_like(nw_ref)

    def one_sequence(i):
        pages = page_refs[i * 2 * N_PAGES:(i + 1) * 2 * N_PAGES]
        kwin_ref, vwin_ref = win_refs[2 * i], win_refs[2 * i + 1]
        qq = (q_ref[i] * Q_SCALE).astype(BF16)

        def logits(get_k):
            out = None
            for k in range(N_KV):
                s = jnp.where(kv_of_row == k, _dot_nt(qq, get_k(k).astype(BF16)), 0.0)
                out = s if out is None else out + s
            return out

        def weighted(pe, get_v):
            out = None
            for k in range(N_KV):
                o = _dot(jnp.where(kv_of_row == k, pe, 0.0).astype(BF16), get_v(k).astype(BF16))
                out = o if out is None else out + o
            return out

        nk_ref[i, 0:8, :] = nkv_ref[i]
        nw_ref[i, 0:8, :] = nwin_ref[i]

        pc = _softmax_rows(logits(lambda k: ck_ref[i, pl.ds(k * SEG_ROWS, SEG_ROWS), :]) + bcmp_ref[...])
        o_c = weighted(pc, lambda k: cv_ref[i, pl.ds(k * SEG_ROWS, SEG_ROWS), :])
        ps = pc + pltpu.roll(pc, 16, axis=0) + pltpu.roll(pc, 32, axis=0) + pltpu.roll(pc, 48, axis=0)
        score = _dot_split3(ps, ov_ref[...])
        n_blk = -(-(PAST_LEN + DEC_SEQ) // SEL_BLK)
        cur = jnp.full((S_ROWS, 128), PAST_LEN // SEL_BLK, jnp.int32)
        sel = _select_blocks(score, cur, n_blk)
        key_mask = (_dot(sel, e_ref[...]) - 1.0) * BIG

        m = jnp.full((S_ROWS, 1), M_FLOOR, F32)
        for p in range(N_PAGES):
            s = (logits(lambda k: head_rows(pages[2 * p], k)) + bsel_ref[p]
                 + key_mask[:, p * PAGE_SIZE:(p + 1) * PAGE_SIZE])
            s_ref[i, p] = s
            m = jnp.maximum(m, rowmax(s))
        sn = logits(lambda k: nk_ref[i, :, col(k, 0)]) + bnew_ref[...]
        m = jnp.maximum(m, rowmax(sn))
        pn = jnp.exp(sn - m)
        l = rowsum(pn)
        acc = weighted(pn, lambda k: nk_ref[i, :, col(k, 1)])
        for p in range(N_PAGES):
            pe = jnp.exp(s_ref[i, p] - m)
            l = l + rowsum(pe)
            acc = acc + weighted(pe, lambda k: head_rows(pages[2 * p + 1], k))
        o_s = acc / jnp.where(l > 0.0, l, 1.0)

        sw = logits(lambda k: head_rows(kwin_ref, k)) + bwin_ref[...]
        sn = logits(lambda k: nw_ref[i, :, col(k, 0)]) + bnew_ref[...]
        m = jnp.maximum(jnp.maximum(rowmax(sw), rowmax(sn)), M_FLOOR)
        pw = jnp.exp(sw - m)
        pn = jnp.exp(sn - m)
        l = rowsum(pw) + rowsum(pn)
        o_w = weighted(pw, lambda k: head_rows(vwin_ref, k)) + weighted(pn, lambda k: nw_ref[i, :, col(k, 1)])
        o_w = o_w / jnp.where(l > 0.0, l, 1.0)

        gates = jax.nn.sigmoid(gs_ref[i])
        o_ref[i] = gates[:, 0:1] * o_c + gates[:, 1:2] * o_s + gates[:, 2:3] * o_w

    for i in range(ATT_SS):
        one_sequence(i)


def _attn_sample(pt_flat, q_s, g_s, ck, cv, cache_kv, nkv, state_win, nwin, bcmp, bsel, bnew, bwin, ov, emat):
    const2 = lambda shape: pl.BlockSpec(shape, lambda n, pt: (0, 0))
    per_seq = lambda rows, width: pl.BlockSpec((ATT_SS, rows, width), lambda n, pt: (n, 0, 0))
    page_specs = [pl.BlockSpec((None, None, PAGE_SIZE, None, N_KV, HEAD_DIM),
                               lambda n, pt, i=i, p=p, kind=kind:
                               (0, pt[(n * ATT_SS + i) * N_PAGES + p], 0, kind, 0, 0))
                  for i in range(ATT_SS) for p in range(N_PAGES) for kind in (2, 3)]
    win_specs = [pl.BlockSpec((None, None, WINDOW, None, N_KV, HEAD_DIM),
                              lambda n, pt, i=i, kind=kind: (0, n * ATT_SS + i, 0, kind, 0, 0))
                 for i in range(ATT_SS) for kind in (0, 1)]
    grid_spec = pltpu.PrefetchScalarGridSpec(
        num_scalar_prefetch=1,
        grid=(DEC_BATCH // ATT_SS,),
        in_specs=[
            per_seq(S_ROWS, HEAD_DIM), per_seq(S_ROWS, 128),
            per_seq(N_KV * SEG_ROWS, HEAD_DIM), per_seq(N_KV * SEG_ROWS, HEAD_DIM),
        ] + page_specs + win_specs + [
            per_seq(8, 1024), per_seq(8, 1024),
            const2((S_ROWS, 128)),
            pl.BlockSpec((N_PAGES, S_ROWS, 128), lambda n, pt: (0, 0, 0)),
            const2((S_ROWS, 128)),
            const2((S_ROWS, WINDOW)),
            const2((128, 128)),
            const2((128, PAST_LEN)),
        ],
        out_specs=per_seq(S_ROWS, HEAD_DIM),
        scratch_shapes=[
            pltpu.VMEM((ATT_SS, N_PAGES, S_ROWS, 128), F32),
            pltpu.VMEM((ATT_SS, 128, 1024), F32), pltpu.VMEM((ATT_SS, 128, 1024), F32),
        ],
    )
    return pl.pallas_call(
        _attn_sample_body,
        grid_spec=grid_spec,
        out_shape=jax.ShapeDtypeStruct((DEC_BATCH, S_ROWS, HEAD_DIM), F32),
        compiler_params=_cparams(1),
        name="attn_sample",
    )(pt_flat, q_s, g_s, ck, cv, *([cache_kv] * len(page_specs)), *([state_win] * len(win_specs)), nkv, nwin,
      bcmp, bsel, bnew, bwin, ov, emat)


def _t5_bucket(dist):
    d = jnp.maximum(dist, 0)
    df = jnp.maximum(d, 1).astype(F32)
    large = MAX_EXACT + (jnp.log(df / MAX_EXACT) / math.log(MAX_DIST / MAX_EXACT)
                         * (N_BUCKETS - MAX_EXACT)).astype(jnp.int32)
    large = jnp.minimum(large, N_BUCKETS - 1)
    return jnp.where(d < MAX_EXACT, d, large)


def _bias_lookup_body(rb_ref, idx_ref, o_ref):
    idx = idx_ref[0]
    for h in range(N_HEADS):
        acc = jnp.full(idx.shape, NEG, F32)
        for b in range(N_BUCKETS):
            acc = jnp.where(idx == b, rb_ref[b * N_HEADS + h], acc)
        o_ref[0, h] = acc


def _bias_table(rel_bias, dist, valid, name):
    p, r, _ = dist.shape
    idx = jnp.where(jnp.asarray(valid), _t5_bucket(jnp.asarray(dist, jnp.int32)), -1)
    return pl.pallas_call(
        _bias_lookup_body,
        grid=(p,),
        in_specs=[pl.BlockSpec(memory_space=pltpu.SMEM), pl.BlockSpec((1, r, 128), lambda i: (i, 0, 0))],
        out_specs=pl.BlockSpec((1, N_HEADS, r, 128), lambda i: (i, 0, 0, 0)),
        out_shape=jax.ShapeDtypeStruct((p, N_HEADS, r, 128), F32),
        compiler_params=_cparams(1),
        name=name,
    )(rel_bias.astype(F32).reshape(-1), idx)


def _overlap(nc, nb):
    cs = np.arange(nc)[:, None] * CMP_STRIDE
    js = np.arange(nb)[None, :] * SEL_BLK
    ov = np.clip(np.minimum(cs + CMP_LEN, js + SEL_BLK) - np.maximum(cs, js), 0, None) / CMP_LEN
    out = np.zeros((128, 128), np.float32)
    out[:nc, :nb] = ov
    return jnp.asarray(out, BF16)


def _position_tables(rel_bias):
    nc = SEG_ROWS - 1
    t = np.arange(128)[None, :, None]
    c = np.arange(128)[None, None, :]
    cend = c * CMP_STRIDE + CMP_LEN - 1
    d = np.arange(MASKED_TILE + 1)[:, None, None] * 128 + t - c
    tz = _bias_table(rel_bias, d, (d >= 0) & (d < WINDOW), "bias_tiles")
    d = np.arange(SEQ // 128)[:, None, None] * 128 + t - cend
    bias_cmp = _bias_table(rel_bias, d, (d >= 0) & (c < nc), "bias_cmp")
    ts = np.arange(8)[None, :, None]
    qpos = PAST_LEN + ts
    live = ts < DEC_SEQ
    d_cmp = qpos - cend
    d_sel = qpos - (np.arange(N_PAGES)[:, None, None] * PAGE_SIZE + c)
    d_new = ts - c
    d_win = qpos - (PAST_LEN - WINDOW + np.arange(WINDOW // 128)[:, None, None] * 128 + c)
    d = np.concatenate([d_cmp, d_sel, d_new, d_win], axis=0)
    valid = np.concatenate([(d_cmp >= 0) & (c < nc), d_sel >= 0, (d_new >= 0) & (c < DEC_SEQ),
                            (d_win >= 0) & (d_win < WINDOW)], axis=0) & live
    o = _bias_table(rel_bias, d, valid, "bias_sample")[:, :, :DEC_SEQ]
    o = jnp.transpose(o.reshape(-1, N_KV, GROUP, DEC_SEQ, 128), (0, 2, 1, 3, 4)).reshape(-1, S_ROWS, 128)
    bcmp, bsel, bnew = o[0], o[1:1 + N_PAGES], o[1 + N_PAGES]
    bwin = jnp.transpose(o[2 + N_PAGES:], (1, 0, 2)).reshape(S_ROWS, WINDOW)
    keys = np.arange(SEQ)
    e_all = (np.arange(128)[:, None] == (keys // SEL_BLK)[None, :]).astype(np.float32)
    e_tiles = jnp.asarray(e_all.reshape(128, SEQ // 256, 256).transpose(1, 0, 2), BF16)
    ovt_p = jnp.transpose(_overlap(nc, SEQ // SEL_BLK))[:SEQ // SEL_BLK]
    return dict(tz=tz, bias_cmp=bias_cmp, bcmp=bcmp, bsel=bsel, bnew=bnew, bwin=bwin,
                ovt_p=ovt_p, ov_s=_overlap(nc, -(-(PAST_LEN + DEC_SEQ) // SEL_BLK)),
                e_tiles=e_tiles, e_all=jnp.asarray(e_all, BF16))


W_TR = 512
W_TC = 2048
N_GNSA = 3 * N_HEADS


W_TB = 16


def _permute_w_in_body(a_ref, *refs):
    b_refs = refs[:N_GNSA // W_TB]
    g_ref, o_ref = refs[N_GNSA // W_TB:]
    j = pl.program_id(0)
    first, last = C_GRNN // W_TR, C_GNSA // W_TR

    @pl.when(j < first)
    def _():
        o_ref[...] = a_ref[...].astype(BF16)

    @pl.when((j >= first) & (j < last))
    def _():
        o_ref[0:W_TR - N_GNSA, :] = a_ref[N_GNSA:W_TR, :].astype(BF16)
        for t, b_ref in enumerate(b_refs):
            r0 = W_TR - N_GNSA + t * W_TB
            o_ref[r0:r0 + W_TB, :] = b_ref[...].astype(BF16)

    @pl.when(j == last)
    def _():
        o_ref[...] = g_ref[...]


def _permute_w_in(w_in):
    w_t = jnp.swapaxes(w_in, 1, 2)
    g_nsa = w_t[0, C_GRNN:C_GRNN + N_GNSA].reshape(N_KV, GROUP, 3, D_MODEL)
    g_nsa = jnp.transpose(g_nsa, (0, 2, 1, 3)).reshape(N_KV, 3 * GROUP, D_MODEL)
    g_nsa = jnp.pad(g_nsa, ((0, 0), (0, 128 - 3 * GROUP), (0, 0))).reshape(N_KV * 128, D_MODEL).astype(BF16)
    first, last = C_GRNN // W_TR, C_GNSA // W_TR
    n_b = N_GNSA // W_TB
    next_rows = lambda t: pl.BlockSpec(
        (None, W_TB, W_TC), lambda j, c: (0, jnp.clip(j + 1, first + 1, last) * (W_TR // W_TB) + t, c))
    return pl.pallas_call(
        _permute_w_in_body,
        grid=(D_Z // W_TR, D_MODEL // W_TC),
        in_specs=[pl.BlockSpec((None, W_TR, W_TC), lambda j, c: (0, jnp.minimum(j, last - 1), c))]
        + [next_rows(t) for t in range(n_b)]
        + [pl.BlockSpec((W_TR, W_TC), lambda j, c: (0, c))],
        out_specs=pl.BlockSpec((W_TR, W_TC), lambda j, c: (j, c)),
        out_shape=jax.ShapeDtypeStruct((D_Z, D_MODEL), BF16),
        compiler_params=_cparams(2),
        name="w_in_layout",
    )(w_t, *([w_t] * n_b), g_nsa)


def _cmp_weights(w1_k, w1_v, b1_k, b1_v, w2_k, w2_v, pos):
    def cat(w1):
        w = w1.reshape(2, CMP_STRIDE * HEAD_DIM, CMP_HID)
        return jnp.concatenate([w[0], w[1]], axis=1).reshape(N_PAIR, 256, 2 * CMP_HID)
    w1 = jnp.stack([cat(w1_k), cat(w1_v)]).astype(BF16)
    posm = jnp.pad(pos.reshape(2, CMP_STRIDE * HEAD_DIM), ((0, 6), (0, 0)))
    b1 = jnp.stack([b1_k, b1_v]).reshape(2, 1, CMP_HID)
    w2 = jnp.stack([w2_k, w2_v]).astype(BF16)
    return w1, posm, b1, w2


def kernel(x_prompt, x_sample, cache_kv, page_table, state_win, state_conv, state_h, rel_bias, ln_final, ln_ffn1, w_ffn1_gate, w_ffn1_up, w_ffn1_down, ln_mix, w_in, conv_w, conv_b, rg_wa, rg_ba, rg_wi, rg_bi, rg_lambda, cmp_pos, cmp_k_w1, cmp_k_b1, cmp_k_w2, cmp_v_w1, cmp_v_b1, cmp_v_w2, w_br_rnn, w_br_attn, w_out, ln_ffn2, w_ffn2_gate, w_ffn2_up, w_ffn2_down):
    tabs = _position_tables(rel_bias)
    x = jnp.concatenate([x_prompt.reshape(M_PROMPT, D_MODEL), x_sample.reshape(M_SAMPLE, D_MODEL)], axis=0)

    x = _ffn(x, ln_ffn1[0], w_ffn1_gate[0].astype(BF16), w_ffn1_up[0].astype(BF16), w_ffn1_down[0].astype(BF16),
             ln_final, False)
    z = _in_proj(x, ln_mix[0], _permute_w_in(w_in))
    z_s = z[M_PROMPT:]

    vec = lambda v: v.reshape(1, D_RNN)
    rnn_w = (conv_w[0], vec(conv_b[0]), rg_wa[0].astype(BF16), vec(rg_ba[0]), rg_wi[0].astype(BF16),
             vec(rg_bi[0]), vec(rg_lambda[0]))
    g_p, h_p = _rnn_prompt(z, *rnn_w)
    tmajor = lambda a: jnp.transpose(a.reshape(DEC_BATCH, -1, D_RNN), (1, 0, 2))
    g_s, h_s = _rnn_sample(tmajor(z_s[:, C_UGATE:C_UGATE + D_RNN]), tmajor(z_s[:, C_UX:C_UX + D_RNN]),
                           tmajor(state_conv[0]), state_h[0], *rnn_w)
    grnn = jnp.concatenate([g_p, jnp.transpose(g_s, (1, 0, 2)).reshape(M_SAMPLE, D_RNN)], axis=0)

    cw = _cmp_weights(cmp_k_w1[0], cmp_v_w1[0], cmp_k_b1[0], cmp_v_b1[0], cmp_k_w2[0], cmp_v_w2[0], cmp_pos[0])
    pt_prompt = jnp.arange(BATCH * N_PAGES, dtype=jnp.int32)
    pt_sample = page_table.reshape(-1).astype(jnp.int32)
    ck_p, cv_p = _compress(z.reshape(M_TOK // PAGE_SIZE, PAGE_SIZE, D_Z), pt_prompt, C_PAG // 1024, BATCH, *cw,
                           name="compress_prompt")
    ck_s, cv_s = _compress(cache_kv, pt_sample, None, DEC_BATCH, *cw, name="compress_sample")
    o_p = _attn_prompt(z, ck_p.reshape(BATCH * N_KV, SEG_ROWS, HEAD_DIM), cv_p.reshape(BATCH * N_KV, SEG_ROWS, HEAD_DIM),
                       tabs["bias_cmp"], tabs["tz"], tabs["ovt_p"], tabs["e_tiles"])

    def rows_gkt(a, width):
        a = a.reshape(DEC_BATCH, DEC_SEQ, N_KV, GROUP, width)
        return jnp.transpose(a, (0, 3, 2, 1, 4)).reshape(DEC_BATCH, S_ROWS, width)

    q_s = rows_gkt(z_s[:, C_Q:C_Q + N_HEADS * HEAD_DIM], HEAD_DIM)
    gn = z_s[:, C_GNSA:].reshape(M_SAMPLE, N_KV, 128)[:, :, :3 * GROUP].reshape(M_SAMPLE, N_KV, 3, GROUP)
    g_s3 = jnp.pad(rows_gkt(jnp.transpose(gn, (0, 1, 3, 2)), 3), ((0, 0), (0, 0), (0, 125)))
    pad8 = lambda a: jnp.pad(a.reshape(DEC_BATCH, DEC_SEQ, -1), ((0, 0), (0, 8 - DEC_SEQ), (0, 0)))
    nkv = pad8(z_s[:, C_PAG + 2 * N_KV * HEAD_DIM:C_PAG + 4 * N_KV * HEAD_DIM])
    nwin = pad8(z_s[:, C_WIN:C_WIN + 2 * N_KV * HEAD_DIM])
    o_s = _attn_sample(pt_sample, q_s, g_s3, ck_s, cv_s, cache_kv, nkv, state_win, nwin,
                       tabs["bcmp"], tabs["bsel"], tabs["bnew"], tabs["bwin"], tabs["ov_s"], tabs["e_all"])
    o_s = jnp.transpose(o_s.reshape(DEC_BATCH, GROUP, N_KV, DEC_SEQ, HEAD_DIM), (0, 3, 2, 1, 4))
    oattn = jnp.concatenate([o_p, o_s.reshape(M_SAMPLE, N_HEADS * HEAD_DIM).astype(BF16)], axis=0)

    merged = _merge(z, grnn, oattn, w_br_rnn[0].astype(BF16), w_br_attn[0].astype(BF16))
    x = _out_proj(x, merged, w_out[0].astype(BF16))
    ffn2 = functools.partial(_ffn, x, ln_ffn2[0], w_ffn2_gate[0].astype(BF16), w_ffn2_up[0].astype(BF16),
                             w_ffn2_down[0].astype(BF16), ln_final, True)
    y_p = ffn2(tile0=0, n_tiles=M_PROMPT // TM)
    y_s = ffn2(tile0=M_PROMPT // TM, n_tiles=M_SAMPLE // TM)

    kv = z[:, C_PAG:C_PAG + 4 * N_KV * HEAD_DIM]
    wn = z[:, C_WIN:C_WIN + 2 * N_KV * HEAD_DIM]
    keep = CONV_W - 1
    conv_p = jnp.stack([lax.slice(z, ((n + 1) * SEQ - keep, C_UX), ((n + 1) * SEQ, C_UX + D_RNN))
                        for n in range(BATCH)])
    conv_s = z_s[:, C_UX:C_UX + D_RNN].reshape(DEC_BATCH, DEC_SEQ, D_RNN)[:, DEC_SEQ - keep:]
    win_p = wn[:M_PROMPT].reshape(BATCH, SEQ, 2, N_KV, HEAD_DIM)[:, SEQ - WINDOW:]
    win_s = jnp.concatenate([state_win.reshape(DEC_BATCH, WINDOW, 2, N_KV, HEAD_DIM),
                             wn[M_PROMPT:].reshape(DEC_BATCH, DEC_SEQ, 2, N_KV, HEAD_DIM)], axis=1)
    return (
        y_p.reshape(BATCH, SEQ, D_MODEL),
        y_s.reshape(DEC_BATCH, DEC_SEQ, D_MODEL),
        kv[:M_PROMPT].reshape(1, BATCH, SEQ, 4, N_KV, HEAD_DIM),
        kv[M_PROMPT:].reshape(1, DEC_BATCH, DEC_SEQ, 4, N_KV, HEAD_DIM),
        win_p[None],
        win_s[None, :, DEC_SEQ:],
        conv_p[None],
        conv_s[None],
        h_p[None, :, 7],
        h_s[None],
    )
```
